```python
import math
import jax, jax.numpy as jnp
from jax import lax
import numpy as np

D_MODEL = 1024
BATCH = 8
SEQ = 2048
DEPTH = 4
DEC_BATCH = 32
DEC_SEQ = 8
PAST_LEN = 8192
PAGE_SIZE = 128

N_MIXERS = 2
N_HEADS = 8
HEAD_DIM = D_MODEL // N_HEADS
N_KV_HEADS = 2
GROUP = N_HEADS // N_KV_HEADS
IDX_HEADS = 4
IDX_DIM = 64
IDX_W_SCALE = (IDX_HEADS * IDX_DIM) ** -0.5
TOPK_MAX = 256
Q_BLOCK = 128
SG_CHUNK = 128
D_SG = D_MODEL
SG_GROUPS = 8
SG_GROUP_DIM = D_SG // SG_GROUPS
D_FF = 2816
CONV_W = 3
PLE_DIM = 256
ROPE_THETA = 10000.0
EPS = 1e-6

kernel_name = "dsa_gmlp_hybrid_decode_step"


def _rms(x, g):
    xf = x.astype(jnp.float32)
    n = xf * lax.rsqrt(jnp.mean(xf * xf, axis=-1, keepdims=True) + EPS)
    return (n * g.astype(jnp.float32)).astype(x.dtype)


def _rope(x, pos):
    half = x.shape[-1] // 2
    inv = ROPE_THETA ** (-jnp.arange(half, dtype=jnp.float32) / half)
    ang = pos.astype(jnp.float32)[:, None] * inv[None, :]
    cos = jnp.cos(ang)[None, :, None, :]
    sin = jnp.sin(ang)[None, :, None, :]
    x1 = x[..., :half].astype(jnp.float32)
    x2 = x[..., half:].astype(jnp.float32)
    out = jnp.concatenate([x1 * cos - x2 * sin, x2 * cos + x1 * sin], axis=-1)
    return out.astype(x.dtype)


def _a_project(xn, w_in, pos):
    B, T, _ = xn.shape
    sizes = (N_HEADS * HEAD_DIM, N_KV_HEADS * HEAD_DIM, N_KV_HEADS * HEAD_DIM,
             IDX_HEADS * IDX_DIM, IDX_DIM, IDX_HEADS)
    offs = np.cumsum(sizes)[:-1].tolist()
    q, k, v, iq, ik, iw = jnp.split(xn @ w_in, offs, axis=-1)
    q = _rope(q.reshape(B, T, N_HEADS, HEAD_DIM), pos)
    k = _rope(k.reshape(B, T, N_KV_HEADS, HEAD_DIM), pos)
    v = v.reshape(B, T, N_KV_HEADS, HEAD_DIM)
    iq = _rope(iq.reshape(B, T, IDX_HEADS, IDX_DIM), pos)
    ik = _rope(ik.reshape(B, T, 1, IDX_DIM), pos)[:, :, 0]
    iw = iw * IDX_W_SCALE
    return q, k, v, iq, ik, iw


def _dsa_core(q, iq, iw, ik_all, q_pos, gather_kv, k_sel):
    B, T = q.shape[:2]
    L = ik_all.shape[1]
    s = jnp.einsum('bthd,bsd->bths', iq.astype(jnp.float32), ik_all.astype(jnp.float32))
    score = jnp.einsum('bths,bth->bts', jax.nn.relu(s), iw.astype(jnp.float32))
    admissible = jnp.arange(L)[None, :] <= q_pos[:, None]
    score = jnp.where(admissible[None], score, -jnp.inf)
    _, sel = lax.top_k(score, k_sel)
    k_g, v_g = gather_kv(sel)
    valid = sel <= q_pos[None, :, None]
    qg = q.reshape(B, T, N_KV_HEADS, GROUP, HEAD_DIM)
    logits = jnp.einsum('btkgd,btskd->btkgs', qg, k_g).astype(jnp.float32) * (HEAD_DIM ** -0.5)
    logits = jnp.where(valid[:, :, None, None, :], logits, -jnp.inf)
    probs = jax.nn.softmax(logits, axis=-1).astype(v_g.dtype)
    o = jnp.einsum('btkgs,btskd->btkgd', probs, v_g)
    return o.reshape(B, T, N_HEADS * HEAD_DIM)


def _attn_prompt(xn, w_in, w_out):
    B, S, _ = xn.shape
    pos = jnp.arange(S)
    q, k, v, iq, ik, iw = _a_project(xn, w_in, pos)
    k_sel = min(TOPK_MAX, S // 4)
    nb = S // Q_BLOCK
    take = jax.vmap(lambda a, i: a[i])

    def gather(sel):
        return take(k, sel), take(v, sel)

    def blk(args):
        qb, iqb, iwb, pb = args
        return _dsa_core(qb, iqb, iwb, ik, pb, gather, k_sel)

    to_blocks = lambda a: jnp.moveaxis(a.reshape(B, nb, Q_BLOCK, *a.shape[2:]), 1, 0)
    out = lax.map(blk, (to_blocks(q), to_blocks(iq), to_blocks(iw), pos.reshape(nb, Q_BLOCK)))
    out = jnp.moveaxis(out, 0, 1).reshape(B, S, N_HEADS * HEAD_DIM)
    return out @ w_out, k, v, ik


def _attn_sample(xn, pool_k, pool_v, pool_ik, page_table, w_in, w_out):
    B, T, _ = xn.shape
    page = pool_k.shape[1]
    P = page_table.shape[1] * page
    pos = P + jnp.arange(T)
    q, k, v, iq, ik, iw = _a_project(xn, w_in, pos)
    ik_past = pool_ik[page_table].reshape(B, P, IDX_DIM)
    ik_all = jnp.concatenate([ik_past.astype(ik.dtype), ik], axis=1)
    k_sel = min(TOPK_MAX, (P + T) // 4)
    take = jax.vmap(lambda a, i: a[i])

    def gather(sel):
        is_past = (sel < P)[..., None, None]
        sp = jnp.minimum(sel, P - 1)
        phys = jnp.take_along_axis(page_table, (sp // page).reshape(B, -1), axis=1).reshape(sel.shape)
        off = sp % page
        jn = jnp.clip(sel - P, 0, T - 1)
        kg = jnp.where(is_past, pool_k[phys, off].astype(k.dtype), take(k, jn))
        vg = jnp.where(is_past, pool_v[phys, off].astype(v.dtype), take(v, jn))
        return kg, vg

    o = _dsa_core(q, iq, iw, ik_all, pos, gather, k_sel)
    return o @ w_out, k, v, ik


def _sgu(xn, w_in, b_in, g_v, w_s, b_s, w_out):
    z = jax.nn.gelu(xn @ w_in + b_in, approximate=False)
    u, v = jnp.split(z, 2, axis=-1)
    v = _rms(v, g_v)
    B, T, _ = v.shape
    n = -(-T // SG_CHUNK)
    vp = jnp.pad(v, ((0, 0), (0, n * SG_CHUNK - T), (0, 0))).reshape(B, n, SG_CHUNK, SG_GROUPS, SG_GROUP_DIM)
    ws = w_s * jnp.tril(jnp.ones((SG_CHUNK, SG_CHUNK), w_s.dtype))
    mixed = jnp.einsum('gts,bnsgc->bntgc', ws, vp) + b_s.T[None, None, :, :, None]
    mixed = mixed.reshape(B, n * SG_CHUNK, D_SG)[:, :T]
    return (u * mixed) @ w_out, v


def _conv_ffn(xn, prev, w_up, cw, cb, w_down):
    up = xn @ w_up
    T = up.shape[1]
    full = jnp.concatenate([prev.astype(up.dtype), up], axis=1)
    h = cb + sum(cw[j] * full[:, j:j + T] for j in range(CONV_W))
    g, val = jnp.split(h, 2, axis=-1)
    return (jax.nn.silu(g) * val) @ w_down, full[:, -(CONV_W - 1):]


def _ple(h, p, g, w_p, w_gate):
    return h + (p @ w_p) * jax.nn.sigmoid(_rms(h, g) @ w_gate)


def setup_inputs(seed: int = 0) -> dict:
    key = jax.random.key(seed)
    ks = iter(jax.random.split(key, 40))
    nrm = lambda shape, scale: jax.random.normal(next(ks), shape, jnp.float32) * scale
    n_a = (DEPTH + N_MIXERS - 1) // N_MIXERS
    n_b = DEPTH // N_MIXERS
    n_pages = PAST_LEN // PAGE_SIZE
    used = DEC_BATCH * n_pages
    n_pool = used + (used + 3) // 4
    a_w = N_HEADS * HEAD_DIM + 2 * N_KV_HEADS * HEAD_DIM + IDX_HEADS * IDX_DIM + IDX_DIM + IDX_HEADS
    d = D_MODEL
    return {
        'x_prompt': nrm((BATCH, SEQ, d), 1.0),
        'x_sample': nrm((DEC_BATCH, DEC_SEQ, d), 1.0),
        'cache_k': nrm((n_a, n_pool, PAGE_SIZE, N_KV_HEADS, HEAD_DIM), 1.0),
        'cache_v': nrm((n_a, n_pool, PAGE_SIZE, N_KV_HEADS, HEAD_DIM), 1.0),
        'cache_idx_k': nrm((n_a, n_pool, PAGE_SIZE, IDX_DIM), 1.0),
        'state_conv': nrm((DEPTH, DEC_BATCH, CONV_W - 1, 2 * D_FF), 1.0),
        'page_table': jax.random.permutation(next(ks), n_pool)[:used].reshape(DEC_BATCH, n_pages).astype(jnp.int32),
        'p_prompt': nrm((DEPTH, BATCH, SEQ, PLE_DIM), 1.0),
        'p_sample': nrm((DEPTH, DEC_BATCH, DEC_SEQ, PLE_DIM), 1.0),
        'norm_mix': 1.0 + nrm((DEPTH, d), 0.01),
        'w_attn_in': nrm((n_a, d, a_w), d ** -0.5),
        'w_attn_out': nrm((n_a, N_HEADS * HEAD_DIM, d), (N_HEADS * HEAD_DIM) ** -0.5),
        'w_sg_in': nrm((n_b, d, 2 * D_SG), d ** -0.5),
        'b_sg_in': nrm((n_b, 2 * D_SG), 0.01),
        'norm_sg_v': 1.0 + nrm((n_b, D_SG), 0.01),
        'w_sg_spatial': nrm((n_b, SG_GROUPS, SG_CHUNK, SG_CHUNK), 0.5 * SG_CHUNK ** -0.5),
        'b_sg_spatial': 1.0 + nrm((n_b, SG_GROUPS, SG_CHUNK), 0.01),
        'w_sg_out': nrm((n_b, D_SG, d), D_SG ** -0.5),
        'norm_ffn': 1.0 + nrm((DEPTH, d), 0.01),
        'w_ffn_up': nrm((DEPTH, d, 2 * D_FF), d ** -0.5),
        'w_ffn_conv': nrm((DEPTH, CONV_W, 2 * D_FF), CONV_W ** -0.5),
        'b_ffn_conv': nrm((DEPTH, 2 * D_FF), 0.01),
        'w_ffn_down': nrm((DEPTH, D_FF, d), D_FF ** -0.5),
        'norm_ple': 1.0 + nrm((DEPTH, d), 0.01),
        'w_ple': nrm((DEPTH, PLE_DIM, d), PLE_DIM ** -0.5),
        'w_ple_gate': nrm((DEPTH, d, d), d ** -0.5),
        'norm_final': 1.0 + nrm((d,), 0.01),
    }


def reference(x_prompt, x_sample, cache_k, cache_v, cache_idx_k, state_conv, page_table, p_prompt, p_sample,
              norm_mix, w_attn_in, w_attn_out, w_sg_in, b_sg_in, norm_sg_v, w_sg_spatial, b_sg_spatial, w_sg_out,
              norm_ffn, w_ffn_up, w_ffn_conv, b_ffn_conv, w_ffn_down, norm_ple, w_ple, w_ple_gate, norm_final):
    hp, hs = x_prompt, x_sample
    kp_l, vp_l, ikp_l, ks_l, vs_l, iks_l, cp_l, cs_l, sgv_l = [], [], [], [], [], [], [], [], []
    for i in range(DEPTH):
        j = i // N_MIXERS
        xp = _rms(hp, norm_mix[i])
        xs = _rms(hs, norm_mix[i])
        if i % N_MIXERS == 0:
            yp, k_, v_, ik_ = _attn_prompt(xp, w_attn_in[j], w_attn_out[j])
            kp_l.append(k_); vp_l.append(v_); ikp_l.append(ik_)
            ys, k_, v_, ik_ = _attn_sample(xs, cache_k[j], cache_v[j], cache_idx_k[j], page_table,
                                           w_attn_in[j], w_attn_out[j])
            ks_l.append(k_); vs_l.append(v_); iks_l.append(ik_)
        else:
            yp, _ = _sgu(xp, w_sg_in[j], b_sg_in[j], norm_sg_v[j], w_sg_spatial[j], b_sg_spatial[j], w_sg_out[j])
            ys, vrow = _sgu(xs, w_sg_in[j], b_sg_in[j], norm_sg_v[j], w_sg_spatial[j], b_sg_spatial[j], w_sg_out[j])
            sgv_l.append(vrow)
        hp = hp + yp
        hs = hs + ys
        zero_buf = jnp.zeros((hp.shape[0], CONV_W - 1, 2 * D_FF), hp.dtype)
        yp, cp = _conv_ffn(_rms(hp, norm_ffn[i]), zero_buf, w_ffn_up[i], w_ffn_conv[i], b_ffn_conv[i], w_ffn_down[i])
        ys, cs = _conv_ffn(_rms(hs, norm_ffn[i]), state_conv[i], w_ffn_up[i], w_ffn_conv[i], b_ffn_conv[i], w_ffn_down[i])
        cp_l.append(cp); cs_l.append(cs)
        hp = _ple(hp + yp, p_prompt[i], norm_ple[i], w_ple[i], w_ple_gate[i])
        hs = _ple(hs + ys, p_sample[i], norm_ple[i], w_ple[i], w_ple_gate[i])
    y_prompt = _rms(hp, norm_final)
    y_sample = _rms(hs, norm_final)
    return (y_prompt, y_sample,
            jnp.stack(kp_l), jnp.stack(vp_l), jnp.stack(ikp_l),
            jnp.stack(ks_l), jnp.stack(vs_l), jnp.stack(iks_l),
            jnp.stack(cp_l), jnp.stack(cs_l), jnp.stack(sgv_l))
```

```python
import functools

import jax
import jax.numpy as jnp
from jax import lax
from jax.experimental import pallas as pl
from jax.experimental.pallas import tpu as pltpu

F32 = jnp.float32
BF16 = jnp.bfloat16
I32 = jnp.int32

N_HEADS = 8
N_KV_HEADS = 2
GROUP = N_HEADS // N_KV_HEADS
HEAD_DIM = 128
IDX_HEADS = 4
IDX_DIM = 64
IDX_W_SCALE = (IDX_HEADS * IDX_DIM) ** -0.5
TOPK_MAX = 256
SG_CHUNK = 128
SG_GROUPS = 8
CONV_W = 3
ROPE_THETA = 10000.0
EPS = 1e-6

LANES = 128
SUBLANES = 8
V7X_VMEM_BYTES = 64 * 1024 * 1024
VMEM_LIMIT_BYTES = V7X_VMEM_BYTES - 8 * 1024 * 1024

NEG_INF = float("-inf")
INT_MIN = -(2 ** 31)


def _params(grid_rank=1):
    return pltpu.CompilerParams(dimension_semantics=("arbitrary",) * grid_rank,
                                vmem_limit_bytes=VMEM_LIMIT_BYTES)


def _log2(n):
    assert n > 0 and n & (n - 1) == 0, n
    return n.bit_length() - 1


def _rms(x, g):
    ms = jnp.mean(x * x, axis=-1, keepdims=True)
    return x * lax.rsqrt(ms + EPS) * g


def _sigmoid(x):
    return 1.0 / (1.0 + jnp.exp(-x))


def _dot(a, b):
    return jnp.dot(a, b, preferred_element_type=F32)


def _dot_nt(a, b):
    return lax.dot_general(a, b, (((1,), (1,)), ((), ())), preferred_element_type=F32)


Q_OFF = 0
K_OFF = N_HEADS * HEAD_DIM
V_OFF = K_OFF + N_KV_HEADS * HEAD_DIM
IQ_OFF = V_OFF + N_KV_HEADS * HEAD_DIM
MISC_OFF = IQ_OFF + IDX_HEADS * IDX_DIM
PROJ_W = MISC_OFF + LANES


def _proj_kernel(h_ref, g_ref, w_ref, c128_ref, s128_ref, c64_ref, s64_ref, cm_ref, sm_ref,
                 q8_ref, k_ref, v_ref, kb_ref, vb_ref, iq4_ref, misc_ref, ikb_ref):
    tm = h_ref.shape[0]
    xn = _rms(h_ref[...], g_ref[...]).astype(BF16)
    y = _dot(xn, w_ref[...])
    c128 = c128_ref[...]
    s128 = s128_ref[...]

    def rope128(t):
        return t * c128 + pltpu.roll(t, HEAD_DIM // 2, 1) * s128

    for h in range(N_HEADS):
        q8_ref[h] = rope128(y[:, Q_OFF + h * HEAD_DIM:Q_OFF + (h + 1) * HEAD_DIM]).astype(q8_ref.dtype)
    for h in range(N_KV_HEADS):
        kh = rope128(y[:, K_OFF + h * HEAD_DIM:K_OFF + (h + 1) * HEAD_DIM])
        k_ref[:, h * HEAD_DIM:(h + 1) * HEAD_DIM] = kh
        kb_ref[:, h * HEAD_DIM:(h + 1) * HEAD_DIM] = kh.astype(BF16)
    v = y[:, V_OFF:IQ_OFF]
    v_ref[...] = v
    vb_ref[...] = v.astype(BF16)

    lane = lax.broadcasted_iota(I32, (tm, LANES), 1)
    low_half = (lane & (IDX_DIM - 1)) < (IDX_DIM // 2)

    def rope64(t, c, s):
        partner = jnp.where(low_half, pltpu.roll(t, LANES - IDX_DIM // 2, 1), pltpu.roll(t, IDX_DIM // 2, 1))
        return t * c + partner * s

    c64 = c64_ref[...]
    s64 = s64_ref[...]
    for pair in range(IDX_HEADS // 2):
        t = rope64(y[:, IQ_OFF + pair * LANES:IQ_OFF + (pair + 1) * LANES], c64, s64)
        iq4_ref[2 * pair] = t[:, :IDX_DIM].astype(iq4_ref.dtype)
        iq4_ref[2 * pair + 1] = t[:, IDX_DIM:].astype(iq4_ref.dtype)
    m = rope64(y[:, MISC_OFF:MISC_OFF + LANES], cm_ref[...], sm_ref[...])
    misc_ref[...] = m
    ikb_ref[...] = m[:, :IDX_DIM].astype(BF16)


def _proj_call(h, g, w, tabs, tm, qdtype):
    n, d = h.shape
    t_tab = tabs[0].shape[0]
    n_tab = t_tab // tm
    tab_spec = pl.BlockSpec((tm, LANES), lambda i: (i % n_tab, 0))
    kvw = N_KV_HEADS * HEAD_DIM
    out_shape = (
        jax.ShapeDtypeStruct((N_HEADS, n, HEAD_DIM), qdtype),
        jax.ShapeDtypeStruct((n, kvw), F32),
        jax.ShapeDtypeStruct((n, kvw), F32),
        jax.ShapeDtypeStruct((n, kvw), BF16),
        jax.ShapeDtypeStruct((n, kvw), BF16),
        jax.ShapeDtypeStruct((IDX_HEADS, n, IDX_DIM), qdtype),
        jax.ShapeDtypeStruct((n, LANES), F32),
        jax.ShapeDtypeStruct((n, IDX_DIM), BF16),
    )
    out_specs = (
        pl.BlockSpec((N_HEADS, tm, HEAD_DIM), lambda i: (0, i, 0)),
        pl.BlockSpec((tm, kvw), lambda i: (i, 0)),
        pl.BlockSpec((tm, kvw), lambda i: (i, 0)),
        pl.BlockSpec((tm, kvw), lambda i: (i, 0)),
        pl.BlockSpec((tm, kvw), lambda i: (i, 0)),
        pl.BlockSpec((IDX_HEADS, tm, IDX_DIM), lambda i: (0, i, 0)),
        pl.BlockSpec((tm, LANES), lambda i: (i, 0)),
        pl.BlockSpec((tm, IDX_DIM), lambda i: (i, 0)),
    )
    return pl.pallas_call(
        _proj_kernel,
        grid=(n // tm,),
        in_specs=[
            pl.BlockSpec((tm, d), lambda i: (i, 0)),
            pl.BlockSpec((1, d), lambda i: (0, 0)),
            pl.BlockSpec((d, PROJ_W), lambda i: (0, 0)),
        ] + [tab_spec] * 6,
        out_specs=out_specs,
        out_shape=out_shape,
        compiler_params=_params(),
        name="attn_proj",
    )(h, g.reshape(1, d), w, *tabs)


def _rope_tables(pos):
    t = pos.shape[0]

    def tab(half):
        inv = ROPE_THETA ** (-jnp.arange(half, dtype=F32) / half)
        ang = pos.astype(F32)[:, None] * inv[None, :]
        return jnp.cos(ang), jnp.sin(ang)

    c64, s64 = tab(HEAD_DIM // 2)
    c32, s32 = tab(IDX_DIM // 2)
    c128 = jnp.concatenate([c64, c64], axis=1)
    s128 = jnp.concatenate([-s64, s64], axis=1)
    cq = jnp.concatenate([c32, c32, c32, c32], axis=1)
    sq = jnp.concatenate([-s32, s32, -s32, s32], axis=1)
    cm = jnp.concatenate([c32, c32, jnp.full((t, LANES - IDX_DIM), IDX_W_SCALE, F32)], axis=1)
    sm = jnp.concatenate([-s32, s32, jnp.zeros((t, LANES - IDX_DIM), F32)], axis=1)
    return (c128, s128, cq, sq, cm, sm)


def _sortable_key(score):
    b = pltpu.bitcast(score, I32)
    b = jnp.where(b == INT_MIN, 0, b)
    return b ^ ((b >> 31) & 0x7FFFFFFF)


def _count(key_ref, indicator):
    r, l = key_ref.shape
    acc = jnp.zeros((r, LANES), I32)
    for c in range(l // LANES):
        acc = acc + indicator(key_ref[:, c * LANES:(c + 1) * LANES], c)
    return jnp.sum(acc, axis=-1, keepdims=True)


def _select_bias(key_ref, bias_ref, jmax_ref, kq):
    r, l = key_ref.shape

    def ge_body(i, t):
        cand = t + lax.shift_left(jnp.int32(1), 31 - i)
        candb = jnp.broadcast_to(cand, (r, LANES))
        cnt = _count(key_ref, lambda k, c: jnp.where(k >= candb, 1, 0))
        return jnp.where(cnt >= kq, cand, t)

    thr = lax.fori_loop(0, 32, ge_body, jnp.full((r, 1), INT_MIN, I32))
    thrb = jnp.broadcast_to(thr, (r, LANES))
    cnt_gt = _count(key_ref, lambda k, c: jnp.where(k > thrb, 1, 0))
    cnt_ge = _count(key_ref, lambda k, c: jnp.where(k >= thrb, 1, 0))
    need = kq - cnt_gt

    jmax_ref[...] = jnp.full(jmax_ref.shape, l, I32)

    @pl.when(jnp.max(cnt_ge - kq) > 0)
    def _():
        nbits = max(1, (l - 1).bit_length())
        lane = lax.broadcasted_iota(I32, (r, LANES), 1)

        def lt_body(i, j):
            cand = j + lax.shift_left(jnp.int32(1), nbits - 1 - i)
            candb = jnp.broadcast_to(cand, (r, LANES))
            cnt = _count(key_ref, lambda k, c: jnp.where(k == thrb, jnp.where(lane + c * LANES < candb, 1, 0), 0))
            return jnp.where(cnt < need, cand, j)

        jmax_ref[...] = jnp.broadcast_to(lax.fori_loop(0, nbits, lt_body, jnp.zeros((r, 1), I32)), jmax_ref.shape)

    jmaxb = jmax_ref[...]
    lane = lax.broadcasted_iota(I32, (r, LANES), 1)
    for c in range(l // LANES):
        k = key_ref[:, c * LANES:(c + 1) * LANES]
        keep_tie = jnp.where(lane + c * LANES <= jmaxb, 0.0, NEG_INF)
        bias_ref[:, c * LANES:(c + 1) * LANES] = jnp.where(k > thrb, 0.0, jnp.where(k == thrb, keep_tie, NEG_INF))


def _indexer_keys(s, iw, qpos, r, l):
    score = jnp.zeros((r, l), F32)
    for h in range(IDX_HEADS):
        score = score + jnp.maximum(s[h * r:(h + 1) * r], 0.0) * iw[:, h:h + 1]
    kpos = lax.broadcasted_iota(I32, (r, l), 1)
    score = jnp.where(kpos <= qpos, score, NEG_INF)
    return _sortable_key(score)


def _masked_attention(q, k, v, bias, r):
    logits = _dot_nt(q, k) * (HEAD_DIM ** -0.5)
    logits = logits + jnp.concatenate([bias] * GROUP, axis=0)
    m = jnp.max(logits, axis=-1, keepdims=True)
    e = jnp.exp(logits - m)
    den = jnp.sum(e, axis=-1, keepdims=True)
    return _dot(e.astype(BF16), v) / den


def _dsa_prompt_kernel(h_ref, q8_ref, iq4_ref, misc_ref, ik_ref, k_ref, v_ref, wo_ref, o_ref,
                       key_ref, bias_ref, jmax_ref, att_ref, *, k_sel):
    tq = h_ref.shape[0]
    s_len = ik_ref.shape[0]
    qi = pl.program_id(1)
    qpos = qi * tq + lax.broadcasted_iota(I32, (tq, 1), 0)

    iq = iq4_ref[...].reshape(IDX_HEADS * tq, IDX_DIM)
    s = _dot_nt(iq, ik_ref[...])
    iw = misc_ref[:, IDX_DIM:IDX_DIM + IDX_HEADS]
    key_ref[...] = _indexer_keys(s, iw, qpos, tq, s_len)
    kq = jnp.minimum(k_sel, qpos + 1)
    _select_bias(key_ref, bias_ref, jmax_ref, kq)

    bias = bias_ref[...]
    for g in range(N_KV_HEADS):
        q = q8_ref[g * GROUP:(g + 1) * GROUP].reshape(GROUP * tq, HEAD_DIM)
        o = _masked_attention(q, k_ref[:, g * HEAD_DIM:(g + 1) * HEAD_DIM],
                              v_ref[:, g * HEAD_DIM:(g + 1) * HEAD_DIM], bias, tq)
        for hh in range(GROUP):
            head = g * GROUP + hh
            att_ref[:, head * HEAD_DIM:(head + 1) * HEAD_DIM] = o[hh * tq:(hh + 1) * tq].astype(BF16)
    o_ref[...] = h_ref[...] + _dot(att_ref[...], wo_ref[...])


def _dsa_prompt_call(h, q8, iq4, misc, ikb, kb, vb, wo, batch, seq, tq):
    n, d = h.shape
    nq = seq // tq
    kvw = N_KV_HEADS * HEAD_DIM
    k_sel = min(TOPK_MAX, seq // 4)
    return pl.pallas_call(
        functools.partial(_dsa_prompt_kernel, k_sel=k_sel),
        grid=(batch, nq),
        in_specs=[
            pl.BlockSpec((tq, d), lambda b, q: (b * nq + q, 0)),
            pl.BlockSpec((N_HEADS, tq, HEAD_DIM), lambda b, q: (0, b * nq + q, 0)),
            pl.BlockSpec((IDX_HEADS, tq, IDX_DIM), lambda b, q: (0, b * nq + q, 0)),
            pl.BlockSpec((tq, LANES), lambda b, q: (b * nq + q, 0)),
            pl.BlockSpec((seq, IDX_DIM), lambda b, q: (b, 0)),
            pl.BlockSpec((seq, kvw), lambda b, q: (b, 0)),
            pl.BlockSpec((seq, kvw), lambda b, q: (b, 0)),
            pl.BlockSpec((d, d), lambda b, q: (0, 0)),
        ],
        out_specs=pl.BlockSpec((tq, d), lambda b, q: (b * nq + q, 0)),
        out_shape=jax.ShapeDtypeStruct((n, d), F32),
        scratch_shapes=[
            pltpu.VMEM((tq, seq), I32),
            pltpu.VMEM((tq, seq), F32),
            pltpu.VMEM((tq, LANES), I32),
            pltpu.VMEM((tq, d), BF16),
        ],
        compiler_params=_params(grid_rank=2),
        name="dsa_prompt",
    )(h, q8, iq4, misc, ikb, kb, vb, wo)


def _dsa_sample_kernel(pt_ref, h_ref, q8_ref, iq4_ref, misc_ref, kn_ref, vn_ref,
                       ck_ref, cv_ref, cik_ref, wo_ref, o_ref,
                       kbuf, vbuf, ikbuf, key_ref, bias_ref, jmax_ref, att_ref, sems, *, k_sel, layer):
    ts = kn_ref.shape[0]
    page = ck_ref.shape[2]
    n_pages = pt_ref.shape[1]
    past = n_pages * page
    l_pad = kbuf.shape[0]
    b = pl.program_id(0)
    nb = pl.num_programs(0)

    def page_copies(p):
        src = pt_ref[b, p]
        rows = pl.ds(pl.multiple_of(p * page, page), page)
        return (pltpu.make_async_copy(ck_ref.at[layer, src], kbuf.at[rows], sems.at[0]),
                pltpu.make_async_copy(cv_ref.at[layer, src], vbuf.at[rows], sems.at[1]),
                pltpu.make_async_copy(cik_ref.at[layer, src], ikbuf.at[rows], sems.at[2]))

    def start_page(p, carry):
        for cp in page_copies(p):
            cp.start()
        return carry

    def wait_page(p, carry):
        for cp in page_copies(p):
            cp.wait()
        return carry

    lax.fori_loop(0, n_pages, start_page, 0)

    tail = l_pad - past - ts
    kbuf[past:past + ts, :] = kn_ref[...]
    vbuf[past:past + ts, :] = vn_ref[...]
    ikbuf[past:past + ts, :] = misc_ref[:, :IDX_DIM]
    kbuf[past + ts:, :] = jnp.zeros((tail, kbuf.shape[1]), F32)
    vbuf[past + ts:, :] = jnp.zeros((tail, vbuf.shape[1]), F32)
    ikbuf[past + ts:, :] = jnp.zeros((tail, IDX_DIM), F32)

    lax.fori_loop(0, n_pages, wait_page, 0)

    qpos = past + lax.broadcasted_iota(I32, (ts, 1), 0)
    iq = iq4_ref[...].reshape(IDX_HEADS * ts, IDX_DIM).astype(BF16)
    s = _dot_nt(iq, ikbuf[...].astype(BF16))
    iw = misc_ref[:, IDX_DIM:IDX_DIM + IDX_HEADS]
    key_ref[...] = _indexer_keys(s, iw, qpos, ts, l_pad)
    kq = jnp.minimum(k_sel, qpos + 1)
    _select_bias(key_ref, bias_ref, jmax_ref, kq)

    bias = bias_ref[...]
    row0 = pl.multiple_of(b * ts, ts)
    for g in range(N_KV_HEADS):
        q = q8_ref[g * GROUP:(g + 1) * GROUP].reshape(GROUP * ts, HEAD_DIM).astype(BF16)
        o = _masked_attention(q, kbuf[:, g * HEAD_DIM:(g + 1) * HEAD_DIM].astype(BF16),
                              vbuf[:, g * HEAD_DIM:(g + 1) * HEAD_DIM].astype(BF16), bias, ts)
        for hh in range(GROUP):
            head = g * GROUP + hh
            att_ref[pl.ds(row0, ts), head * HEAD_DIM:(head + 1) * HEAD_DIM] = o[hh * ts:(hh + 1) * ts]

    @pl.when(b == nb - 1)
    def _():
        o_ref[...] = h_ref[...] + _dot(att_ref[...].astype(BF16), wo_ref[...])


def _dsa_sample_call(h, q8, iq4, misc, k_new, v_new, cache_k, cache_v, cache_ik, layer, page_table, wo, ts):
    n, d = h.shape
    nb = n // ts
    n_layers, n_pool, page = cache_k.shape[:3]
    kvw = N_KV_HEADS * HEAD_DIM
    n_pages = page_table.shape[1]
    past = n_pages * page
    l_pad = -(-(past + ts) // LANES) * LANES
    k_sel = min(TOPK_MAX, (past + ts) // 4)
    grid_spec = pltpu.PrefetchScalarGridSpec(
        num_scalar_prefetch=1,
        grid=(nb,),
        in_specs=[
            pl.BlockSpec((n, d), lambda b, pt: (0, 0)),
            pl.BlockSpec((N_HEADS, ts, HEAD_DIM), lambda b, pt: (0, b, 0)),
            pl.BlockSpec((IDX_HEADS, ts, IDX_DIM), lambda b, pt: (0, b, 0)),
            pl.BlockSpec((ts, LANES), lambda b, pt: (b, 0)),
            pl.BlockSpec((ts, kvw), lambda b, pt: (b, 0)),
            pl.BlockSpec((ts, kvw), lambda b, pt: (b, 0)),
            pl.BlockSpec(memory_space=pl.ANY),
            pl.BlockSpec(memory_space=pl.ANY),
            pl.BlockSpec(memory_space=pl.ANY),
            pl.BlockSpec((d, d), lambda b, pt: (0, 0)),
        ],
        out_specs=pl.BlockSpec((n, d), lambda b, pt: (0, 0)),
        scratch_shapes=[
            pltpu.VMEM((l_pad, kvw), F32),
            pltpu.VMEM((l_pad, kvw), F32),
            pltpu.VMEM((l_pad, IDX_DIM), F32),
            pltpu.VMEM((ts, l_pad), I32),
            pltpu.VMEM((ts, l_pad), F32),
            pltpu.VMEM((ts, LANES), I32),
            pltpu.VMEM((n, d), F32),
            pltpu.SemaphoreType.DMA((3,)),
        ],
    )
    return pl.pallas_call(
        functools.partial(_dsa_sample_kernel, k_sel=k_sel, layer=layer),
        grid_spec=grid_spec,
        out_shape=jax.ShapeDtypeStruct((n, d), F32),
        compiler_params=_params(),
        name="dsa_sample",
    )(page_table, h, q8, iq4, misc, k_new, v_new,
      cache_k.reshape(n_layers, n_pool, page, kvw), cache_v.reshape(n_layers, n_pool, page, kvw), cache_ik, wo)


def _sgu_kernel(h_ref, g_ref, win_ref, bin_ref, gv_ref, ws_ref, bs_ref, wout_ref, o_ref, *rest,
                seg, emit_v):
    if emit_v:
        v_ref, gated_ref = rest
    else:
        (gated_ref,) = rest
    tm, d = h_ref.shape
    c_len = ws_ref.shape[1]
    d_sg = gv_ref.shape[1]
    gw = d_sg // SG_GROUPS
    x = h_ref[...]
    xn = _rms(x, g_ref[...]).astype(BF16)
    z = _dot(xn, win_ref[...]) + bin_ref[...]
    z = 0.5 * z * (1.0 + lax.erf(z * (0.5 ** 0.5)))
    u = z[:, :d_sg]
    v = _rms(z[:, d_sg:], gv_ref[...])
    if emit_v:
        v_ref[...] = v
    vb = v.astype(BF16)
    row = lax.broadcasted_iota(I32, (c_len, c_len), 0)
    col = lax.broadcasted_iota(I32, (c_len, c_len), 1)
    same_seq = (row >> _log2(seg)) == (col >> _log2(seg))
    for g in range(SG_GROUPS):
        wg = jnp.where(col <= row, jnp.where(same_seq, ws_ref[g], 0.0), 0.0).astype(BF16)
        bg = bs_ref[:, g:g + 1]
        for ch in range(tm // c_len):
            rows = slice(ch * c_len, (ch + 1) * c_len)
            cols = slice(g * gw, (g + 1) * gw)
            mixed = _dot(wg, vb[rows, cols]) + bg
            gated_ref[rows, cols] = (u[rows, cols] * mixed).astype(BF16)
    o_ref[...] = x + _dot(gated_ref[...], wout_ref[...])


def _sgu_call(h, g, win, b_in, gv, ws, bs_t, wout, tm, seg, emit_v):
    n, d = h.shape
    d2 = win.shape[1]
    d_sg = d2 // 2
    c_len = ws.shape[1]
    out_shape = [jax.ShapeDtypeStruct((n, d), F32)]
    out_specs = [pl.BlockSpec((tm, d), lambda i: (i, 0))]
    if emit_v:
        out_shape.append(jax.ShapeDtypeStruct((n, d_sg), F32))
        out_specs.append(pl.BlockSpec((tm, d_sg), lambda i: (i, 0)))
    res = pl.pallas_call(
        functools.partial(_sgu_kernel, seg=seg, emit_v=emit_v),
        grid=(n // tm,),
        in_specs=[
            pl.BlockSpec((tm, d), lambda i: (i, 0)),
            pl.BlockSpec((1, d), lambda i: (0, 0)),
            pl.BlockSpec((d, d2), lambda i: (0, 0)),
            pl.BlockSpec((1, d2), lambda i: (0, 0)),
            pl.BlockSpec((1, d_sg), lambda i: (0, 0)),
            pl.BlockSpec((SG_GROUPS, c_len, c_len), lambda i: (0, 0, 0)),
            pl.BlockSpec((c_len, SG_GROUPS), lambda i: (0, 0)),
            pl.BlockSpec((d_sg, d), lambda i: (0, 0)),
        ],
        out_specs=tuple(out_specs),
        out_shape=tuple(out_shape),
        scratch_shapes=[pltpu.VMEM((tm, d_sg), BF16)],
        compiler_params=_params(),
        name="sgu",
    )(h, g.reshape(1, d), win, b_in.reshape(1, d2), gv.reshape(1, d_sg), ws, bs_t, wout)
    return res


FFN_CHUNK = 256


def _ffn_body(x, p_ref, gf_ref, wup_ref, cw_ref, cb_ref, wdn_ref, gp_ref, wple_ref, wgate_ref, gfin_ref,
              o_ref, act_ref, shifted, emit_up, final_norm):
    d_ff = wdn_ref.shape[0]
    xn = _rms(x, gf_ref[...]).astype(BF16)

    def conv(cols):
        up = _dot(xn, wup_ref[:, cols])
        m1, m2 = shifted(up, cols)
        emit_up(up, cols)
        return cb_ref[:, cols] + cw_ref[0:1, cols] * m2 + cw_ref[1:2, cols] * m1 + cw_ref[2:3, cols] * up

    for c in range(d_ff // FFN_CHUNK):
        gate = conv(slice(c * FFN_CHUNK, (c + 1) * FFN_CHUNK))
        val = conv(slice(d_ff + c * FFN_CHUNK, d_ff + (c + 1) * FFN_CHUNK))
        act_ref[:, c * FFN_CHUNK:(c + 1) * FFN_CHUNK] = (gate * _sigmoid(gate) * val).astype(BF16)
    h2 = x + _dot(act_ref[...], wdn_ref[...])
    gate = _sigmoid(_dot(_rms(h2, gp_ref[...]).astype(BF16), wgate_ref[...]))
    h3 = h2 + _dot(p_ref[...].astype(BF16), wple_ref[...]) * gate
    o_ref[...] = _rms(h3, gfin_ref[...]) if final_norm else h3


def _ffn_prompt_kernel(h_ref, p_ref, gf_ref, wup_ref, cw_ref, cb_ref, wdn_ref, gp_ref, wple_ref, wgate_ref,
                       gfin_ref, o_ref, tail_ref, carry_ref, act_ref, *, tiles_per_seq, final_norm):
    tm = h_ref.shape[0]
    i = pl.program_id(0)

    @pl.when(i % tiles_per_seq == 0)
    def _():
        carry_ref[...] = jnp.zeros(carry_ref.shape, F32)

    row = lax.broadcasted_iota(I32, (tm, FFN_CHUNK), 0)

    def shifted(up, cols):
        prev = carry_ref[:, cols]
        p1 = prev[SUBLANES - 1:SUBLANES]
        p2 = prev[SUBLANES - 2:SUBLANES - 1]
        m1 = jnp.where(row >= 1, pltpu.roll(up, 1, 0), p1)
        m2 = jnp.where(row >= 2, pltpu.roll(up, 2, 0), jnp.where(row == 0, p2, p1))
        return m1, m2

    def emit_up(up, cols):
        last = up[tm - SUBLANES:tm]
        carry_ref[:, cols] = last
        tail_ref[0, :, cols] = last

    _ffn_body(h_ref[...], p_ref, gf_ref, wup_ref, cw_ref, cb_ref, wdn_ref, gp_ref, wple_ref, wgate_ref, gfin_ref,
              o_ref, act_ref, shifted, emit_up, final_norm)


def _ffn_sample_kernel(h_ref, p_ref, pm1_ref, pm2_ref, gf_ref, wup_ref, cw_ref, cb_ref, wdn_ref, gp_ref, wple_ref,
                       wgate_ref, gfin_ref, o_ref, up_ref, act_ref, *, seg, final_norm):
    tm = h_ref.shape[0]
    row = lax.broadcasted_iota(I32, (tm, FFN_CHUNK), 0) & ((1 << _log2(seg)) - 1)

    def shifted(up, cols):
        m1 = jnp.where(row >= 1, pltpu.roll(up, 1, 0), pm1_ref[:, cols])
        m2 = jnp.where(row >= 2, pltpu.roll(up, 2, 0), pm2_ref[:, cols])
        return m1, m2

    def emit_up(up, cols):
        up_ref[:, cols] = up

    _ffn_body(h_ref[...], p_ref, gf_ref, wup_ref, cw_ref, cb_ref, wdn_ref, gp_ref, wple_ref, wgate_ref, gfin_ref,
              o_ref, act_ref, shifted, emit_up, final_norm)


def _ffn_weight_specs(d, f2, d_ff, ple):
    full = lambda shape: pl.BlockSpec(shape, lambda i: (0,) * len(shape))
    return [full((1, d)), full((d, f2)), full((CONV_W, f2)), full((1, f2)), full((d_ff, d)),
            full((1, d)), full((ple, d)), full((d, d)), full((1, d))]


def _ffn_prompt_call(h, p_all, layer, weights, tm, tiles_per_seq, final_norm):
    n, d = h.shape
    gf, wup, cw, cb, wdn, gp, wple, wgate, gfin = weights
    f2 = wup.shape[1]
    d_ff = wdn.shape[0]
    ple = p_all.shape[2]
    nt = n // tm
    return pl.pallas_call(
        functools.partial(_ffn_prompt_kernel, tiles_per_seq=tiles_per_seq, final_norm=final_norm),
        grid=(nt,),
        in_specs=[pl.BlockSpec((tm, d), lambda i: (i, 0)), pl.BlockSpec((None, tm, ple), lambda i: (layer, i, 0))]
        + _ffn_weight_specs(d, f2, d_ff, ple),
        out_specs=(pl.BlockSpec((tm, d), lambda i: (i, 0)), pl.BlockSpec((1, SUBLANES, f2), lambda i: (i, 0, 0))),
        out_shape=(jax.ShapeDtypeStruct((n, d), F32), jax.ShapeDtypeStruct((nt, SUBLANES, f2), F32)),
        scratch_shapes=[pltpu.VMEM((SUBLANES, f2), F32), pltpu.VMEM((tm, d_ff), BF16)],
        compiler_params=_params(),
        name="ffn_prompt",
    )(h, p_all, gf, wup, cw, cb, wdn, gp, wple, wgate, gfin)


def _ffn_sample_call(h, p, pm1, pm2, weights, seg, final_norm):
    n, d = h.shape
    gf, wup, cw, cb, wdn, gp, wple, wgate, gfin = weights
    f2 = wup.shape[1]
    d_ff = wdn.shape[0]
    ple = p.shape[1]
    full = lambda shape: pl.BlockSpec(shape, lambda i: (0,) * len(shape))
    return pl.pallas_call(
        functools.partial(_ffn_sample_kernel, seg=seg, final_norm=final_norm),
        grid=(1,),
        in_specs=[full((n, d)), full((n, ple)), full((n, f2)), full((n, f2))] + _ffn_weight_specs(d, f2, d_ff, ple),
        out_specs=(full((n, d)), full((n, f2))),
        out_shape=(jax.ShapeDtypeStruct((n, d), F32), jax.ShapeDtypeStruct((n, f2), F32)),
        scratch_shapes=[pltpu.VMEM((n, d_ff), BF16)],
        compiler_params=_params(),
        name="ffn_sample",
    )(h, p, pm1, pm2, gf, wup, cw, cb, wdn, gp, wple, wgate, gfin)


PROMPT_TILE = 512
ATTN_Q_TILE = 128


def kernel(x_prompt, x_sample, cache_k, cache_v, cache_idx_k, state_conv, page_table, p_prompt, p_sample,
           norm_mix, w_attn_in, w_attn_out, w_sg_in, b_sg_in, norm_sg_v, w_sg_spatial, b_sg_spatial, w_sg_out,
           norm_ffn, w_ffn_up, w_ffn_conv, b_ffn_conv, w_ffn_down, norm_ple, w_ple, w_ple_gate, norm_final):
    batch, seq, d = x_prompt.shape
    nb, ts, _ = x_sample.shape
    depth = norm_mix.shape[0]
    page = cache_k.shape[2]
    past = page_table.shape[1] * page
    f2 = w_ffn_up.shape[2]
    n_s = nb * ts
    tm = PROMPT_TILE

    hp = x_prompt.reshape(batch * seq, d)
    hs = x_sample.reshape(n_s, d)
    tabs_p = _rope_tables(jnp.arange(seq))
    tabs_s = tuple(jnp.tile(t, (nb, 1)) for t in _rope_tables(past + jnp.arange(ts)))

    kp_l, vp_l, ikp_l, ks_l, vs_l, iks_l, cp_l, cs_l, sgv_l = [], [], [], [], [], [], [], [], []
    for i in range(depth):
        j = i // 2
        if i % 2 == 0:
            w_in = jnp.pad(w_attn_in[j], ((0, 0), (0, PROJ_W - w_attn_in.shape[2]))).astype(BF16)
            w_out = w_attn_out[j].astype(BF16)
            q8, k, v, kb, vb, iq4, misc, ikb = _proj_call(hp, norm_mix[i], w_in, tabs_p, tm, BF16)
            hp = _dsa_prompt_call(hp, q8, iq4, misc, ikb, kb, vb, w_out, batch, seq, ATTN_Q_TILE)
            kp_l.append(k.reshape(batch, seq, N_KV_HEADS, HEAD_DIM))
            vp_l.append(v.reshape(batch, seq, N_KV_HEADS, HEAD_DIM))
            ikp_l.append(misc[:, :IDX_DIM].reshape(batch, seq, IDX_DIM))
            q8, k, v, _, _, iq4, misc, _ = _proj_call(hs, norm_mix[i], w_in, tabs_s, n_s, F32)
            hs = _dsa_sample_call(hs, q8, iq4, misc, k, v, cache_k, cache_v, cache_idx_k, j,
                                  page_table, w_out, ts)
            ks_l.append(k.reshape(nb, ts, N_KV_HEADS, HEAD_DIM))
            vs_l.append(v.reshape(nb, ts, N_KV_HEADS, HEAD_DIM))
            iks_l.append(misc[:, :IDX_DIM].reshape(nb, ts, IDX_DIM))
        else:
            win = w_sg_in[j].astype(BF16)
            wout = w_sg_out[j].astype(BF16)
            (hp,) = _sgu_call(hp, norm_mix[i], win, b_sg_in[j], norm_sg_v[j], w_sg_spatial[j],
                              b_sg_spatial[j].T, wout, tm, SG_CHUNK, False)
            ws_s = jnp.tile(w_sg_spatial[j][:, :ts, :ts], (1, nb, nb))
            bs_s = jnp.tile(b_sg_spatial[j].T[:ts], (nb, 1))
            hs, v_rows = _sgu_call(hs, norm_mix[i], win, b_sg_in[j], norm_sg_v[j], ws_s, bs_s, wout, n_s, ts, True)
            sgv_l.append(v_rows.reshape(nb, ts, -1))

        final = i == depth - 1
        weights = (norm_ffn[i].reshape(1, d), w_ffn_up[i].astype(BF16), w_ffn_conv[i], b_ffn_conv[i].reshape(1, f2),
                   w_ffn_down[i].astype(BF16), norm_ple[i].reshape(1, d), w_ple[i].astype(BF16),
                   w_ple_gate[i].astype(BF16), norm_final.reshape(1, d))
        hp, tail = _ffn_prompt_call(hp, p_prompt.reshape(depth, batch * seq, -1), i, weights, tm, seq // tm, final)
        cp_l.append(tail.reshape(batch, seq // tm, SUBLANES, f2)[:, -1, SUBLANES - (CONV_W - 1):])
        st = state_conv[i]
        zeros = jnp.zeros((nb, ts - 1, f2), F32)
        pm1 = jnp.concatenate([st[:, 1:2], zeros], axis=1).reshape(n_s, f2)
        pm2 = jnp.concatenate([st, zeros[:, 1:]], axis=1).reshape(n_s, f2)
        hs, up_s = _ffn_sample_call(hs, p_sample[i].reshape(n_s, -1), pm1, pm2, weights, ts, final)
        cs_l.append(up_s.reshape(nb, ts, f2)[:, ts - (CONV_W - 1):])

    return (hp.reshape(batch, seq, d), hs.reshape(nb, ts, d),
            jnp.stack(kp_l), jnp.stack(vp_l), jnp.stack(ikp_l),
            jnp.stack(ks_l), jnp.stack(vs_l), jnp.stack(iks_l),
            jnp.stack(cp_l), jnp.stack(cs_l), jnp.stack(sgv_l))
```

```python
import functools

import jax
import jax.numpy as jnp
from jax import lax
from jax.experimental import pallas as pl
from jax.experimental.pallas import tpu as pltpu

F32 = jnp.float32
BF16 = jnp.bfloat16
I32 = jnp.int32

N_HEADS = 8
N_KV_HEADS = 2
GROUP = N_HEADS // N_KV_HEADS
HEAD_DIM = 128
IDX_HEADS = 4
IDX_DIM = 64
IDX_W_SCALE = (IDX_HEADS * IDX_DIM) ** -0.5
TOPK_MAX = 256
SG_CHUNK = 128
SG_GROUPS = 8
CONV_W = 3
ROPE_THETA = 10000.0
EPS = 1e-6

LANES = 128
SUBLANES = 8
V7X_VMEM_BYTES = 64 * 1024 * 1024
VMEM_LIMIT_BYTES = V7X_VMEM_BYTES - 8 * 1024 * 1024

NEG_INF = float("-inf")
INT_MIN = -(2 ** 31)


def _params(grid_rank=1):
    return pltpu.CompilerParams(dimension_semantics=("arbitrary",) * grid_rank,
                                vmem_limit_bytes=VMEM_LIMIT_BYTES)


def _log2(n):
    assert n > 0 and n & (n - 1) == 0, n
    return n.bit_length() - 1


def _rms(x, g):
    ms = jnp.mean(x * x, axis=-1, keepdims=True)
    return x * lax.rsqrt(ms + EPS) * g


def _sigmoid(x):
    return 1.0 / (1.0 + jnp.exp(-x))


def _dot(a, b):
    return jnp.dot(a, b, preferred_element_type=F32)


def _dot_nt(a, b):
    return lax.dot_general(a, b, (((1,), (1,)), ((), ())), preferred_element_type=F32)


def _sortable_key(score):
    b = pltpu.bitcast(score, I32)
    b = jnp.where(b == INT_MIN, 0, b)
    return b ^ ((b >> 31) & 0x7FFFFFFF)


Q_OFF = 0
K_OFF = N_HEADS * HEAD_DIM
V_OFF = K_OFF + N_KV_HEADS * HEAD_DIM
IQ_OFF = V_OFF + N_KV_HEADS * HEAD_DIM
MISC_OFF = IQ_OFF + IDX_HEADS * IDX_DIM
PROJ_W = MISC_OFF + LANES
KV_W = N_KV_HEADS * HEAD_DIM
ATTN_TILE = 256


def _project(h_ref, g_ref, w_ref, tab_refs, q8_ref, k_ref, iq4_ref, misc_ref):
    c128_ref, s128_ref, c64_ref, s64_ref, cm_ref, sm_ref = tab_refs
    tm = h_ref.shape[0]
    xn = _rms(h_ref[...], g_ref[...]).astype(BF16)
    y = _dot(xn, w_ref[...])
    c128 = c128_ref[...]
    s128 = s128_ref[...]

    def rope128(t):
        return t * c128 + pltpu.roll(t, HEAD_DIM // 2, 1) * s128

    for h in range(N_HEADS):
        q8_ref[h] = rope128(y[:, Q_OFF + h * HEAD_DIM:Q_OFF + (h + 1) * HEAD_DIM]).astype(q8_ref.dtype)
    ks = []
    for h in range(N_KV_HEADS):
        kh = rope128(y[:, K_OFF + h * HEAD_DIM:K_OFF + (h + 1) * HEAD_DIM])
        k_ref[:, h * HEAD_DIM:(h + 1) * HEAD_DIM] = kh
        ks.append(kh)
    v = y[:, V_OFF:IQ_OFF]

    lane = lax.broadcasted_iota(I32, (tm, LANES), 1)
    low_half = (lane & (IDX_DIM - 1)) < (IDX_DIM // 2)

    def rope64(t, c, s):
        partner = jnp.where(low_half, pltpu.roll(t, LANES - IDX_DIM // 2, 1), pltpu.roll(t, IDX_DIM // 2, 1))
        return t * c + partner * s

    c64 = c64_ref[...]
    s64 = s64_ref[...]
    for pair in range(IDX_HEADS // 2):
        t = rope64(y[:, IQ_OFF + pair * LANES:IQ_OFF + (pair + 1) * LANES], c64, s64)
        iq4_ref[2 * pair] = t[:, :IDX_DIM].astype(iq4_ref.dtype)
        iq4_ref[2 * pair + 1] = t[:, IDX_DIM:].astype(iq4_ref.dtype)
    m = rope64(y[:, MISC_OFF:MISC_OFF + LANES], cm_ref[...], sm_ref[...])
    misc_ref[...] = m
    return ks, v, m


def _proj_prompt_kernel(h_ref, g_ref, w_ref, c128_ref, s128_ref, c64_ref, s64_ref, cm_ref, sm_ref,
                        q8_ref, k_ref, v_ref, kb_ref, vt_ref, iq4_ref, misc_ref, misct_ref, ikb_ref):
    tm = h_ref.shape[0]
    ks, v, m = _project(h_ref, g_ref, w_ref, (c128_ref, s128_ref, c64_ref, s64_ref, cm_ref, sm_ref),
                        q8_ref, k_ref, iq4_ref, misc_ref)
    for h in range(N_KV_HEADS):
        kb_ref[:, h * HEAD_DIM:(h + 1) * HEAD_DIM] = ks[h].astype(BF16)
    v_ref[...] = v
    for c in range(tm // ATTN_TILE):
        vt_ref[c] = v[c * ATTN_TILE:(c + 1) * ATTN_TILE].T.astype(BF16)
    misct_ref[...] = m.T
    ikb_ref[...] = m[:, :IDX_DIM].astype(BF16)


def _proj_sample_kernel(h_ref, g_ref, w_ref, c128_ref, s128_ref, c64_ref, s64_ref, cm_ref, sm_ref,
                        q8_ref, k_ref, v_ref, iq4_ref, misc_ref):
    _, v, _ = _project(h_ref, g_ref, w_ref, (c128_ref, s128_ref, c64_ref, s64_ref, cm_ref, sm_ref),
                       q8_ref, k_ref, iq4_ref, misc_ref)
    v_ref[...] = v


def _proj_call(h, g, w, tabs, tm, prompt):
    n, d = h.shape
    t_tab = tabs[0].shape[0]
    n_tab = t_tab // tm
    tab_spec = pl.BlockSpec((tm, LANES), lambda i: (i % n_tab, 0))
    row = lambda w_: pl.BlockSpec((tm, w_), lambda i: (i, 0))
    qdtype = BF16 if prompt else F32
    q8 = (jax.ShapeDtypeStruct((N_HEADS, n, HEAD_DIM), qdtype), pl.BlockSpec((N_HEADS, tm, HEAD_DIM), lambda i: (0, i, 0)))
    iq4 = (jax.ShapeDtypeStruct((IDX_HEADS, n, IDX_DIM), qdtype), pl.BlockSpec((IDX_HEADS, tm, IDX_DIM), lambda i: (0, i, 0)))
    kf = (jax.ShapeDtypeStruct((n, KV_W), F32), row(KV_W))
    misc = (jax.ShapeDtypeStruct((n, LANES), F32), row(LANES))
    if prompt:
        assert tm % ATTN_TILE == 0
        outs = [q8, kf, kf, (jax.ShapeDtypeStruct((n, KV_W), BF16), row(KV_W)),
                (jax.ShapeDtypeStruct((n // ATTN_TILE, KV_W, ATTN_TILE), BF16),
                 pl.BlockSpec((tm // ATTN_TILE, KV_W, ATTN_TILE), lambda i: (i, 0, 0))),
                iq4, misc,
                (jax.ShapeDtypeStruct((LANES, n), F32), pl.BlockSpec((LANES, tm), lambda i: (0, i))),
                (jax.ShapeDtypeStruct((n, IDX_DIM), BF16), row(IDX_DIM))]
        body = _proj_prompt_kernel
    else:
        outs = [q8, kf, kf, iq4, misc]
        body = _proj_sample_kernel
    return pl.pallas_call(
        body,
        grid=(n // tm,),
        in_specs=[
            pl.BlockSpec((tm, d), lambda i: (i, 0)),
            pl.BlockSpec((1, d), lambda i: (0, 0)),
            pl.BlockSpec((d, PROJ_W), lambda i: (0, 0)),
        ] + [tab_spec] * 6,
        out_specs=tuple(o[1] for o in outs),
        out_shape=tuple(o[0] for o in outs),
        compiler_params=_params(),
        name="attn_proj_prompt" if prompt else "attn_proj_sample",
    )(h, g.reshape(1, d), w, *tabs)


def _rope_tables(pos):
    t = pos.shape[0]

    def tab(half):
        inv = ROPE_THETA ** (-jnp.arange(half, dtype=F32) / half)
        ang = pos.astype(F32)[:, None] * inv[None, :]
        return jnp.cos(ang), jnp.sin(ang)

    c64, s64 = tab(HEAD_DIM // 2)
    c32, s32 = tab(IDX_DIM // 2)
    c128 = jnp.concatenate([c64, c64], axis=1)
    s128 = jnp.concatenate([-s64, s64], axis=1)
    cq = jnp.concatenate([c32, c32, c32, c32], axis=1)
    sq = jnp.concatenate([-s32, s32, -s32, s32], axis=1)
    cm = jnp.concatenate([c32, c32, jnp.full((t, LANES - IDX_DIM), IDX_W_SCALE, F32)], axis=1)
    sm = jnp.concatenate([-s32, s32, jnp.zeros((t, LANES - IDX_DIM), F32)], axis=1)
    return (c128, s128, cq, sq, cm, sm)


COUNT_SLAB = 64


def _col_count(key_ref, n_rows, indicator):
    w = key_ref.shape[1]
    acc = jnp.zeros((COUNT_SLAB, w), I32)
    for c in range(n_rows // COUNT_SLAB):
        acc = acc + indicator(key_ref[c * COUNT_SLAB:(c + 1) * COUNT_SLAB, :], c * COUNT_SLAB)
    return jnp.sum(acc, axis=0, keepdims=True)


def _select_bias_cols(key_ref, bias_ref, jmax_ref, n_rows, kq):
    w = key_ref.shape[1]

    def ge_body(i, t):
        cand = t + lax.shift_left(jnp.int32(1), 31 - i)
        cnt = _col_count(key_ref, n_rows, lambda k, r0: jnp.where(k >= cand, 1, 0))
        return jnp.where(cnt >= kq, cand, t)

    thr = lax.fori_loop(0, 32, ge_body, jnp.full((1, w), INT_MIN, I32))
    cnt_gt = _col_count(key_ref, n_rows, lambda k, r0: jnp.where(k > thr, 1, 0))
    cnt_ge = _col_count(key_ref, n_rows, lambda k, r0: jnp.where(k >= thr, 1, 0))
    need = kq - cnt_gt

    jmax_ref[...] = jnp.full(jmax_ref.shape, n_rows, I32)
    sub = lax.broadcasted_iota(I32, (COUNT_SLAB, w), 0)

    @pl.when(jnp.max(cnt_ge - kq) > 0)
    def _():
        nbits = max(1, (n_rows - 1).bit_length())

        def lt_body(i, j):
            cand = j + lax.shift_left(jnp.int32(1), nbits - 1 - i)
            cnt = _col_count(key_ref, n_rows,
                             lambda k, r0: jnp.where(k == thr, jnp.where(sub + r0 < cand, 1, 0), 0))
            return jnp.where(cnt < need, cand, j)

        jmax_ref[...] = jnp.broadcast_to(lax.fori_loop(0, nbits, lt_body, jnp.zeros((1, w), I32)), jmax_ref.shape)

    jmax = jmax_ref[0:1, :]
    for c in range(n_rows // COUNT_SLAB):
        rows = slice(c * COUNT_SLAB, (c + 1) * COUNT_SLAB)
        k = key_ref[rows, :]
        keep_tie = jnp.where(sub + c * COUNT_SLAB <= jmax, 0.0, NEG_INF)
        bias_ref[rows, :] = jnp.where(k > thr, 0.0, jnp.where(k == thr, keep_tie, NEG_INF))


def _dsa_prompt_kernel(q8_ref, iq4_ref, misct_ref, ik_ref, k_ref, vt_ref, att_ref,
                       key_ref, bias_ref, logit_ref, acc_ref, jmax_ref, *, k_sel):
    t = q8_ref.shape[1]
    nq = key_ref.shape[0] // t
    qi = pl.program_id(1)
    n_chunks = qi + 1
    qpos = qi * t + lax.broadcasted_iota(I32, (1, t), 1)
    rows_of = lambda c: pl.ds(pl.multiple_of(c * t, t), t)

    iq = iq4_ref[...].reshape(IDX_HEADS * t, IDX_DIM)
    iw = misct_ref[IDX_DIM:IDX_DIM + SUBLANES, :]

    def score_chunk(c, carry):
        s = _dot_nt(ik_ref[rows_of(c), :], iq)
        score = jnp.zeros((t, t), F32)
        for h in range(IDX_HEADS):
            score = score + jnp.maximum(s[:, h * t:(h + 1) * t], 0.0) * iw[h:h + 1, :]
        kpos = c * t + lax.broadcasted_iota(I32, (t, t), 0)
        key_ref[rows_of(c), :] = _sortable_key(jnp.where(kpos <= qpos, score, NEG_INF))
        return carry

    lax.fori_loop(0, n_chunks, score_chunk, 0)

    kq = jnp.minimum(k_sel, qpos + 1)
    for j in range(nq):
        @pl.when(qi == j)
        def _():
            _select_bias_cols(key_ref, bias_ref, jmax_ref, (j + 1) * t, kq)

    for g in range(N_KV_HEADS):
        q = q8_ref[g * GROUP:(g + 1) * GROUP].reshape(GROUP * t, HEAD_DIM)
        cols = slice(g * HEAD_DIM, (g + 1) * HEAD_DIM)

        def logits_chunk(c, m):
            lg = _dot_nt(k_ref[rows_of(c), cols], q) * (HEAD_DIM ** -0.5)
            lg = lg + jnp.concatenate([bias_ref[rows_of(c), :]] * GROUP, axis=1)
            logit_ref[rows_of(c), :] = lg
            return jnp.maximum(m, jnp.max(lg, axis=0, keepdims=True))

        m = lax.fori_loop(0, n_chunks, logits_chunk, jnp.full((1, GROUP * t), NEG_INF, F32))
        acc_ref[...] = jnp.zeros(acc_ref.shape, F32)

        def pv_chunk(c, den):
            e = jnp.exp(logit_ref[rows_of(c), :] - m)
            acc_ref[...] += _dot(vt_ref[c, cols, :], e.astype(BF16))
            return den + jnp.sum(e, axis=0, keepdims=True)

        den = lax.fori_loop(0, n_chunks, pv_chunk, jnp.zeros((1, GROUP * t), F32))
        o = acc_ref[...] / den
        for hh in range(GROUP):
            head = g * GROUP + hh
            att_ref[:, head * HEAD_DIM:(head + 1) * HEAD_DIM] = o[:, hh * t:(hh + 1) * t].T.astype(BF16)


def _dsa_prompt_call(q8, iq4, misct, ikb, kb, vt, batch, seq):
    t = ATTN_TILE
    n = batch * seq
    d = N_HEADS * HEAD_DIM
    nq = seq // t
    k_sel = min(TOPK_MAX, seq // 4)
    return pl.pallas_call(
        functools.partial(_dsa_prompt_kernel, k_sel=k_sel),
        grid=(batch, nq),
        in_specs=[
            pl.BlockSpec((N_HEADS, t, HEAD_DIM), lambda b, q: (0, b * nq + q, 0)),
            pl.BlockSpec((IDX_HEADS, t, IDX_DIM), lambda b, q: (0, b * nq + q, 0)),
            pl.BlockSpec((LANES, t), lambda b, q: (0, b * nq + q)),
            pl.BlockSpec((seq, IDX_DIM), lambda b, q: (b, 0)),
            pl.BlockSpec((seq, KV_W), lambda b, q: (b, 0)),
            pl.BlockSpec((nq, KV_W, t), lambda b, q: (b, 0, 0)),
        ],
        out_specs=pl.BlockSpec((t, d), lambda b, q: (b * nq + q, 0)),
        out_shape=jax.ShapeDtypeStruct((n, d), BF16),
        scratch_shapes=[
            pltpu.VMEM((seq, t), I32),
            pltpu.VMEM((seq, t), F32),
            pltpu.VMEM((seq, GROUP * t), F32),
            pltpu.VMEM((HEAD_DIM, GROUP * t), F32),
            pltpu.VMEM((SUBLANES, t), I32),
        ],
        compiler_params=_params(grid_rank=2),
        name="dsa_prompt",
    )(q8, iq4, misct, ikb, kb, vt)


def _row_count(key_ref, indicator):
    r, l = key_ref.shape
    acc = jnp.zeros((r, LANES), I32)
    for c in range(l // LANES):
        acc = acc + indicator(key_ref[:, c * LANES:(c + 1) * LANES], c)
    return jnp.sum(acc, axis=-1, keepdims=True)


def _select_bias_rows(key_ref, bias_ref, jmax_ref, kq):
    r, l = key_ref.shape

    def ge_body(i, t):
        cand = t + lax.shift_left(jnp.int32(1), 31 - i)
        candb = jnp.broadcast_to(cand, (r, LANES))
        cnt = _row_count(key_ref, lambda k, c: jnp.where(k >= candb, 1, 0))
        return jnp.where(cnt >= kq, cand, t)

    thr = lax.fori_loop(0, 32, ge_body, jnp.full((r, 1), INT_MIN, I32))
    thrb = jnp.broadcast_to(thr, (r, LANES))
    cnt_gt = _row_count(key_ref, lambda k, c: jnp.where(k > thrb, 1, 0))
    cnt_ge = _row_count(key_ref, lambda k, c: jnp.where(k >= thrb, 1, 0))
    need = kq - cnt_gt

    jmax_ref[...] = jnp.full(jmax_ref.shape, l, I32)
    lane = lax.broadcasted_iota(I32, (r, LANES), 1)

    @pl.when(jnp.max(cnt_ge - kq) > 0)
    def _():
        nbits = max(1, (l - 1).bit_length())

        def lt_body(i, j):
            cand = j + lax.shift_left(jnp.int32(1), nbits - 1 - i)
            candb = jnp.broadcast_to(cand, (r, LANES))
            cnt = _row_count(key_ref, lambda k, c: jnp.where(k == thrb, jnp.where(lane + c * LANES < candb, 1, 0), 0))
            return jnp.where(cnt < need, cand, j)

        jmax_ref[...] = jnp.broadcast_to(lax.fori_loop(0, nbits, lt_body, jnp.zeros((r, 1), I32)), jmax_ref.shape)

    jmaxb = jmax_ref[...]
    for c in range(l // LANES):
        k = key_ref[:, c * LANES:(c + 1) * LANES]
        keep_tie = jnp.where(lane + c * LANES <= jmaxb, 0.0, NEG_INF)
        bias_ref[:, c * LANES:(c + 1) * LANES] = jnp.where(k > thrb, 0.0, jnp.where(k == thrb, keep_tie, NEG_INF))


def _dsa_sample_kernel(pt_ref, h_ref, q8_ref, iq4_ref, misc_ref, kn_ref, vn_ref,
                       ck_ref, cv_ref, cikt_ref, wo_ref, o_ref,
                       kbuf, vbuf, iktbuf, knew, vnew, iknew, key_ref, bias_ref, jmax_ref, att_ref, sems,
                       *, k_sel, layer):
    ts = kn_ref.shape[0]
    page = cikt_ref.shape[3]
    page_rows = ck_ref.shape[2]
    n_pages = pt_ref.shape[1]
    past = n_pages * page
    b = pl.program_id(0)
    nb = pl.num_programs(0)

    def page_copies(p):
        src = pt_ref[b, p]
        rows = pl.ds(pl.multiple_of(p * page_rows, page_rows), page_rows)
        lanes = pl.ds(pl.multiple_of(p * page, page), page)
        return (pltpu.make_async_copy(ck_ref.at[layer, src], kbuf.at[rows], sems.at[0]),
                pltpu.make_async_copy(cv_ref.at[layer, src], vbuf.at[rows], sems.at[1]),
                pltpu.make_async_copy(cikt_ref.at[layer, src], iktbuf.at[:, lanes], sems.at[2]))

    def start_page(p, carry):
        for cp in page_copies(p):
            cp.start()
        return carry

    def wait_page(p, carry):
        for cp in page_copies(p):
            cp.wait()
        return carry

    lax.fori_loop(0, n_pages, start_page, 0)

    pad = jnp.zeros((LANES - ts, KV_W), F32)
    knew[0:ts, :] = kn_ref[...]
    knew[ts:, :] = pad
    vnew[0:ts, :] = vn_ref[...]
    vnew[ts:, :] = pad
    iknew[0:ts, :] = misc_ref[:, :IDX_DIM]
    iknew[ts:, :] = jnp.zeros((LANES - ts, IDX_DIM), F32)

    lax.fori_loop(0, n_pages, wait_page, 0)

    qpos = past + lax.broadcasted_iota(I32, (ts, 1), 0)
    l_pad = past + LANES
    iq = iq4_ref[...].reshape(IDX_HEADS * ts, IDX_DIM).astype(BF16)
    s = jnp.concatenate([_dot(iq, iktbuf[...].astype(BF16)), _dot_nt(iq, iknew[...].astype(BF16))], axis=1)
    iw = misc_ref[:, IDX_DIM:IDX_DIM + IDX_HEADS]
    score = jnp.zeros((ts, l_pad), F32)
    for h in range(IDX_HEADS):
        score = score + jnp.maximum(s[h * ts:(h + 1) * ts], 0.0) * iw[:, h:h + 1]
    kpos = lax.broadcasted_iota(I32, (ts, l_pad), 1)
    key_ref[...] = _sortable_key(jnp.where(kpos <= qpos, score, NEG_INF))
    kq = jnp.minimum(k_sel, qpos + 1)
    _select_bias_rows(key_ref, bias_ref, jmax_ref, kq)

    bias = jnp.concatenate([bias_ref[...]] * GROUP, axis=0)
    row0 = pl.multiple_of(b * ts, ts)
    for g in range(N_KV_HEADS):
        cols = slice(g * HEAD_DIM, (g + 1) * HEAD_DIM)
        k_past = kbuf[pl.ds(g, past, stride=N_KV_HEADS), :].astype(BF16)
        v_past = vbuf[pl.ds(g, past, stride=N_KV_HEADS), :].astype(BF16)
        q = q8_ref[g * GROUP:(g + 1) * GROUP].reshape(GROUP * ts, HEAD_DIM).astype(BF16)
        logits = jnp.concatenate([_dot_nt(q, k_past), _dot_nt(q, knew[:, cols].astype(BF16))], axis=1)
        logits = logits * (HEAD_DIM ** -0.5) + bias
        m = jnp.max(logits, axis=-1, keepdims=True)
        e = jnp.exp(logits - m)
        den = jnp.sum(e, axis=-1, keepdims=True)
        eb = e.astype(BF16)
        o = (_dot(eb[:, :past], v_past) + _dot(eb[:, past:], vnew[:, cols].astype(BF16))) / den
        for hh in range(GROUP):
            head = g * GROUP + hh
            att_ref[pl.ds(row0, ts), head * HEAD_DIM:(head + 1) * HEAD_DIM] = o[hh * ts:(hh + 1) * ts]

    @pl.when(b == nb - 1)
    def _():
        o_ref[...] = h_ref[...] + _dot(att_ref[...].astype(BF16), wo_ref[...])


def _dsa_sample_call(h, q8, iq4, misc, k_new, v_new, cache_k, cache_v, cache_ik, layer, page_table, wo, ts):
    n, d = h.shape
    nb = n // ts
    n_layers, n_pool, page = cache_k.shape[:3]
    n_pages = page_table.shape[1]
    past = n_pages * page
    l_pad = past + LANES
    k_sel = min(TOPK_MAX, (past + ts) // 4)
    ck = cache_k.reshape(n_layers, n_pool, page * N_KV_HEADS, HEAD_DIM)
    cv = cache_v.reshape(n_layers, n_pool, page * N_KV_HEADS, HEAD_DIM)
    cikt = jnp.swapaxes(cache_ik, 2, 3)
    grid_spec = pltpu.PrefetchScalarGridSpec(
        num_scalar_prefetch=1,
        grid=(nb,),
        in_specs=[
            pl.BlockSpec((n, d), lambda b, pt: (0, 0)),
            pl.BlockSpec((N_HEADS, ts, HEAD_DIM), lambda b, pt: (0, b, 0)),
            pl.BlockSpec((IDX_HEADS, ts, IDX_DIM), lambda b, pt: (0, b, 0)),
            pl.BlockSpec((ts, LANES), lambda b, pt: (b, 0)),
            pl.BlockSpec((ts, KV_W), lambda b, pt: (b, 0)),
            pl.BlockSpec((ts, KV_W), lambda b, pt: (b, 0)),
            pl.BlockSpec(memory_space=pl.ANY),
            pl.BlockSpec(memory_space=pl.ANY),
            pl.BlockSpec(memory_space=pl.ANY),
            pl.BlockSpec((d, d), lambda b, pt: (0, 0)),
        ],
        out_specs=pl.BlockSpec((n, d), lambda b, pt: (0, 0)),
        scratch_shapes=[
            pltpu.VMEM((past * N_KV_HEADS, HEAD_DIM), F32),
            pltpu.VMEM((past * N_KV_HEADS, HEAD_DIM), F32),
            pltpu.VMEM((IDX_DIM, past), F32),
            pltpu.VMEM((LANES, KV_W), F32),
            pltpu.VMEM((LANES, KV_W), F32),
            pltpu.VMEM((LANES, IDX_DIM), F32),
            pltpu.VMEM((ts, l_pad), I32),
            pltpu.VMEM((ts, l_pad), F32),
            pltpu.VMEM((ts, LANES), I32),
            pltpu.VMEM((n, d), F32),
            pltpu.SemaphoreType.DMA((3,)),
        ],
    )
    return pl.pallas_call(
        functools.partial(_dsa_sample_kernel, k_sel=k_sel, layer=layer),
        grid_spec=grid_spec,
        out_shape=jax.ShapeDtypeStruct((n, d), F32),
        compiler_params=_params(),
        name="dsa_sample",
    )(page_table, h, q8, iq4, misc, k_new, v_new, ck, cv, cikt, wo)


def _sgu_kernel(h_ref, g_ref, win_ref, bin_ref, gv_ref, ws_ref, bs_ref, wout_ref, o_ref, *rest,
                seg, emit_v):
    if emit_v:
        v_ref, gated_ref = rest
    else:
        (gated_ref,) = rest
    tm, d = h_ref.shape
    c_len = ws_ref.shape[1]
    d_sg = gv_ref.shape[1]
    gw = d_sg // SG_GROUPS
    x = h_ref[...]
    xn = _rms(x, g_ref[...]).astype(BF16)
    z = _dot(xn, win_ref[...]) + bin_ref[...]
    z = 0.5 * z * (1.0 + lax.erf(z * (0.5 ** 0.5)))
    u = z[:, :d_sg]
    v = _rms(z[:, d_sg:], gv_ref[...])
    if emit_v:
        v_ref[...] = v
    vb = v.astype(BF16)
    row = lax.broadcasted_iota(I32, (c_len, c_len), 0)
    col = lax.broadcasted_iota(I32, (c_len, c_len), 1)
    same_seq = (row >> _log2(seg)) == (col >> _log2(seg))
    for g in range(SG_GROUPS):
        wg = jnp.where(col <= row, jnp.where(same_seq, ws_ref[g], 0.0), 0.0).astype(BF16)
        bg = bs_ref[:, g:g + 1]
        for ch in range(tm // c_len):
            rows = slice(ch * c_len, (ch + 1) * c_len)
            cols = slice(g * gw, (g + 1) * gw)
            mixed = _dot(wg, vb[rows, cols]) + bg
            gated_ref[rows, cols] = (u[rows, cols] * mixed).astype(BF16)
    o_ref[...] = x + _dot(gated_ref[...], wout_ref[...])


def _sgu_call(h, g, win, b_in, gv, ws, bs_t, wout, tm, seg, emit_v):
    n, d = h.shape
    d2 = win.shape[1]
    d_sg = d2 // 2
    c_len = ws.shape[1]
    out_shape = [jax.ShapeDtypeStruct((n, d), F32)]
    out_specs = [pl.BlockSpec((tm, d), lambda i: (i, 0))]
    if emit_v:
        out_shape.append(jax.ShapeDtypeStruct((n, d_sg), F32))
        out_specs.append(pl.BlockSpec((tm, d_sg), lambda i: (i, 0)))
    res = pl.pallas_call(
        functools.partial(_sgu_kernel, seg=seg, emit_v=emit_v),
        grid=(n // tm,),
        in_specs=[
            pl.BlockSpec((tm, d), lambda i: (i, 0)),
            pl.BlockSpec((1, d), lambda i: (0, 0)),
            pl.BlockSpec((d, d2), lambda i: (0, 0)),
            pl.BlockSpec((1, d2), lambda i: (0, 0)),
            pl.BlockSpec((1, d_sg), lambda i: (0, 0)),
            pl.BlockSpec((SG_GROUPS, c_len, c_len), lambda i: (0, 0, 0)),
            pl.BlockSpec((c_len, SG_GROUPS), lambda i: (0, 0)),
            pl.BlockSpec((d_sg, d), lambda i: (0, 0)),
        ],
        out_specs=tuple(out_specs),
        out_shape=tuple(out_shape),
        scratch_shapes=[pltpu.VMEM((tm, d_sg), BF16)],
        compiler_params=_params(),
        name="sgu",
    )(h, g.reshape(1, d), win, b_in.reshape(1, d2), gv.reshape(1, d_sg), ws, bs_t, wout)
    return res


FFN_CHUNK = 256


def _ffn_body(x, p_ref, gf_ref, wup_ref, cw_ref, cb_ref, wdn_ref, gp_ref, wple_ref, wgate_ref, gfin_ref,
              o_ref, act_ref, shifted, emit_up, final_norm):
    d_ff = wdn_ref.shape[0]
    xn = _rms(x, gf_ref[...]).astype(BF16)

    def conv(cols):
        up = _dot(xn, wup_ref[:, cols])
        m1, m2 = shifted(up, cols)
        emit_up(up, cols)
        return cb_ref[:, cols] + cw_ref[0:1, cols] * m2 + cw_ref[1:2, cols] * m1 + cw_ref[2:3, cols] * up

    for c in range(d_ff // FFN_CHUNK):
        gate = conv(slice(c * FFN_CHUNK, (c + 1) * FFN_CHUNK))
        val = conv(slice(d_ff + c * FFN_CHUNK, d_ff + (c + 1) * FFN_CHUNK))
        act_ref[:, c * FFN_CHUNK:(c + 1) * FFN_CHUNK] = (gate * _sigmoid(gate) * val).astype(BF16)
    h2 = x + _dot(act_ref[...], wdn_ref[...])
    gate = _sigmoid(_dot(_rms(h2, gp_ref[...]).astype(BF16), wgate_ref[...]))
    h3 = h2 + _dot(p_ref[...].astype(BF16), wple_ref[...]) * gate
    o_ref[...] = _rms(h3, gfin_ref[...]) if final_norm else h3


def _ffn_prompt_kernel(*refs, tiles_per_seq, final_norm, with_attn):
    if with_attn:
        att_ref, wo_ref, *refs = refs
    (h_ref, p_ref, gf_ref, wup_ref, cw_ref, cb_ref, wdn_ref, gp_ref, wple_ref, wgate_ref, gfin_ref,
     o_ref, tail_ref, carry_ref, act_ref) = refs
    tm = h_ref.shape[0]
    i = pl.program_id(0)

    @pl.when(i % tiles_per_seq == 0)
    def _():
        carry_ref[...] = jnp.zeros(carry_ref.shape, F32)

    row = lax.broadcasted_iota(I32, (tm, FFN_CHUNK), 0)

    def shifted(up, cols):
        prev = carry_ref[:, cols]
        p1 = prev[SUBLANES - 1:SUBLANES]
        p2 = prev[SUBLANES - 2:SUBLANES - 1]
        m1 = jnp.where(row >= 1, pltpu.roll(up, 1, 0), p1)
        m2 = jnp.where(row >= 2, pltpu.roll(up, 2, 0), jnp.where(row == 0, p2, p1))
        return m1, m2

    def emit_up(up, cols):
        last = up[tm - SUBLANES:tm]
        carry_ref[:, cols] = last
        tail_ref[0, :, cols] = last

    x = h_ref[...]
    if with_attn:
        x = x + _dot(att_ref[...], wo_ref[...])
    _ffn_body(x, p_ref, gf_ref, wup_ref, cw_ref, cb_ref, wdn_ref, gp_ref, wple_ref, wgate_ref, gfin_ref,
              o_ref, act_ref, shifted, emit_up, final_norm)


def _ffn_sample_kernel(h_ref, p_ref, pm1_ref, pm2_ref, gf_ref, wup_ref, cw_ref, cb_ref, wdn_ref, gp_ref, wple_ref,
                       wgate_ref, gfin_ref, o_ref, up_ref, act_ref, *, seg, final_norm):
    tm = h_ref.shape[0]
    row = lax.broadcasted_iota(I32, (tm, FFN_CHUNK), 0) & ((1 << _log2(seg)) - 1)

    def shifted(up, cols):
        m1 = jnp.where(row >= 1, pltpu.roll(up, 1, 0), pm1_ref[:, cols])
        m2 = jnp.where(row >= 2, pltpu.roll(up, 2, 0), pm2_ref[:, cols])
        return m1, m2

    def emit_up(up, cols):
        up_ref[:, cols] = up

    _ffn_body(h_ref[...], p_ref, gf_ref, wup_ref, cw_ref, cb_ref, wdn_ref, gp_ref, wple_ref, wgate_ref, gfin_ref,
              o_ref, act_ref, shifted, emit_up, final_norm)


def _ffn_weight_specs(d, f2, d_ff, ple):
    full = lambda shape: pl.BlockSpec(shape, lambda i: (0,) * len(shape))
    return [full((1, d)), full((d, f2)), full((CONV_W, f2)), full((1, f2)), full((d_ff, d)),
            full((1, d)), full((ple, d)), full((d, d)), full((1, d))]


def _ffn_prompt_call(h, att, wo, p_all, layer, weights, tm, tiles_per_seq, final_norm):
    n, d = h.shape
    gf, wup, cw, cb, wdn, gp, wple, wgate, gfin = weights
    f2 = wup.shape[1]
    d_ff = wdn.shape[0]
    ple = p_all.shape[2]
    nt = n // tm
    with_attn = att is not None
    attn_specs = [pl.BlockSpec((tm, d), lambda i: (i, 0)), pl.BlockSpec((d, d), lambda i: (0, 0))] if with_attn else []
    attn_args = (att, wo) if with_attn else ()
    return pl.pallas_call(
        functools.partial(_ffn_prompt_kernel, tiles_per_seq=tiles_per_seq, final_norm=final_norm,
                          with_attn=with_attn),
        grid=(nt,),
        in_specs=attn_specs
        + [pl.BlockSpec((tm, d), lambda i: (i, 0)), pl.BlockSpec((None, tm, ple), lambda i: (layer, i, 0))]
        + _ffn_weight_specs(d, f2, d_ff, ple),
        out_specs=(pl.BlockSpec((tm, d), lambda i: (i, 0)), pl.BlockSpec((1, SUBLANES, f2), lambda i: (i, 0, 0))),
        out_shape=(jax.ShapeDtypeStruct((n, d), F32), jax.ShapeDtypeStruct((nt, SUBLANES, f2), F32)),
        scratch_shapes=[pltpu.VMEM((SUBLANES, f2), F32), pltpu.VMEM((tm, d_ff), BF16)],
        compiler_params=_params(),
        name="ffn_prompt",
    )(*attn_args, h, p_all, gf, wup, cw, cb, wdn, gp, wple, wgate, gfin)


def _ffn_sample_call(h, p, pm1, pm2, weights, seg, final_norm):
    n, d = h.shape
    gf, wup, cw, cb, wdn, gp, wple, wgate, gfin = weights
    f2 = wup.shape[1]
    d_ff = wdn.shape[0]
    ple = p.shape[1]
    full = lambda shape: pl.BlockSpec(shape, lambda i: (0,) * len(shape))
    return pl.pallas_call(
        functools.partial(_ffn_sample_kernel, seg=seg, final_norm=final_norm),
        grid=(1,),
        in_specs=[full((n, d)), full((n, ple)), full((n, f2)), full((n, f2))] + _ffn_weight_specs(d, f2, d_ff, ple),
        out_specs=(full((n, d)), full((n, f2))),
        out_shape=(jax.ShapeDtypeStruct((n, d), F32), jax.ShapeDtypeStruct((n, f2), F32)),
        scratch_shapes=[pltpu.VMEM((n, d_ff), BF16)],
        compiler_params=_params(),
        name="ffn_sample",
    )(h, p, pm1, pm2, gf, wup, cw, cb, wdn, gp, wple, wgate, gfin)


PROMPT_TILE = 512


def kernel(x_prompt, x_sample, cache_k, cache_v, cache_idx_k, state_conv, page_table, p_prompt, p_sample,
           norm_mix, w_attn_in, w_attn_out, w_sg_in, b_sg_in, norm_sg_v, w_sg_spatial, b_sg_spatial, w_sg_out,
           norm_ffn, w_ffn_up, w_ffn_conv, b_ffn_conv, w_ffn_down, norm_ple, w_ple, w_ple_gate, norm_final):
    batch, seq, d = x_prompt.shape
    nb, ts, _ = x_sample.shape
    depth = norm_mix.shape[0]
    page = cache_k.shape[2]
    past = page_table.shape[1] * page
    f2 = w_ffn_up.shape[2]
    n_s = nb * ts
    tm = PROMPT_TILE

    hp = x_prompt.reshape(batch * seq, d)
    hs = x_sample.reshape(n_s, d)
    tabs_p = _rope_tables(jnp.arange(seq))
    tabs_s = tuple(jnp.tile(t, (nb, 1)) for t in _rope_tables(past + jnp.arange(ts)))

    kp_l, vp_l, ikp_l, ks_l, vs_l, iks_l, cp_l, cs_l, sgv_l = [], [], [], [], [], [], [], [], []
    for i in range(depth):
        j = i // 2
        att = w_out = None
        if i % 2 == 0:
            w_in = jnp.pad(w_attn_in[j], ((0, 0), (0, PROJ_W - w_attn_in.shape[2]))).astype(BF16)
            w_out = w_attn_out[j].astype(BF16)
            q8, k, v, kb, vt, iq4, misc, misct, ikb = _proj_call(hp, norm_mix[i], w_in, tabs_p, tm, True)
            att = _dsa_prompt_call(q8, iq4, misct, ikb, kb, vt, batch, seq)
            kp_l.append(k.reshape(batch, seq, N_KV_HEADS, HEAD_DIM))
            vp_l.append(v.reshape(batch, seq, N_KV_HEADS, HEAD_DIM))
            ikp_l.append(misc[:, :IDX_DIM].reshape(batch, seq, IDX_DIM))
            q8, k, v, iq4, misc = _proj_call(hs, norm_mix[i], w_in, tabs_s, n_s, False)
            hs = _dsa_sample_call(hs, q8, iq4, misc, k, v, cache_k, cache_v, cache_idx_k, j,
                                  page_table, w_out, ts)
            ks_l.append(k.reshape(nb, ts, N_KV_HEADS, HEAD_DIM))
            vs_l.append(v.reshape(nb, ts, N_KV_HEADS, HEAD_DIM))
            iks_l.append(misc[:, :IDX_DIM].reshape(nb, ts, IDX_DIM))
        else:
            win = w_sg_in[j].astype(BF16)
            wout = w_sg_out[j].astype(BF16)
            (hp,) = _sgu_call(hp, norm_mix[i], win, b_sg_in[j], norm_sg_v[j], w_sg_spatial[j],
                              b_sg_spatial[j].T, wout, tm, SG_CHUNK, False)
            ws_s = jnp.tile(w_sg_spatial[j][:, :ts, :ts], (1, nb, nb))
            bs_s = jnp.tile(b_sg_spatial[j].T[:ts], (nb, 1))
            hs, v_rows = _sgu_call(hs, norm_mix[i], win, b_sg_in[j], norm_sg_v[j], ws_s, bs_s, wout, n_s, ts, True)
            sgv_l.append(v_rows.reshape(nb, ts, -1))

        final = i == depth - 1
        weights = (norm_ffn[i].reshape(1, d), w_ffn_up[i].astype(BF16), w_ffn_conv[i], b_ffn_conv[i].reshape(1, f2),
                   w_ffn_down[i].astype(BF16), norm_ple[i].reshape(1, d), w_ple[i].astype(BF16),
                   w_ple_gate[i].astype(BF16), norm_final.reshape(1, d))
        hp, tail = _ffn_prompt_call(hp, att, w_out, p_prompt.reshape(depth, batch * seq, -1), i, weights,
                                    tm, seq // tm, final)
        cp_l.append(tail.reshape(batch, seq // tm, SUBLANES, f2)[:, -1, SUBLANES - (CONV_W - 1):])
        st = state_conv[i]
        zeros = jnp.zeros((nb, ts - 1, f2), F32)
        pm1 = jnp.concatenate([st[:, 1:2], zeros], axis=1).reshape(n_s, f2)
        pm2 = jnp.concatenate([st, zeros[:, 1:]], axis=1).reshape(n_s, f2)
        hs, up_s = _ffn_sample_call(hs, p_sample[i].reshape(n_s, -1), pm1, pm2, weights, ts, final)
        cs_l.append(up_s.reshape(nb, ts, f2)[:, ts - (CONV_W - 1):])

    return (hp.reshape(batch, seq, d), hs.reshape(nb, ts, d),
            jnp.stack(kp_l), jnp.stack(vp_l), jnp.stack(ikp_l),
            jnp.stack(ks_l), jnp.stack(vs_l), jnp.stack(iks_l),
            jnp.stack(cp_l), jnp.stack(cs_l), jnp.stack(sgv_l))
```

```python
import functools

import jax
import jax.numpy as jnp
from jax import lax
from jax.experimental import pallas as pl
from jax.experimental.pallas import tpu as pltpu

F32 = jnp.float32
BF16 = jnp.bfloat16
I32 = jnp.int32

N_HEADS = 8
N_KV_HEADS = 2
GROUP = N_HEADS // N_KV_HEADS
HEAD_DIM = 128
IDX_HEADS = 4
IDX_DIM = 64
IDX_W_SCALE = (IDX_HEADS * IDX_DIM) ** -0.5
TOPK_MAX = 256
SG_CHUNK = 128
SG_GROUPS = 8
CONV_W = 3
ROPE_THETA = 10000.0
EPS = 1e-6

LANES = 128
SUBLANES = 8
V7X_VMEM_BYTES = 64 * 1024 * 1024
VMEM_LIMIT_BYTES = V7X_VMEM_BYTES - 8 * 1024 * 1024

NEG_INF = float("-inf")
INT_MIN = -(2 ** 31)


def _params(grid_rank=1):
    return pltpu.CompilerParams(dimension_semantics=("arbitrary",) * grid_rank,
                                vmem_limit_bytes=VMEM_LIMIT_BYTES)


def _log2(n):
    assert n > 0 and n & (n - 1) == 0, n
    return n.bit_length() - 1


def _rms(x, g):
    ms = jnp.mean(x * x, axis=-1, keepdims=True)
    return x * lax.rsqrt(ms + EPS) * g


def _sigmoid(x):
    return 1.0 / (1.0 + jnp.exp(-x))


def _dot(a, b):
    return jnp.dot(a, b, preferred_element_type=F32)


def _dot_nt(a, b):
    return lax.dot_general(a, b, (((1,), (1,)), ((), ())), preferred_element_type=F32)


def _sortable_key(score):
    b = pltpu.bitcast(score, I32)
    b = jnp.where(b == INT_MIN, 0, b)
    return b ^ ((b >> 31) & 0x7FFFFFFF)


Q_OFF = 0
K_OFF = N_HEADS * HEAD_DIM
V_OFF = K_OFF + N_KV_HEADS * HEAD_DIM
IQ_OFF = V_OFF + N_KV_HEADS * HEAD_DIM
MISC_OFF = IQ_OFF + IDX_HEADS * IDX_DIM
PROJ_W = MISC_OFF + LANES
KV_W = N_KV_HEADS * HEAD_DIM
ATTN_TILE = 256


def _project(h_ref, g_ref, w_ref, tab_refs, q8_ref, k_ref, iq4_ref, misc_ref):
    c128_ref, s128_ref, c64_ref, s64_ref, cm_ref, sm_ref = tab_refs
    tm = h_ref.shape[0]
    xn = _rms(h_ref[...], g_ref[...]).astype(BF16)
    y = _dot(xn, w_ref[...])
    c128 = c128_ref[...]
    s128 = s128_ref[...]

    def rope128(t):
        return t * c128 + pltpu.roll(t, HEAD_DIM // 2, 1) * s128

    for h in range(N_HEADS):
        q8_ref[h] = rope128(y[:, Q_OFF + h * HEAD_DIM:Q_OFF + (h + 1) * HEAD_DIM]).astype(q8_ref.dtype)
    ks = []
    for h in range(N_KV_HEADS):
        kh = rope128(y[:, K_OFF + h * HEAD_DIM:K_OFF + (h + 1) * HEAD_DIM])
        k_ref[:, h * HEAD_DIM:(h + 1) * HEAD_DIM] = kh
        ks.append(kh)
    v = y[:, V_OFF:IQ_OFF]

    lane = lax.broadcasted_iota(I32, (tm, LANES), 1)
    low_half = (lane & (IDX_DIM - 1)) < (IDX_DIM // 2)

    def rope64(t, c, s):
        partner = jnp.where(low_half, pltpu.roll(t, LANES - IDX_DIM // 2, 1), pltpu.roll(t, IDX_DIM // 2, 1))
        return t * c + partner * s

    c64 = c64_ref[...]
    s64 = s64_ref[...]
    for pair in range(IDX_HEADS // 2):
        t = rope64(y[:, IQ_OFF + pair * LANES:IQ_OFF + (pair + 1) * LANES], c64, s64)
        iq4_ref[2 * pair] = t[:, :IDX_DIM].astype(iq4_ref.dtype)
        iq4_ref[2 * pair + 1] = t[:, IDX_DIM:].astype(iq4_ref.dtype)
    m = rope64(y[:, MISC_OFF:MISC_OFF + LANES], cm_ref[...], sm_ref[...])
    misc_ref[...] = m
    return ks, v, m


def _proj_prompt_kernel(h_ref, g_ref, w_ref, c128_ref, s128_ref, c64_ref, s64_ref, cm_ref, sm_ref,
                        q8_ref, k_ref, v_ref, kb_ref, vt_ref, iq4_ref, misc_ref, misct_ref, ikb_ref):
    tm = h_ref.shape[0]
    ks, v, m = _project(h_ref, g_ref, w_ref, (c128_ref, s128_ref, c64_ref, s64_ref, cm_ref, sm_ref),
                        q8_ref, k_ref, iq4_ref, misc_ref)
    for h in range(N_KV_HEADS):
        kb_ref[:, h * HEAD_DIM:(h + 1) * HEAD_DIM] = ks[h].astype(BF16)
    v_ref[...] = v
    for c in range(tm // ATTN_TILE):
        vt_ref[c] = v[c * ATTN_TILE:(c + 1) * ATTN_TILE].T.astype(BF16)
    misct_ref[...] = m.T
    ikb_ref[...] = m[:, :IDX_DIM].astype(BF16)


def _proj_sample_kernel(h_ref, g_ref, w_ref, c128_ref, s128_ref, c64_ref, s64_ref, cm_ref, sm_ref,
                        q8_ref, k_ref, v_ref, iq4_ref, misc_ref):
    _, v, _ = _project(h_ref, g_ref, w_ref, (c128_ref, s128_ref, c64_ref, s64_ref, cm_ref, sm_ref),
                       q8_ref, k_ref, iq4_ref, misc_ref)
    v_ref[...] = v


def _proj_call(h, g, w, tabs, tm, prompt):
    n, d = h.shape
    t_tab = tabs[0].shape[0]
    n_tab = t_tab // tm
    tab_spec = pl.BlockSpec((tm, LANES), lambda i: (i % n_tab, 0))
    row = lambda w_: pl.BlockSpec((tm, w_), lambda i: (i, 0))
    qdtype = BF16 if prompt else F32
    q8 = (jax.ShapeDtypeStruct((N_HEADS, n, HEAD_DIM), qdtype), pl.BlockSpec((N_HEADS, tm, HEAD_DIM), lambda i: (0, i, 0)))
    iq4 = (jax.ShapeDtypeStruct((IDX_HEADS, n, IDX_DIM), qdtype), pl.BlockSpec((IDX_HEADS, tm, IDX_DIM), lambda i: (0, i, 0)))
    kf = (jax.ShapeDtypeStruct((n, KV_W), F32), row(KV_W))
    misc = (jax.ShapeDtypeStruct((n, LANES), F32), row(LANES))
    if prompt:
        assert tm % ATTN_TILE == 0
        outs = [q8, kf, kf, (jax.ShapeDtypeStruct((n, KV_W), BF16), row(KV_W)),
                (jax.ShapeDtypeStruct((n // ATTN_TILE, KV_W, ATTN_TILE), BF16),
                 pl.BlockSpec((tm // ATTN_TILE, KV_W, ATTN_TILE), lambda i: (i, 0, 0))),
                iq4, misc,
                (jax.ShapeDtypeStruct((LANES, n), F32), pl.BlockSpec((LANES, tm), lambda i: (0, i))),
                (jax.ShapeDtypeStruct((n, IDX_DIM), BF16), row(IDX_DIM))]
        body = _proj_prompt_kernel
    else:
        outs = [q8, kf, kf, iq4, misc]
        body = _proj_sample_kernel
    return pl.pallas_call(
        body,
        grid=(n // tm,),
        in_specs=[
            pl.BlockSpec((tm, d), lambda i: (i, 0)),
            pl.BlockSpec((1, d), lambda i: (0, 0)),
            pl.BlockSpec((d, PROJ_W), lambda i: (0, 0)),
        ] + [tab_spec] * 6,
        out_specs=tuple(o[1] for o in outs),
        out_shape=tuple(o[0] for o in outs),
        compiler_params=_params(),
        name="attn_proj_prompt" if prompt else "attn_proj_sample",
    )(h, g.reshape(1, d), w, *tabs)


def _rope_tables(pos):
    t = pos.shape[0]

    def tab(half):
        inv = ROPE_THETA ** (-jnp.arange(half, dtype=F32) / half)
        ang = pos.astype(F32)[:, None] * inv[None, :]
        return jnp.cos(ang), jnp.sin(ang)

    c64, s64 = tab(HEAD_DIM // 2)
    c32, s32 = tab(IDX_DIM // 2)
    c128 = jnp.concatenate([c64, c64], axis=1)
    s128 = jnp.concatenate([-s64, s64], axis=1)
    cq = jnp.concatenate([c32, c32, c32, c32], axis=1)
    sq = jnp.concatenate([-s32, s32, -s32, s32], axis=1)
    cm = jnp.concatenate([c32, c32, jnp.full((t, LANES - IDX_DIM), IDX_W_SCALE, F32)], axis=1)
    sm = jnp.concatenate([-s32, s32, jnp.zeros((t, LANES - IDX_DIM), F32)], axis=1)
    return (c128, s128, cq, sq, cm, sm)


COUNT_SLAB = 64


def _col_count(key_ref, n_rows, indicator):
    w = key_ref.shape[1]
    acc = jnp.zeros((COUNT_SLAB, w), I32)
    for c in range(n_rows // COUNT_SLAB):
        acc = acc + indicator(key_ref[c * COUNT_SLAB:(c + 1) * COUNT_SLAB, :], c * COUNT_SLAB)
    return jnp.sum(acc, axis=0, keepdims=True)


def _select_bias_cols(key_ref, bias_ref, jmax_ref, n_rows, kq):
    w = key_ref.shape[1]

    def ge_body(i, t):
        cand = t + lax.shift_left(jnp.int32(1), 31 - i)
        cnt = _col_count(key_ref, n_rows, lambda k, r0: jnp.where(k >= cand, 1, 0))
        return jnp.where(cnt >= kq, cand, t)

    thr = lax.fori_loop(0, 32, ge_body, jnp.full((1, w), INT_MIN, I32))
    cnt_gt = _col_count(key_ref, n_rows, lambda k, r0: jnp.where(k > thr, 1, 0))
    cnt_ge = _col_count(key_ref, n_rows, lambda k, r0: jnp.where(k >= thr, 1, 0))
    need = kq - cnt_gt

    jmax_ref[...] = jnp.full(jmax_ref.shape, n_rows, I32)
    sub = lax.broadcasted_iota(I32, (COUNT_SLAB, w), 0)

    @pl.when(jnp.max(cnt_ge - kq) > 0)
    def _():
        nbits = max(1, (n_rows - 1).bit_length())

        def lt_body(i, j):
            cand = j + lax.shift_left(jnp.int32(1), nbits - 1 - i)
            cnt = _col_count(key_ref, n_rows,
                             lambda k, r0: jnp.where(k == thr, jnp.where(sub + r0 < cand, 1, 0), 0))
            return jnp.where(cnt < need, cand, j)

        jmax_ref[...] = jnp.broadcast_to(lax.fori_loop(0, nbits, lt_body, jnp.zeros((1, w), I32)), jmax_ref.shape)

    jmax = jmax_ref[0:1, :]
    for c in range(n_rows // COUNT_SLAB):
        rows = slice(c * COUNT_SLAB, (c + 1) * COUNT_SLAB)
        k = key_ref[rows, :]
        keep_tie = jnp.where(sub + c * COUNT_SLAB <= jmax, 0.0, NEG_INF)
        bias_ref[rows, :] = jnp.where(k > thr, 0.0, jnp.where(k == thr, keep_tie, NEG_INF))


def _for_chunks(n, body):
    def pair(p, carry):
        body([2 * p, 2 * p + 1])
        return carry

    lax.fori_loop(0, lax.shift_right_logical(n, 1), pair, 0)

    @pl.when((n & 1) == 1)
    def _():
        body([n - 1])


def _dsa_prompt_kernel(q8_ref, iq4_ref, misct_ref, ik_ref, k_ref, vt_ref, att_ref,
                       key_ref, bias_ref, logit_ref, acc_ref, stat_ref, jmax_ref, *, k_sel):
    t = q8_ref.shape[1]
    nq = key_ref.shape[0] // t
    gw = GROUP * t
    qi = pl.program_id(1)
    n_chunks = qi + 1
    qpos = qi * t + lax.broadcasted_iota(I32, (1, t), 1)
    rows_of = lambda c: pl.ds(pl.multiple_of(c * t, t), t)

    iq = iq4_ref[...].reshape(IDX_HEADS * t, IDX_DIM)
    iw = misct_ref[IDX_DIM:IDX_DIM + SUBLANES, :]

    def score_chunks(cs):
        for c in cs:
            s = _dot_nt(ik_ref[rows_of(c), :], iq)
            score = jnp.zeros((t, t), F32)
            for h in range(IDX_HEADS):
                score = score + jnp.maximum(s[:, h * t:(h + 1) * t], 0.0) * iw[h:h + 1, :]
            kpos = c * t + lax.broadcasted_iota(I32, (t, t), 0)
            key_ref[rows_of(c), :] = _sortable_key(jnp.where(kpos <= qpos, score, NEG_INF))

    _for_chunks(n_chunks, score_chunks)

    kq = jnp.minimum(k_sel, qpos + 1)
    for j in range(nq):
        @pl.when(qi == j)
        def _():
            _select_bias_cols(key_ref, bias_ref, jmax_ref, (j + 1) * t, kq)

    qs = [q8_ref[g * GROUP:(g + 1) * GROUP].reshape(gw, HEAD_DIM) for g in range(N_KV_HEADS)]
    head_cols = lambda g: slice(g * HEAD_DIM, (g + 1) * HEAD_DIM)
    lanes_of = lambda g: slice(g * gw, (g + 1) * gw)
    stat_ref[0:1, :] = jnp.full((1, N_KV_HEADS * gw), NEG_INF, F32)
    stat_ref[1:2, :] = jnp.zeros((1, N_KV_HEADS * gw), F32)
    acc_ref[...] = jnp.zeros(acc_ref.shape, F32)

    def logits_chunks(cs):
        for g in range(N_KV_HEADS):
            m = stat_ref[0:1, lanes_of(g)]
            for c in cs:
                lg = _dot_nt(k_ref[rows_of(c), head_cols(g)], qs[g]) * (HEAD_DIM ** -0.5)
                lg = lg + jnp.concatenate([bias_ref[rows_of(c), :]] * GROUP, axis=1)
                logit_ref[rows_of(c), lanes_of(g)] = lg
                m = jnp.maximum(m, jnp.max(lg, axis=0, keepdims=True))
            stat_ref[0:1, lanes_of(g)] = m

    _for_chunks(n_chunks, logits_chunks)

    def pv_chunks(cs):
        for g in range(N_KV_HEADS):
            m = stat_ref[0:1, lanes_of(g)]
            den = stat_ref[1:2, lanes_of(g)]
            for c in cs:
                e = jnp.exp(logit_ref[rows_of(c), lanes_of(g)] - m)
                acc_ref[g] += _dot(vt_ref[c, head_cols(g), :], e.astype(BF16))
                den = den + jnp.sum(e, axis=0, keepdims=True)
            stat_ref[1:2, lanes_of(g)] = den

    _for_chunks(n_chunks, pv_chunks)

    for g in range(N_KV_HEADS):
        o = acc_ref[g] / stat_ref[1:2, lanes_of(g)]
        for hh in range(GROUP):
            head = g * GROUP + hh
            att_ref[:, head * HEAD_DIM:(head + 1) * HEAD_DIM] = o[:, hh * t:(hh + 1) * t].T.astype(BF16)


def _dsa_prompt_call(q8, iq4, misct, ikb, kb, vt, batch, seq):
    t = ATTN_TILE
    n = batch * seq
    d = N_HEADS * HEAD_DIM
    nq = seq // t
    k_sel = min(TOPK_MAX, seq // 4)
    return pl.pallas_call(
        functools.partial(_dsa_prompt_kernel, k_sel=k_sel),
        grid=(batch, nq),
        in_specs=[
            pl.BlockSpec((N_HEADS, t, HEAD_DIM), lambda b, q: (0, b * nq + q, 0)),
            pl.BlockSpec((IDX_HEADS, t, IDX_DIM), lambda b, q: (0, b * nq + q, 0)),
            pl.BlockSpec((LANES, t), lambda b, q: (0, b * nq + q)),
            pl.BlockSpec((seq, IDX_DIM), lambda b, q: (b, 0)),
            pl.BlockSpec((seq, KV_W), lambda b, q: (b, 0)),
            pl.BlockSpec((nq, KV_W, t), lambda b, q: (b, 0, 0)),
        ],
        out_specs=pl.BlockSpec((t, d), lambda b, q: (b * nq + q, 0)),
        out_shape=jax.ShapeDtypeStruct((n, d), BF16),
        scratch_shapes=[
            pltpu.VMEM((seq, t), I32),
            pltpu.VMEM((seq, t), F32),
            pltpu.VMEM((seq, N_KV_HEADS * GROUP * t), F32),
            pltpu.VMEM((N_KV_HEADS, HEAD_DIM, GROUP * t), F32),
            pltpu.VMEM((SUBLANES, N_KV_HEADS * GROUP * t), F32),
            pltpu.VMEM((SUBLANES, t), I32),
        ],
        compiler_params=_params(grid_rank=2),
        name="dsa_prompt",
    )(q8, iq4, misct, ikb, kb, vt)


def _row_count(key_ref, indicator):
    r, l = key_ref.shape
    acc = jnp.zeros((r, LANES), I32)
    for c in range(l // LANES):
        acc = acc + indicator(key_ref[:, c * LANES:(c + 1) * LANES], c)
    return jnp.sum(acc, axis=-1, keepdims=True)


def _select_bias_rows(key_ref, bias_ref, jmax_ref, kq):
    r, l = key_ref.shape

    def ge_body(i, t):
        cand = t + lax.shift_left(jnp.int32(1), 31 - i)
        candb = jnp.broadcast_to(cand, (r, LANES))
        cnt = _row_count(key_ref, lambda k, c: jnp.where(k >= candb, 1, 0))
        return jnp.where(cnt >= kq, cand, t)

    thr = lax.fori_loop(0, 32, ge_body, jnp.full((r, 1), INT_MIN, I32))
    thrb = jnp.broadcast_to(thr, (r, LANES))
    cnt_gt = _row_count(key_ref, lambda k, c: jnp.where(k > thrb, 1, 0))
    cnt_ge = _row_count(key_ref, lambda k, c: jnp.where(k >= thrb, 1, 0))
    need = kq - cnt_gt

    jmax_ref[...] = jnp.full(jmax_ref.shape, l, I32)
    lane = lax.broadcasted_iota(I32, (r, LANES), 1)

    @pl.when(jnp.max(cnt_ge - kq) > 0)
    def _():
        nbits = max(1, (l - 1).bit_length())

        def lt_body(i, j):
            cand = j + lax.shift_left(jnp.int32(1), nbits - 1 - i)
            candb = jnp.broadcast_to(cand, (r, LANES))
            cnt = _row_count(key_ref, lambda k, c: jnp.where(k == thrb, jnp.where(lane + c * LANES < candb, 1, 0), 0))
            return jnp.where(cnt < need, cand, j)

        jmax_ref[...] = jnp.broadcast_to(lax.fori_loop(0, nbits, lt_body, jnp.zeros((r, 1), I32)), jmax_ref.shape)

    jmaxb = jmax_ref[...]
    for c in range(l // LANES):
        k = key_ref[:, c * LANES:(c + 1) * LANES]
        keep_tie = jnp.where(lane + c * LANES <= jmaxb, 0.0, NEG_INF)
        bias_ref[:, c * LANES:(c + 1) * LANES] = jnp.where(k > thrb, 0.0, jnp.where(k == thrb, keep_tie, NEG_INF))


SELECT_SEQS = 8
ATTEND_SEQS = 2


def _sample_select_kernel(pt_ref, iq4_ref, misc_ref, cikt_ref, bias_ref, iktbuf, iknew, key_ref, jmax_ref, sem,
                          *, k_sel, layer, ts):
    page = cikt_ref.shape[3]
    n_pages = pt_ref.shape[1]
    past = n_pages * page
    l_pad = past + LANES
    seqs = iktbuf.shape[0]
    step = pl.program_id(0)

    def page_copy(i, p):
        lanes = pl.ds(pl.multiple_of(p * page, page), page)
        return pltpu.make_async_copy(cikt_ref.at[layer, pt_ref[step * seqs + i, p]], iktbuf.at[i, :, lanes], sem.at[0])

    for i in range(seqs):
        lax.fori_loop(0, n_pages, lambda p, c, i=i: (page_copy(i, p).start(), c)[1], 0)

    iknew[...] = jnp.zeros(iknew.shape, F32)
    for i in range(seqs):
        iknew[i, 0:ts, :] = misc_ref[i * ts:(i + 1) * ts, :IDX_DIM]

    for i in range(seqs):
        lax.fori_loop(0, n_pages, lambda p, c, i=i: (page_copy(i, p).wait(), c)[1], 0)

    qpos = past + lax.broadcasted_iota(I32, (ts, 1), 0)
    kpos = lax.broadcasted_iota(I32, (ts, l_pad), 1)
    for i in range(seqs):
        rows = slice(i * ts, (i + 1) * ts)
        iq = iq4_ref[:, rows, :].reshape(IDX_HEADS * ts, IDX_DIM).astype(BF16)
        s = jnp.concatenate([_dot(iq, iktbuf[i].astype(BF16)), _dot_nt(iq, iknew[i].astype(BF16))], axis=1)
        iw = misc_ref[rows, IDX_DIM:IDX_DIM + IDX_HEADS]
        score = jnp.zeros((ts, l_pad), F32)
        for h in range(IDX_HEADS):
            score = score + jnp.maximum(s[h * ts:(h + 1) * ts], 0.0) * iw[:, h:h + 1]
        key_ref[rows, :] = _sortable_key(jnp.where(kpos <= qpos, score, NEG_INF))
    kq = jnp.minimum(k_sel, jnp.concatenate([qpos] * seqs, axis=0) + 1)
    _select_bias_rows(key_ref, bias_ref, jmax_ref, kq)


def _sample_attend_kernel(pt_ref, h_ref, q8_ref, kn_ref, vn_ref, bias_ref, ck_ref, cv_ref, wo_ref, o_ref,
                          kbuf, vbuf, knew, vnew, att_ref, sems, *, layer, ts):
    page_rows = ck_ref.shape[2]
    n_pages = pt_ref.shape[1]
    past = n_pages * page_rows // N_KV_HEADS
    seqs = kbuf.shape[0]
    step = pl.program_id(0)
    n_steps = pl.num_programs(0)

    def page_copies(i, p):
        src = pt_ref[step * seqs + i, p]
        rows = pl.ds(pl.multiple_of(p * page_rows, page_rows), page_rows)
        return (pltpu.make_async_copy(ck_ref.at[layer, src], kbuf.at[i, rows], sems.at[0, i]),
                pltpu.make_async_copy(cv_ref.at[layer, src], vbuf.at[i, rows], sems.at[1, i]))

    def start_page(i):
        def body(p, c):
            for cp in page_copies(i, p):
                cp.start()
            return c
        return body

    def wait_page(i):
        def body(p, c):
            for cp in page_copies(i, p):
                cp.wait()
            return c
        return body

    for i in range(seqs):
        lax.fori_loop(0, n_pages, start_page(i), 0)

    pad = jnp.zeros((LANES - ts, KV_W), F32)
    for i in range(seqs):
        rows = slice(i * ts, (i + 1) * ts)
        lax.fori_loop(0, n_pages, wait_page(i), 0)
        knew[0:ts, :] = kn_ref[rows, :]
        knew[ts:, :] = pad
        vnew[0:ts, :] = vn_ref[rows, :]
        vnew[ts:, :] = pad
        bias = jnp.concatenate([bias_ref[rows, :]] * GROUP, axis=0)
        row0 = pl.multiple_of((step * seqs + i) * ts, ts)
        for g in range(N_KV_HEADS):
            cols = slice(g * HEAD_DIM, (g + 1) * HEAD_DIM)
            k_past = kbuf[i, pl.ds(g, past, stride=N_KV_HEADS), :].astype(BF16)
            v_past = vbuf[i, pl.ds(g, past, stride=N_KV_HEADS), :].astype(BF16)
            q = q8_ref[g * GROUP:(g + 1) * GROUP, rows, :].reshape(GROUP * ts, HEAD_DIM).astype(BF16)
            logits = jnp.concatenate([_dot_nt(q, k_past), _dot_nt(q, knew[:, cols].astype(BF16))], axis=1)
            logits = logits * (HEAD_DIM ** -0.5) + bias
            m = jnp.max(logits, axis=-1, keepdims=True)
            e = jnp.exp(logits - m)
            den = jnp.sum(e, axis=-1, keepdims=True)
            eb = e.astype(BF16)
            o = (_dot(eb[:, :past], v_past) + _dot(eb[:, past:], vnew[:, cols].astype(BF16))) / den
            for hh in range(GROUP):
                head = g * GROUP + hh
                att_ref[pl.ds(row0, ts), head * HEAD_DIM:(head + 1) * HEAD_DIM] = o[hh * ts:(hh + 1) * ts]

    @pl.when(step == n_steps - 1)
    def _():
        o_ref[...] = h_ref[...] + _dot(att_ref[...].astype(BF16), wo_ref[...])


def _dsa_sample_call(h, q8, iq4, misc, k_new, v_new, cache_k, cache_v, cache_ik, layer, page_table, wo, ts):
    n, d = h.shape
    nb = n // ts
    n_layers, n_pool, page = cache_k.shape[:3]
    n_pages = page_table.shape[1]
    past = n_pages * page
    l_pad = past + LANES
    k_sel = min(TOPK_MAX, (past + ts) // 4)
    ck = cache_k.reshape(n_layers, n_pool, page * N_KV_HEADS, HEAD_DIM)
    cv = cache_v.reshape(n_layers, n_pool, page * N_KV_HEADS, HEAD_DIM)
    cikt = jnp.swapaxes(cache_ik, 2, 3)
    sel_rows = SELECT_SEQS * ts
    bias = pl.pallas_call(
        functools.partial(_sample_select_kernel, k_sel=k_sel, layer=layer, ts=ts),
        grid_spec=pltpu.PrefetchScalarGridSpec(
            num_scalar_prefetch=1,
            grid=(nb // SELECT_SEQS,),
            in_specs=[
                pl.BlockSpec((IDX_HEADS, sel_rows, IDX_DIM), lambda s, pt: (0, s, 0)),
                pl.BlockSpec((sel_rows, LANES), lambda s, pt: (s, 0)),
                pl.BlockSpec(memory_space=pl.ANY),
            ],
            out_specs=pl.BlockSpec((sel_rows, l_pad), lambda s, pt: (s, 0)),
            scratch_shapes=[
                pltpu.VMEM((SELECT_SEQS, IDX_DIM, past), F32),
                pltpu.VMEM((SELECT_SEQS, LANES, IDX_DIM), F32),
                pltpu.VMEM((sel_rows, l_pad), I32),
                pltpu.VMEM((sel_rows, LANES), I32),
                pltpu.SemaphoreType.DMA((1,)),
            ],
        ),
        out_shape=jax.ShapeDtypeStruct((n, l_pad), F32),
        compiler_params=_params(),
        name="sample_select",
    )(page_table, iq4, misc, cikt)

    att_rows = ATTEND_SEQS * ts
    return pl.pallas_call(
        functools.partial(_sample_attend_kernel, layer=layer, ts=ts),
        grid_spec=pltpu.PrefetchScalarGridSpec(
            num_scalar_prefetch=1,
            grid=(nb // ATTEND_SEQS,),
            in_specs=[
                pl.BlockSpec((n, d), lambda s, pt: (0, 0)),
                pl.BlockSpec((N_HEADS, att_rows, HEAD_DIM), lambda s, pt: (0, s, 0)),
                pl.BlockSpec((att_rows, KV_W), lambda s, pt: (s, 0)),
                pl.BlockSpec((att_rows, KV_W), lambda s, pt: (s, 0)),
                pl.BlockSpec((att_rows, l_pad), lambda s, pt: (s, 0)),
                pl.BlockSpec(memory_space=pl.ANY),
                pl.BlockSpec(memory_space=pl.ANY),
                pl.BlockSpec((d, d), lambda s, pt: (0, 0)),
            ],
            out_specs=pl.BlockSpec((n, d), lambda s, pt: (0, 0)),
            scratch_shapes=[
                pltpu.VMEM((ATTEND_SEQS, past * N_KV_HEADS, HEAD_DIM), F32),
                pltpu.VMEM((ATTEND_SEQS, past * N_KV_HEADS, HEAD_DIM), F32),
                pltpu.VMEM((LANES, KV_W), F32),
                pltpu.VMEM((LANES, KV_W), F32),
                pltpu.VMEM((n, d), F32),
                pltpu.SemaphoreType.DMA((2, ATTEND_SEQS)),
            ],
        ),
        out_shape=jax.ShapeDtypeStruct((n, d), F32),
        compiler_params=_params(),
        name="sample_attend",
    )(page_table, h, q8, k_new, v_new, bias, ck, cv, wo)


def _sgu_kernel(h_ref, g_ref, win_ref, bin_ref, gv_ref, ws_ref, bs_ref, wout_ref, o_ref, *rest,
                seg, emit_v):
    if emit_v:
        v_ref, gated_ref = rest
    else:
        (gated_ref,) = rest
    tm, d = h_ref.shape
    c_len = ws_ref.shape[1]
    d_sg = gv_ref.shape[1]
    gw = d_sg // SG_GROUPS
    x = h_ref[...]
    xn = _rms(x, g_ref[...]).astype(BF16)
    z = _dot(xn, win_ref[...]) + bin_ref[...]
    z = 0.5 * z * (1.0 + lax.erf(z * (0.5 ** 0.5)))
    u = z[:, :d_sg]
    v = _rms(z[:, d_sg:], gv_ref[...])
    if emit_v:
        v_ref[...] = v
    vb = v.astype(BF16)
    row = lax.broadcasted_iota(I32, (c_len, c_len), 0)
    col = lax.broadcasted_iota(I32, (c_len, c_len), 1)
    same_seq = (row >> _log2(seg)) == (col >> _log2(seg))
    for g in range(SG_GROUPS):
        wg = jnp.where(col <= row, jnp.where(same_seq, ws_ref[g], 0.0), 0.0).astype(BF16)
        bg = bs_ref[:, g:g + 1]
        for ch in range(tm // c_len):
            rows = slice(ch * c_len, (ch + 1) * c_len)
            cols = slice(g * gw, (g + 1) * gw)
            mixed = _dot(wg, vb[rows, cols]) + bg
            gated_ref[rows, cols] = (u[rows, cols] * mixed).astype(BF16)
    o_ref[...] = x + _dot(gated_ref[...], wout_ref[...])


def _sgu_call(h, g, win, b_in, gv, ws, bs_t, wout, tm, seg, emit_v):
    n, d = h.shape
    d2 = win.shape[1]
    d_sg = d2 // 2
    c_len = ws.shape[1]
    out_shape = [jax.ShapeDtypeStruct((n, d), F32)]
    out_specs = [pl.BlockSpec((tm, d), lambda i: (i, 0))]
    if emit_v:
        out_shape.append(jax.ShapeDtypeStruct((n, d_sg), F32))
        out_specs.append(pl.BlockSpec((tm, d_sg), lambda i: (i, 0)))
    res = pl.pallas_call(
        functools.partial(_sgu_kernel, seg=seg, emit_v=emit_v),
        grid=(n // tm,),
        in_specs=[
            pl.BlockSpec((tm, d), lambda i: (i, 0)),
            pl.BlockSpec((1, d), lambda i: (0, 0)),
            pl.BlockSpec((d, d2), lambda i: (0, 0)),
            pl.BlockSpec((1, d2), lambda i: (0, 0)),
            pl.BlockSpec((1, d_sg), lambda i: (0, 0)),
            pl.BlockSpec((SG_GROUPS, c_len, c_len), lambda i: (0, 0, 0)),
            pl.BlockSpec((c_len, SG_GROUPS), lambda i: (0, 0)),
            pl.BlockSpec((d_sg, d), lambda i: (0, 0)),
        ],
        out_specs=tuple(out_specs),
        out_shape=tuple(out_shape),
        scratch_shapes=[pltpu.VMEM((tm, d_sg), BF16)],
        compiler_params=_params(),
        name="sgu",
    )(h, g.reshape(1, d), win, b_in.reshape(1, d2), gv.reshape(1, d_sg), ws, bs_t, wout)
    return res


FFN_CHUNK = 256


def _ffn_body(x, p_ref, gf_ref, wup_ref, cw_ref, cb_ref, wdn_ref, gp_ref, wple_ref, wgate_ref, gfin_ref,
              o_ref, act_ref, shifted, emit_up, final_norm):
    d_ff = wdn_ref.shape[0]
    xn = _rms(x, gf_ref[...]).astype(BF16)

    def conv(cols):
        up = _dot(xn, wup_ref[:, cols])
        m1, m2 = shifted(up, cols)
        emit_up(up, cols)
        return cb_ref[:, cols] + cw_ref[0:1, cols] * m2 + cw_ref[1:2, cols] * m1 + cw_ref[2:3, cols] * up

    for c in range(d_ff // FFN_CHUNK):
        gate = conv(slice(c * FFN_CHUNK, (c + 1) * FFN_CHUNK))
        val = conv(slice(d_ff + c * FFN_CHUNK, d_ff + (c + 1) * FFN_CHUNK))
        act_ref[:, c * FFN_CHUNK:(c + 1) * FFN_CHUNK] = (gate * _sigmoid(gate) * val).astype(BF16)
    h2 = x + _dot(act_ref[...], wdn_ref[...])
    gate = _sigmoid(_dot(_rms(h2, gp_ref[...]).astype(BF16), wgate_ref[...]))
    h3 = h2 + _dot(p_ref[...].astype(BF16), wple_ref[...]) * gate
    o_ref[...] = _rms(h3, gfin_ref[...]) if final_norm else h3


def _ffn_prompt_kernel(*refs, tiles_per_seq, final_norm, with_attn):
    if with_attn:
        att_ref, wo_ref, *refs = refs
    (h_ref, p_ref, gf_ref, wup_ref, cw_ref, cb_ref, wdn_ref, gp_ref, wple_ref, wgate_ref, gfin_ref,
     o_ref, tail_ref, carry_ref, act_ref) = refs
    tm = h_ref.shape[0]
    i = pl.program_id(0)

    @pl.when(i % tiles_per_seq == 0)
    def _():
        carry_ref[...] = jnp.zeros(carry_ref.shape, F32)

    row = lax.broadcasted_iota(I32, (tm, FFN_CHUNK), 0)

    def shifted(up, cols):
        prev = carry_ref[:, cols]
        p1 = prev[SUBLANES - 1:SUBLANES]
        p2 = prev[SUBLANES - 2:SUBLANES - 1]
        m1 = jnp.where(row >= 1, pltpu.roll(up, 1, 0), p1)
        m2 = jnp.where(row >= 2, pltpu.roll(up, 2, 0), jnp.where(row == 0, p2, p1))
        return m1, m2

    def emit_up(up, cols):
        last = up[tm - SUBLANES:tm]
        carry_ref[:, cols] = last
        tail_ref[0, :, cols] = last

    x = h_ref[...]
    if with_attn:
        x = x + _dot(att_ref[...], wo_ref[...])
    _ffn_body(x, p_ref, gf_ref, wup_ref, cw_ref, cb_ref, wdn_ref, gp_ref, wple_ref, wgate_ref, gfin_ref,
              o_ref, act_ref, shifted, emit_up, final_norm)


def _ffn_sample_kernel(h_ref, p_ref, pm1_ref, pm2_ref, gf_ref, wup_ref, cw_ref, cb_ref, wdn_ref, gp_ref, wple_ref,
                       wgate_ref, gfin_ref, o_ref, up_ref, act_ref, *, seg, final_norm):
    tm = h_ref.shape[0]
    row = lax.broadcasted_iota(I32, (tm, FFN_CHUNK), 0) & ((1 << _log2(seg)) - 1)

    def shifted(up, cols):
        m1 = jnp.where(row >= 1, pltpu.roll(up, 1, 0), pm1_ref[:, cols])
        m2 = jnp.where(row >= 2, pltpu.roll(up, 2, 0), pm2_ref[:, cols])
        return m1, m2

    def emit_up(up, cols):
        up_ref[:, cols] = up

    _ffn_body(h_ref[...], p_ref, gf_ref, wup_ref, cw_ref, cb_ref, wdn_ref, gp_ref, wple_ref, wgate_ref, gfin_ref,
              o_ref, act_ref, shifted, emit_up, final_norm)


def _ffn_weight_specs(d, f2, d_ff, ple):
    full = lambda shape: pl.BlockSpec(shape, lambda i: (0,) * len(shape))
    return [full((1, d)), full((d, f2)), full((CONV_W, f2)), full((1, f2)), full((d_ff, d)),
            full((1, d)), full((ple, d)), full((d, d)), full((1, d))]


def _ffn_prompt_call(h, att, wo, p_all, layer, weights, tm, tiles_per_seq, final_norm):
    n, d = h.shape
    gf, wup, cw, cb, wdn, gp, wple, wgate, gfin = weights
    f2 = wup.shape[1]
    d_ff = wdn.shape[0]
    ple = p_all.shape[2]
    nt = n // tm
    with_attn = att is not None
    attn_specs = [pl.BlockSpec((tm, d), lambda i: (i, 0)), pl.BlockSpec((d, d), lambda i: (0, 0))] if with_attn else []
    attn_args = (att, wo) if with_attn else ()
    return pl.pallas_call(
        functools.partial(_ffn_prompt_kernel, tiles_per_seq=tiles_per_seq, final_norm=final_norm,
                          with_attn=with_attn),
        grid=(nt,),
        in_specs=attn_specs
        + [pl.BlockSpec((tm, d), lambda i: (i, 0)), pl.BlockSpec((None, tm, ple), lambda i: (layer, i, 0))]
        + _ffn_weight_specs(d, f2, d_ff, ple),
        out_specs=(pl.BlockSpec((tm, d), lambda i: (i, 0)), pl.BlockSpec((1, SUBLANES, f2), lambda i: (i, 0, 0))),
        out_shape=(jax.ShapeDtypeStruct((n, d), F32), jax.ShapeDtypeStruct((nt, SUBLANES, f2), F32)),
        scratch_shapes=[pltpu.VMEM((SUBLANES, f2), F32), pltpu.VMEM((tm, d_ff), BF16)],
        compiler_params=_params(),
        name="ffn_prompt",
    )(*attn_args, h, p_all, gf, wup, cw, cb, wdn, gp, wple, wgate, gfin)


def _ffn_sample_call(h, p, pm1, pm2, weights, seg, final_norm):
    n, d = h.shape
    gf, wup, cw, cb, wdn, gp, wple, wgate, gfin = weights
    f2 = wup.shape[1]
    d_ff = wdn.shape[0]
    ple = p.shape[1]
    full = lambda shape: pl.BlockSpec(shape, lambda i: (0,) * len(shape))
    return pl.pallas_call(
        functools.partial(_ffn_sample_kernel, seg=seg, final_norm=final_norm),
        grid=(1,),
        in_specs=[full((n, d)), full((n, ple)), full((n, f2)), full((n, f2))] + _ffn_weight_specs(d, f2, d_ff, ple),
        out_specs=(full((n, d)), full((n, f2))),
        out_shape=(jax.ShapeDtypeStruct((n, d), F32), jax.ShapeDtypeStruct((n, f2), F32)),
        scratch_shapes=[pltpu.VMEM((n, d_ff), BF16)],
        compiler_params=_params(),
        name="ffn_sample",
    )(h, p, pm1, pm2, gf, wup, cw, cb, wdn, gp, wple, wgate, gfin)


PROMPT_TILE = 512


def kernel(x_prompt, x_sample, cache_k, cache_v, cache_idx_k, state_conv, page_table, p_prompt, p_sample,
           norm_mix, w_attn_in, w_attn_out, w_sg_in, b_sg_in, norm_sg_v, w_sg_spatial, b_sg_spatial, w_sg_out,
           norm_ffn, w_ffn_up, w_ffn_conv, b_ffn_conv, w_ffn_down, norm_ple, w_ple, w_ple_gate, norm_final):
    batch, seq, d = x_prompt.shape
    nb, ts, _ = x_sample.shape
    depth = norm_mix.shape[0]
    page = cache_k.shape[2]
    past = page_table.shape[1] * page
    f2 = w_ffn_up.shape[2]
    n_s = nb * ts
    tm = PROMPT_TILE

    hp = x_prompt.reshape(batch * seq, d)
    hs = x_sample.reshape(n_s, d)
    tabs_p = _rope_tables(jnp.arange(seq))
    tabs_s = tuple(jnp.tile(t, (nb, 1)) for t in _rope_tables(past + jnp.arange(ts)))

    kp_l, vp_l, ikp_l, ks_l, vs_l, iks_l, cp_l, cs_l, sgv_l = [], [], [], [], [], [], [], [], []
    for i in range(depth):
        j = i // 2
        att = w_out = None
        if i % 2 == 0:
            w_in = jnp.pad(w_attn_in[j], ((0, 0), (0, PROJ_W - w_attn_in.shape[2]))).astype(BF16)
            w_out = w_attn_out[j].astype(BF16)
            q8, k, v, kb, vt, iq4, misc, misct, ikb = _proj_call(hp, norm_mix[i], w_in, tabs_p, tm, True)
            att = _dsa_prompt_call(q8, iq4, misct, ikb, kb, vt, batch, seq)
            kp_l.append(k.reshape(batch, seq, N_KV_HEADS, HEAD_DIM))
            vp_l.append(v.reshape(batch, seq, N_KV_HEADS, HEAD_DIM))
            ikp_l.append(misc[:, :IDX_DIM].reshape(batch, seq, IDX_DIM))
            q8, k, v, iq4, misc = _proj_call(hs, norm_mix[i], w_in, tabs_s, n_s, False)
            hs = _dsa_sample_call(hs, q8, iq4, misc, k, v, cache_k, cache_v, cache_idx_k, j,
                                  page_table, w_out, ts)
            ks_l.append(k.reshape(nb, ts, N_KV_HEADS, HEAD_DIM))
            vs_l.append(v.reshape(nb, ts, N_KV_HEADS, HEAD_DIM))
            iks_l.append(misc[:, :IDX_DIM].reshape(nb, ts, IDX_DIM))
        else:
            win = w_sg_in[j].astype(BF16)
            wout = w_sg_out[j].astype(BF16)
            (hp,) = _sgu_call(hp, norm_mix[i], win, b_sg_in[j], norm_sg_v[j], w_sg_spatial[j],
                              b_sg_spatial[j].T, wout, tm, SG_CHUNK, False)
            ws_s = jnp.tile(w_sg_spatial[j][:, :ts, :ts], (1, nb, nb))
            bs_s = jnp.tile(b_sg_spatial[j].T[:ts], (nb, 1))
            hs, v_rows = _sgu_call(hs, norm_mix[i], win, b_sg_in[j], norm_sg_v[j], ws_s, bs_s, wout, n_s, ts, True)
            sgv_l.append(v_rows.reshape(nb, ts, -1))

        final = i == depth - 1
        weights = (norm_ffn[i].reshape(1, d), w_ffn_up[i].astype(BF16), w_ffn_conv[i], b_ffn_conv[i].reshape(1, f2),
                   w_ffn_down[i].astype(BF16), norm_ple[i].reshape(1, d), w_ple[i].astype(BF16),
                   w_ple_gate[i].astype(BF16), norm_final.reshape(1, d))
        hp, tail = _ffn_prompt_call(hp, att, w_out, p_prompt.reshape(depth, batch * seq, -1), i, weights,
                                    tm, seq // tm, final)
        cp_l.append(tail.reshape(batch, seq // tm, SUBLANES, f2)[:, -1, SUBLANES - (CONV_W - 1):])
        st = state_conv[i]
        zeros = jnp.zeros((nb, ts - 1, f2), F32)
        pm1 = jnp.concatenate([st[:, 1:2], zeros], axis=1).reshape(n_s, f2)
        pm2 = jnp.concatenate([st, zeros[:, 1:]], axis=1).reshape(n_s, f2)
        hs, up_s = _ffn_sample_call(hs, p_sample[i].reshape(n_s, -1), pm1, pm2, weights, ts, final)
        cs_l.append(up_s.reshape(nb, ts, f2)[:, ts - (CONV_W - 1):])

    return (hp.reshape(batch, seq, d), hs.reshape(nb, ts, d),
            jnp.stack(kp_l), jnp.stack(vp_l), jnp.stack(ikp_l),
            jnp.stack(ks_l), jnp.stack(vs_l), jnp.stack(iks_l),
            jnp.stack(cp_l), jnp.stack(cs_l), jnp.stack(sgv_l))
```

```python
import functools

import jax
import jax.numpy as jnp
from jax import lax
from jax.experimental import pallas as pl
from jax.experimental.pallas import tpu as pltpu

F32 = jnp.float32
BF16 = jnp.bfloat16
I32 = jnp.int32

N_HEADS = 8
N_KV_HEADS = 2
GROUP = N_HEADS // N_KV_HEADS
HEAD_DIM = 128
IDX_HEADS = 4
IDX_DIM = 64
IDX_W_SCALE = (IDX_HEADS * IDX_DIM) ** -0.5
TOPK_MAX = 256
SG_CHUNK = 128
SG_GROUPS = 8
CONV_W = 3
ROPE_THETA = 10000.0
EPS = 1e-6

LANES = 128
SUBLANES = 8
V7X_VMEM_BYTES = 64 * 1024 * 1024
VMEM_LIMIT_BYTES = V7X_VMEM_BYTES - 8 * 1024 * 1024

NEG_INF = float("-inf")
INT_MIN = -(2 ** 31)


def _params(grid_rank=1):
    return pltpu.CompilerParams(dimension_semantics=("arbitrary",) * grid_rank,
                                vmem_limit_bytes=VMEM_LIMIT_BYTES)


def _log2(n):
    assert n > 0 and n & (n - 1) == 0, n
    return n.bit_length() - 1


def _rms(x, g):
    ms = jnp.mean(x * x, axis=-1, keepdims=True)
    return x * lax.rsqrt(ms + EPS) * g


def _sigmoid(x):
    return 1.0 / (1.0 + jnp.exp(-x))


def _dot(a, b):
    return jnp.dot(a, b, preferred_element_type=F32)


def _dot_nt(a, b):
    return lax.dot_general(a, b, (((1,), (1,)), ((), ())), preferred_element_type=F32)


def _sortable_key(score):
    b = pltpu.bitcast(score, I32)
    b = jnp.where(b == INT_MIN, 0, b)
    return b ^ ((b >> 31) & 0x7FFFFFFF)


Q_OFF = 0
K_OFF = N_HEADS * HEAD_DIM
V_OFF = K_OFF + N_KV_HEADS * HEAD_DIM
IQ_OFF = V_OFF + N_KV_HEADS * HEAD_DIM
MISC_OFF = IQ_OFF + IDX_HEADS * IDX_DIM
PROJ_W = MISC_OFF + LANES
KV_W = N_KV_HEADS * HEAD_DIM
ATTN_TILE = 256


def _project(h_ref, g_ref, w_ref, tab_refs, q8_ref, k_ref, v_ref, iq4_ref):
    c128_ref, s128_ref, c64_ref, s64_ref, cm_ref, sm_ref = tab_refs
    tm = h_ref.shape[0]
    xn = _rms(h_ref[...], g_ref[...]).astype(BF16)
    y = _dot(xn, w_ref[...])
    c128 = c128_ref[...]
    s128 = s128_ref[...]

    def rope128(t):
        return t * c128 + pltpu.roll(t, HEAD_DIM // 2, 1) * s128

    for h in range(N_HEADS):
        q8_ref[h] = rope128(y[:, Q_OFF + h * HEAD_DIM:Q_OFF + (h + 1) * HEAD_DIM]).astype(q8_ref.dtype)
    ks = []
    for h in range(N_KV_HEADS):
        kh = rope128(y[:, K_OFF + h * HEAD_DIM:K_OFF + (h + 1) * HEAD_DIM])
        k_ref[pl.ds(h, tm, stride=N_KV_HEADS), :] = kh
        v_ref[pl.ds(h, tm, stride=N_KV_HEADS), :] = y[:, V_OFF + h * HEAD_DIM:V_OFF + (h + 1) * HEAD_DIM]
        ks.append(kh)
    v = y[:, V_OFF:IQ_OFF]

    lane = lax.broadcasted_iota(I32, (tm, LANES), 1)
    low_half = (lane & (IDX_DIM - 1)) < (IDX_DIM // 2)

    def rope64(t, c, s):
        partner = jnp.where(low_half, pltpu.roll(t, LANES - IDX_DIM // 2, 1), pltpu.roll(t, IDX_DIM // 2, 1))
        return t * c + partner * s

    c64 = c64_ref[...]
    s64 = s64_ref[...]
    for pair in range(IDX_HEADS // 2):
        t = rope64(y[:, IQ_OFF + pair * LANES:IQ_OFF + (pair + 1) * LANES], c64, s64)
        iq4_ref[2 * pair] = t[:, :IDX_DIM].astype(iq4_ref.dtype)
        iq4_ref[2 * pair + 1] = t[:, IDX_DIM:].astype(iq4_ref.dtype)
    m = rope64(y[:, MISC_OFF:MISC_OFF + LANES], cm_ref[...], sm_ref[...])
    return ks, v, m


def _proj_prompt_kernel(h_ref, g_ref, w_ref, c128_ref, s128_ref, c64_ref, s64_ref, cm_ref, sm_ref,
                        q8_ref, k_ref, v_ref, kb_ref, vt_ref, iq4_ref, ikt_ref, misct_ref, ikb_ref):
    tm = h_ref.shape[0]
    ks, v, m = _project(h_ref, g_ref, w_ref, (c128_ref, s128_ref, c64_ref, s64_ref, cm_ref, sm_ref),
                        q8_ref, k_ref, v_ref, iq4_ref)
    for h in range(N_KV_HEADS):
        kb_ref[:, h * HEAD_DIM:(h + 1) * HEAD_DIM] = ks[h].astype(BF16)
    for c in range(tm // ATTN_TILE):
        vt_ref[c] = v[c * ATTN_TILE:(c + 1) * ATTN_TILE].T.astype(BF16)
    mt = m.T
    misct_ref[...] = mt
    ikt_ref[0] = mt[:IDX_DIM, :]
    ikb_ref[...] = m[:, :IDX_DIM].astype(BF16)


def _proj_sample_kernel(h_ref, g_ref, w_ref, c128_ref, s128_ref, c64_ref, s64_ref, cm_ref, sm_ref,
                        q8_ref, k_ref, v_ref, iq4_ref, misc_ref):
    _, _, m = _project(h_ref, g_ref, w_ref, (c128_ref, s128_ref, c64_ref, s64_ref, cm_ref, sm_ref),
                       q8_ref, k_ref, v_ref, iq4_ref)
    misc_ref[...] = m


def _proj_call(h, g, w, tabs, tm, prompt):
    n, d = h.shape
    t_tab = tabs[0].shape[0]
    n_tab = t_tab // tm
    tab_spec = pl.BlockSpec((tm, LANES), lambda i: (i % n_tab, 0))
    row = lambda w_: pl.BlockSpec((tm, w_), lambda i: (i, 0))
    qdtype = BF16 if prompt else F32
    q8 = (jax.ShapeDtypeStruct((N_HEADS, n, HEAD_DIM), qdtype), pl.BlockSpec((N_HEADS, tm, HEAD_DIM), lambda i: (0, i, 0)))
    iq4 = (jax.ShapeDtypeStruct((IDX_HEADS, n, IDX_DIM), qdtype), pl.BlockSpec((IDX_HEADS, tm, IDX_DIM), lambda i: (0, i, 0)))
    kf = (jax.ShapeDtypeStruct((n * N_KV_HEADS, HEAD_DIM), F32),
          pl.BlockSpec((tm * N_KV_HEADS, HEAD_DIM), lambda i: (i, 0)))
    if prompt:
        assert tm % ATTN_TILE == 0
        outs = [q8, kf, kf, (jax.ShapeDtypeStruct((n, KV_W), BF16), row(KV_W)),
                (jax.ShapeDtypeStruct((n // ATTN_TILE, KV_W, ATTN_TILE), BF16),
                 pl.BlockSpec((tm // ATTN_TILE, KV_W, ATTN_TILE), lambda i: (i, 0, 0))),
                iq4,
                (jax.ShapeDtypeStruct((n // t_tab, IDX_DIM, t_tab), F32),
                 pl.BlockSpec((1, IDX_DIM, tm), lambda i: (i // n_tab, 0, i % n_tab))),
                (jax.ShapeDtypeStruct((LANES, n), F32), pl.BlockSpec((LANES, tm), lambda i: (0, i))),
                (jax.ShapeDtypeStruct((n, IDX_DIM), BF16), row(IDX_DIM))]
        body = _proj_prompt_kernel
    else:
        outs = [q8, kf, kf, iq4, (jax.ShapeDtypeStruct((n, LANES), F32), row(LANES))]
        body = _proj_sample_kernel
    return pl.pallas_call(
        body,
        grid=(n // tm,),
        in_specs=[
            pl.BlockSpec((tm, d), lambda i: (i, 0)),
            pl.BlockSpec((1, d), lambda i: (0, 0)),
            pl.BlockSpec((d, PROJ_W), lambda i: (0, 0)),
        ] + [tab_spec] * 6,
        out_specs=tuple(o[1] for o in outs),
        out_shape=tuple(o[0] for o in outs),
        compiler_params=_params(),
        name="attn_proj_prompt" if prompt else "attn_proj_sample",
    )(h, g.reshape(1, d), w, *tabs)


def _rope_tables(pos):
    t = pos.shape[0]

    def tab(half):
        inv = ROPE_THETA ** (-jnp.arange(half, dtype=F32) / half)
        ang = pos.astype(F32)[:, None] * inv[None, :]
        return jnp.cos(ang), jnp.sin(ang)

    c64, s64 = tab(HEAD_DIM // 2)
    c32, s32 = tab(IDX_DIM // 2)
    c128 = jnp.concatenate([c64, c64], axis=1)
    s128 = jnp.concatenate([-s64, s64], axis=1)
    cq = jnp.concatenate([c32, c32, c32, c32], axis=1)
    sq = jnp.concatenate([-s32, s32, -s32, s32], axis=1)
    cm = jnp.concatenate([c32, c32, jnp.full((t, LANES - IDX_DIM), IDX_W_SCALE, F32)], axis=1)
    sm = jnp.concatenate([-s32, s32, jnp.zeros((t, LANES - IDX_DIM), F32)], axis=1)
    return (c128, s128, cq, sq, cm, sm)


COUNT_SLAB = 64


def _col_count(key_ref, n_rows, indicator):
    w = key_ref.shape[1]
    acc = jnp.zeros((COUNT_SLAB, w), I32)
    for c in range(n_rows // COUNT_SLAB):
        acc = acc + indicator(key_ref[c * COUNT_SLAB:(c + 1) * COUNT_SLAB, :], c * COUNT_SLAB)
    return jnp.sum(acc, axis=0, keepdims=True)


I16 = jnp.int16
I16_MIN = -(2 ** 15)
HALF_SLAB = 128
PACK_ROWS = 16


def _bcast16(x, rows):
    one = jnp.broadcast_to(x, (PACK_ROWS, x.shape[1])).astype(I16)
    return jnp.concatenate([one] * (rows // PACK_ROWS), axis=0)


def _col_count16(half_ref, n_rows, indicator):
    w = half_ref.shape[1]
    acc = jnp.zeros((HALF_SLAB, w), I16)
    for c in range(n_rows // HALF_SLAB):
        acc = acc + indicator(half_ref[c * HALF_SLAB:(c + 1) * HALF_SLAB, :])
    return jnp.sum(acc.astype(I32), axis=0, keepdims=True)


def _kth_largest16(half_ref, n_rows, kk):
    w = half_ref.shape[1]
    one = jnp.ones((HALF_SLAB, w), I16)
    zero = jnp.zeros((HALF_SLAB, w), I16)

    def body(i, t):
        cand = t + lax.shift_left(jnp.int32(1), 15 - i)
        candb = _bcast16(cand, HALF_SLAB)
        cnt = _col_count16(half_ref, n_rows, lambda k: jnp.where(k >= candb, one, zero))
        return jnp.where(cnt >= kk, cand, t)

    return lax.fori_loop(0, 16, body, jnp.full((1, w), I16_MIN, I32))


def _select_bias_cols(key_ref, hi_ref, lo_ref, bias_ref, jmax_ref, n_rows, kq):
    w = key_ref.shape[1]
    one = jnp.ones((HALF_SLAB, w), I16)
    zero = jnp.zeros((HALF_SLAB, w), I16)

    t_hi = _kth_largest16(hi_ref, n_rows, kq)
    t_hib = _bcast16(t_hi, HALF_SLAB)
    above = _col_count16(hi_ref, n_rows, lambda k: jnp.where(k > t_hib, one, zero))
    lowest = jnp.full((HALF_SLAB, w), I16_MIN, I16)
    for c in range(n_rows // HALF_SLAB):
        rows = slice(c * HALF_SLAB, (c + 1) * HALF_SLAB)
        lo_ref[rows, :] = jnp.where(hi_ref[rows, :] == t_hib, lo_ref[rows, :], lowest)
    t_lo = _kth_largest16(lo_ref, n_rows, kq - above)
    thr = t_hi * 65536 + (t_lo - I16_MIN)

    cnt_gt = _col_count(key_ref, n_rows, lambda k, r0: jnp.where(k > thr, 1, 0))
    cnt_ge = _col_count(key_ref, n_rows, lambda k, r0: jnp.where(k >= thr, 1, 0))
    need = kq - cnt_gt

    jmax_ref[...] = jnp.full(jmax_ref.shape, n_rows, I32)
    sub = lax.broadcasted_iota(I32, (COUNT_SLAB, w), 0)

    @pl.when(jnp.max(cnt_ge - kq) > 0)
    def _():
        nbits = max(1, (n_rows - 1).bit_length())

        def lt_body(i, j):
            cand = j + lax.shift_left(jnp.int32(1), nbits - 1 - i)
            cnt = _col_count(key_ref, n_rows,
                             lambda k, r0: jnp.where(k == thr, jnp.where(sub + r0 < cand, 1, 0), 0))
            return jnp.where(cnt < need, cand, j)

        jmax_ref[...] = jnp.broadcast_to(lax.fori_loop(0, nbits, lt_body, jnp.zeros((1, w), I32)), jmax_ref.shape)

    jmax = jmax_ref[0:1, :]
    for c in range(n_rows // COUNT_SLAB):
        rows = slice(c * COUNT_SLAB, (c + 1) * COUNT_SLAB)
        k = key_ref[rows, :]
        keep_tie = jnp.where(sub + c * COUNT_SLAB <= jmax, 0.0, NEG_INF)
        bias_ref[rows, :] = jnp.where(k > thr, 0.0, jnp.where(k == thr, keep_tie, NEG_INF))


def _for_chunks(n, body):
    def pair(p, carry):
        body([2 * p, 2 * p + 1])
        return carry

    lax.fori_loop(0, lax.shift_right_logical(n, 1), pair, 0)

    @pl.when((n & 1) == 1)
    def _():
        body([n - 1])


def _dsa_prompt_kernel(q8_ref, iq4_ref, misct_ref, ik_ref, k_ref, vt_ref, att_ref,
                       key_ref, hi_ref, lo_ref, bias_ref, logit_ref, acc_ref, stat_ref, jmax_ref, *, k_sel):
    t = q8_ref.shape[1]
    nq = key_ref.shape[0] // t
    gw = GROUP * t
    qi = pl.program_id(1)
    n_chunks = qi + 1
    qpos = qi * t + lax.broadcasted_iota(I32, (1, t), 1)
    rows_of = lambda c: pl.ds(pl.multiple_of(c * t, t), t)

    iq = iq4_ref[...].reshape(IDX_HEADS * t, IDX_DIM)
    iw = misct_ref[IDX_DIM:IDX_DIM + SUBLANES, :]

    def score_chunks(cs):
        for c in cs:
            s = _dot_nt(ik_ref[rows_of(c), :], iq)
            score = jnp.zeros((t, t), F32)
            for h in range(IDX_HEADS):
                score = score + jnp.maximum(s[:, h * t:(h + 1) * t], 0.0) * iw[h:h + 1, :]
            kpos = c * t + lax.broadcasted_iota(I32, (t, t), 0)
            key = _sortable_key(jnp.where(kpos <= qpos, score, NEG_INF))
            key_ref[rows_of(c), :] = key
            hi_ref[rows_of(c), :] = (key >> 16).astype(I16)
            lo_ref[rows_of(c), :] = ((key & 0xFFFF) + I16_MIN).astype(I16)

    _for_chunks(n_chunks, score_chunks)

    kq = jnp.minimum(k_sel, qpos + 1)
    for j in range(nq):
        @pl.when(qi == j)
        def _():
            _select_bias_cols(key_ref, hi_ref, lo_ref, bias_ref, jmax_ref, (j + 1) * t, kq)

    qs = [q8_ref[g * GROUP:(g + 1) * GROUP].reshape(gw, HEAD_DIM) for g in range(N_KV_HEADS)]
    head_cols = lambda g: slice(g * HEAD_DIM, (g + 1) * HEAD_DIM)
    lanes_of = lambda g: slice(g * gw, (g + 1) * gw)
    stat_ref[0:1, :] = jnp.full((1, N_KV_HEADS * gw), NEG_INF, F32)
    stat_ref[1:2, :] = jnp.zeros((1, N_KV_HEADS * gw), F32)
    acc_ref[...] = jnp.zeros(acc_ref.shape, F32)

    def logits_chunks(cs):
        for g in range(N_KV_HEADS):
            m = stat_ref[0:1, lanes_of(g)]
            for c in cs:
                lg = _dot_nt(k_ref[rows_of(c), head_cols(g)], qs[g]) * (HEAD_DIM ** -0.5)
                lg = lg + jnp.concatenate([bias_ref[rows_of(c), :]] * GROUP, axis=1)
                logit_ref[rows_of(c), lanes_of(g)] = lg
                m = jnp.maximum(m, jnp.max(lg, axis=0, keepdims=True))
            stat_ref[0:1, lanes_of(g)] = m

    _for_chunks(n_chunks, logits_chunks)

    def pv_chunks(cs):
        for g in range(N_KV_HEADS):
            m = stat_ref[0:1, lanes_of(g)]
            den = stat_ref[1:2, lanes_of(g)]
            for c in cs:
                e = jnp.exp(logit_ref[rows_of(c), lanes_of(g)] - m)
                acc_ref[g] += _dot(vt_ref[c, head_cols(g), :], e.astype(BF16))
                den = den + jnp.sum(e, axis=0, keepdims=True)
            stat_ref[1:2, lanes_of(g)] = den

    _for_chunks(n_chunks, pv_chunks)

    for g in range(N_KV_HEADS):
        o = acc_ref[g] / stat_ref[1:2, lanes_of(g)]
        for hh in range(GROUP):
            head = g * GROUP + hh
            att_ref[:, head * HEAD_DIM:(head + 1) * HEAD_DIM] = o[:, hh * t:(hh + 1) * t].T.astype(BF16)


def _dsa_prompt_call(q8, iq4, misct, ikb, kb, vt, batch, seq):
    t = ATTN_TILE
    n = batch * seq
    d = N_HEADS * HEAD_DIM
    nq = seq // t
    k_sel = min(TOPK_MAX, seq // 4)
    return pl.pallas_call(
        functools.partial(_dsa_prompt_kernel, k_sel=k_sel),
        grid=(batch, nq),
        in_specs=[
            pl.BlockSpec((N_HEADS, t, HEAD_DIM), lambda b, q: (0, b * nq + q, 0)),
            pl.BlockSpec((IDX_HEADS, t, IDX_DIM), lambda b, q: (0, b * nq + q, 0)),
            pl.BlockSpec((LANES, t), lambda b, q: (0, b * nq + q)),
            pl.BlockSpec((seq, IDX_DIM), lambda b, q: (b, 0)),
            pl.BlockSpec((seq, KV_W), lambda b, q: (b, 0)),
            pl.BlockSpec((nq, KV_W, t), lambda b, q: (b, 0, 0)),
        ],
        out_specs=pl.BlockSpec((t, d), lambda b, q: (b * nq + q, 0)),
        out_shape=jax.ShapeDtypeStruct((n, d), BF16),
        scratch_shapes=[
            pltpu.VMEM((seq, t), I32),
            pltpu.VMEM((seq, t), I16),
            pltpu.VMEM((seq, t), I16),
            pltpu.VMEM((seq, t), F32),
            pltpu.VMEM((seq, N_KV_HEADS * GROUP * t), F32),
            pltpu.VMEM((N_KV_HEADS, HEAD_DIM, GROUP * t), F32),
            pltpu.VMEM((SUBLANES, N_KV_HEADS * GROUP * t), F32),
            pltpu.VMEM((SUBLANES, t), I32),
        ],
        compiler_params=_params(grid_rank=2),
        name="dsa_prompt",
    )(q8, iq4, misct, ikb, kb, vt)


def _row_count(key_ref, indicator):
    r, l = key_ref.shape
    acc = jnp.zeros((r, LANES), I32)
    for c in range(l // LANES):
        acc = acc + indicator(key_ref[:, c * LANES:(c + 1) * LANES], c)
    return jnp.sum(acc, axis=-1, keepdims=True)


def _select_bias_rows(key_ref, bias_ref, jmax_ref, kq):
    r, l = key_ref.shape

    def ge_body(i, t):
        cand = t + lax.shift_left(jnp.int32(1), 31 - i)
        candb = jnp.broadcast_to(cand, (r, LANES))
        cnt = _row_count(key_ref, lambda k, c: jnp.where(k >= candb, 1, 0))
        return jnp.where(cnt >= kq, cand, t)

    thr = lax.fori_loop(0, 32, ge_body, jnp.full((r, 1), INT_MIN, I32))
    thrb = jnp.broadcast_to(thr, (r, LANES))
    cnt_gt = _row_count(key_ref, lambda k, c: jnp.where(k > thrb, 1, 0))
    cnt_ge = _row_count(key_ref, lambda k, c: jnp.where(k >= thrb, 1, 0))
    need = kq - cnt_gt

    jmax_ref[...] = jnp.full(jmax_ref.shape, l, I32)
    lane = lax.broadcasted_iota(I32, (r, LANES), 1)

    @pl.when(jnp.max(cnt_ge - kq) > 0)
    def _():
        nbits = max(1, (l - 1).bit_length())

        def lt_body(i, j):
            cand = j + lax.shift_left(jnp.int32(1), nbits - 1 - i)
            candb = jnp.broadcast_to(cand, (r, LANES))
            cnt = _row_count(key_ref, lambda k, c: jnp.where(k == thrb, jnp.where(lane + c * LANES < candb, 1, 0), 0))
            return jnp.where(cnt < need, cand, j)

        jmax_ref[...] = jnp.broadcast_to(lax.fori_loop(0, nbits, lt_body, jnp.zeros((r, 1), I32)), jmax_ref.shape)

    jmaxb = jmax_ref[...]
    for c in range(l // LANES):
        k = key_ref[:, c * LANES:(c + 1) * LANES]
        keep_tie = jnp.where(lane + c * LANES <= jmaxb, 0.0, NEG_INF)
        bias_ref[:, c * LANES:(c + 1) * LANES] = jnp.where(k > thrb, 0.0, jnp.where(k == thrb, keep_tie, NEG_INF))


SELECT_SEQS = 8
ATTEND_SEQS = 2


def _sample_select_kernel(pt_ref, iq4_ref, misc_ref, cikt_ref, bias_ref, iktbuf, iknew, key_ref, jmax_ref, sem,
                          *, k_sel, layer, ts):
    page = cikt_ref.shape[3]
    n_pages = pt_ref.shape[1]
    past = n_pages * page
    l_pad = past + LANES
    seqs = iktbuf.shape[0]
    step = pl.program_id(0)

    def page_copy(i, p):
        lanes = pl.ds(pl.multiple_of(p * page, page), page)
        return pltpu.make_async_copy(cikt_ref.at[layer, pt_ref[step * seqs + i, p]], iktbuf.at[i, :, lanes], sem.at[0])

    for i in range(seqs):
        lax.fori_loop(0, n_pages, lambda p, c, i=i: (page_copy(i, p).start(), c)[1], 0)

    iknew[...] = jnp.zeros(iknew.shape, F32)
    for i in range(seqs):
        iknew[i, 0:ts, :] = misc_ref[i * ts:(i + 1) * ts, :IDX_DIM]

    for i in range(seqs):
        lax.fori_loop(0, n_pages, lambda p, c, i=i: (page_copy(i, p).wait(), c)[1], 0)

    qpos = past + lax.broadcasted_iota(I32, (ts, 1), 0)
    kpos = lax.broadcasted_iota(I32, (ts, l_pad), 1)
    for i in range(seqs):
        rows = slice(i * ts, (i + 1) * ts)
        iq = iq4_ref[:, rows, :].reshape(IDX_HEADS * ts, IDX_DIM).astype(BF16)
        s = jnp.concatenate([_dot(iq, iktbuf[i].astype(BF16)), _dot_nt(iq, iknew[i].astype(BF16))], axis=1)
        iw = misc_ref[rows, IDX_DIM:IDX_DIM + IDX_HEADS]
        score = jnp.zeros((ts, l_pad), F32)
        for h in range(IDX_HEADS):
            score = score + jnp.maximum(s[h * ts:(h + 1) * ts], 0.0) * iw[:, h:h + 1]
        key_ref[rows, :] = _sortable_key(jnp.where(kpos <= qpos, score, NEG_INF))
    kq = jnp.minimum(k_sel, jnp.concatenate([qpos] * seqs, axis=0) + 1)
    _select_bias_rows(key_ref, bias_ref, jmax_ref, kq)


def _sample_attend_kernel(pt_ref, h_ref, q8_ref, kn_ref, vn_ref, bias_ref, ck_ref, cv_ref, wo_ref, o_ref,
                          kbuf, vbuf, knew, vnew, att_ref, sems, *, layer, ts):
    page_rows = ck_ref.shape[2]
    n_pages = pt_ref.shape[1]
    past = n_pages * page_rows // N_KV_HEADS
    seqs = kbuf.shape[0]
    step = pl.program_id(0)
    n_steps = pl.num_programs(0)

    def page_copies(i, p):
        src = pt_ref[step * seqs + i, p]
        rows = pl.ds(pl.multiple_of(p * page_rows, page_rows), page_rows)
        return (pltpu.make_async_copy(ck_ref.at[layer, src], kbuf.at[i, rows], sems.at[0, i]),
                pltpu.make_async_copy(cv_ref.at[layer, src], vbuf.at[i, rows], sems.at[1, i]))

    def start_page(i):
        def body(p, c):
            for cp in page_copies(i, p):
                cp.start()
            return c
        return body

    def wait_page(i):
        def body(p, c):
            for cp in page_copies(i, p):
                cp.wait()
            return c
        return body

    for i in range(seqs):
        lax.fori_loop(0, n_pages, start_page(i), 0)

    pad = jnp.zeros((LANES - ts, KV_W), F32)
    for i in range(seqs):
        rows = slice(i * ts, (i + 1) * ts)
        lax.fori_loop(0, n_pages, wait_page(i), 0)
        for g in range(N_KV_HEADS):
            new_rows = pl.ds(i * ts * N_KV_HEADS + g, ts, stride=N_KV_HEADS)
            knew[0:ts, g * HEAD_DIM:(g + 1) * HEAD_DIM] = kn_ref[new_rows, :]
            vnew[0:ts, g * HEAD_DIM:(g + 1) * HEAD_DIM] = vn_ref[new_rows, :]
        knew[ts:, :] = pad
        vnew[ts:, :] = pad
        bias = jnp.concatenate([bias_ref[rows, :]] * GROUP, axis=0)
        row0 = pl.multiple_of((step * seqs + i) * ts, ts)
        for g in range(N_KV_HEADS):
            cols = slice(g * HEAD_DIM, (g + 1) * HEAD_DIM)
            k_past = kbuf[i, pl.ds(g, past, stride=N_KV_HEADS), :].astype(BF16)
            v_past = vbuf[i, pl.ds(g, past, stride=N_KV_HEADS), :].astype(BF16)
            q = q8_ref[g * GROUP:(g + 1) * GROUP, rows, :].reshape(GROUP * ts, HEAD_DIM).astype(BF16)
            logits = jnp.concatenate([_dot_nt(q, k_past), _dot_nt(q, knew[:, cols].astype(BF16))], axis=1)
            logits = logits * (HEAD_DIM ** -0.5) + bias
            m = jnp.max(logits, axis=-1, keepdims=True)
            e = jnp.exp(logits - m)
            den = jnp.sum(e, axis=-1, keepdims=True)
            eb = e.astype(BF16)
            o = (_dot(eb[:, :past], v_past) + _dot(eb[:, past:], vnew[:, cols].astype(BF16))) / den
            for hh in range(GROUP):
                head = g * GROUP + hh
                att_ref[pl.ds(row0, ts), head * HEAD_DIM:(head + 1) * HEAD_DIM] = o[hh * ts:(hh + 1) * ts]

    @pl.when(step == n_steps - 1)
    def _():
        o_ref[...] = h_ref[...] + _dot(att_ref[...].astype(BF16), wo_ref[...])


def _dsa_sample_call(h, q8, iq4, misc, k_new, v_new, cache_k, cache_v, cache_ik, layer, page_table, wo, ts):
    n, d = h.shape
    nb = n // ts
    n_layers, n_pool, page = cache_k.shape[:3]
    n_pages = page_table.shape[1]
    past = n_pages * page
    l_pad = past + LANES
    k_sel = min(TOPK_MAX, (past + ts) // 4)
    ck = cache_k.reshape(n_layers, n_pool, page * N_KV_HEADS, HEAD_DIM)
    cv = cache_v.reshape(n_layers, n_pool, page * N_KV_HEADS, HEAD_DIM)
    cikt = jnp.swapaxes(cache_ik, 2, 3)
    sel_rows = SELECT_SEQS * ts
    bias = pl.pallas_call(
        functools.partial(_sample_select_kernel, k_sel=k_sel, layer=layer, ts=ts),
        grid_spec=pltpu.PrefetchScalarGridSpec(
            num_scalar_prefetch=1,
            grid=(nb // SELECT_SEQS,),
            in_specs=[
                pl.BlockSpec((IDX_HEADS, sel_rows, IDX_DIM), lambda s, pt: (0, s, 0)),
                pl.BlockSpec((sel_rows, LANES), lambda s, pt: (s, 0)),
                pl.BlockSpec(memory_space=pl.ANY),
            ],
            out_specs=pl.BlockSpec((sel_rows, l_pad), lambda s, pt: (s, 0)),
            scratch_shapes=[
                pltpu.VMEM((SELECT_SEQS, IDX_DIM, past), F32),
                pltpu.VMEM((SELECT_SEQS, LANES, IDX_DIM), F32),
                pltpu.VMEM((sel_rows, l_pad), I32),
                pltpu.VMEM((sel_rows, LANES), I32),
                pltpu.SemaphoreType.DMA((1,)),
            ],
        ),
        out_shape=jax.ShapeDtypeStruct((n, l_pad), F32),
        compiler_params=_params(),
        name="sample_select",
    )(page_table, iq4, misc, cikt)

    att_rows = ATTEND_SEQS * ts
    return pl.pallas_call(
        functools.partial(_sample_attend_kernel, layer=layer, ts=ts),
        grid_spec=pltpu.PrefetchScalarGridSpec(
            num_scalar_prefetch=1,
            grid=(nb // ATTEND_SEQS,),
            in_specs=[
                pl.BlockSpec((n, d), lambda s, pt: (0, 0)),
                pl.BlockSpec((N_HEADS, att_rows, HEAD_DIM), lambda s, pt: (0, s, 0)),
                pl.BlockSpec((att_rows * N_KV_HEADS, HEAD_DIM), lambda s, pt: (s, 0)),
                pl.BlockSpec((att_rows * N_KV_HEADS, HEAD_DIM), lambda s, pt: (s, 0)),
                pl.BlockSpec((att_rows, l_pad), lambda s, pt: (s, 0)),
                pl.BlockSpec(memory_space=pl.ANY),
                pl.BlockSpec(memory_space=pl.ANY),
                pl.BlockSpec((d, d), lambda s, pt: (0, 0)),
            ],
            out_specs=pl.BlockSpec((n, d), lambda s, pt: (0, 0)),
            scratch_shapes=[
                pltpu.VMEM((ATTEND_SEQS, past * N_KV_HEADS, HEAD_DIM), F32),
                pltpu.VMEM((ATTEND_SEQS, past * N_KV_HEADS, HEAD_DIM), F32),
                pltpu.VMEM((LANES, KV_W), F32),
                pltpu.VMEM((LANES, KV_W), F32),
                pltpu.VMEM((n, d), F32),
                pltpu.SemaphoreType.DMA((2, ATTEND_SEQS)),
            ],
        ),
        out_shape=jax.ShapeDtypeStruct((n, d), F32),
        compiler_params=_params(),
        name="sample_attend",
    )(page_table, h, q8, k_new, v_new, bias, ck, cv, wo)


def _sgu_kernel(h_ref, g_ref, win_ref, bin_ref, gv_ref, ws_ref, bs_ref, wout_ref, o_ref, *rest,
                seg, emit_v):
    if emit_v:
        v_ref, gated_ref = rest
    else:
        (gated_ref,) = rest
    tm, d = h_ref.shape
    c_len = ws_ref.shape[1]
    d_sg = gv_ref.shape[1]
    gw = d_sg // SG_GROUPS
    x = h_ref[...]
    xn = _rms(x, g_ref[...]).astype(BF16)
    z = _dot(xn, win_ref[...]) + bin_ref[...]
    z = 0.5 * z * (1.0 + lax.erf(z * (0.5 ** 0.5)))
    u = z[:, :d_sg]
    v = _rms(z[:, d_sg:], gv_ref[...])
    if emit_v:
        v_ref[...] = v
    vb = v.astype(BF16)
    row = lax.broadcasted_iota(I32, (c_len, c_len), 0)
    col = lax.broadcasted_iota(I32, (c_len, c_len), 1)
    same_seq = (row >> _log2(seg)) == (col >> _log2(seg))
    if seg != c_len:
        src = lax.broadcasted_iota(I32, (ws_ref.shape[2], c_len), 0)
        dst = lax.broadcasted_iota(I32, (ws_ref.shape[2], c_len), 1)
        spread = jnp.where((dst & (seg - 1)) == src, 1.0, 0.0).astype(BF16)
    for g in range(SG_GROUPS):
        wsg = ws_ref[g] if seg == c_len else _dot(ws_ref[g].astype(BF16), spread)
        wg = jnp.where(col <= row, jnp.where(same_seq, wsg, 0.0), 0.0).astype(BF16)
        bg = bs_ref[:, g:g + 1]
        for ch in range(tm // c_len):
            rows = slice(ch * c_len, (ch + 1) * c_len)
            cols = slice(g * gw, (g + 1) * gw)
            mixed = _dot(wg, vb[rows, cols]) + bg
            gated_ref[rows, cols] = (u[rows, cols] * mixed).astype(BF16)
    o_ref[...] = x + _dot(gated_ref[...], wout_ref[...])


def _sgu_call(h, g, win, b_in, gv, ws, bs_t, wout, tm, seg, emit_v):
    n, d = h.shape
    d2 = win.shape[1]
    d_sg = d2 // 2
    c_len = ws.shape[1]
    out_shape = [jax.ShapeDtypeStruct((n, d), F32)]
    out_specs = [pl.BlockSpec((tm, d), lambda i: (i, 0))]
    if emit_v:
        out_shape.append(jax.ShapeDtypeStruct((n, d_sg), F32))
        out_specs.append(pl.BlockSpec((tm, d_sg), lambda i: (i, 0)))
    res = pl.pallas_call(
        functools.partial(_sgu_kernel, seg=seg, emit_v=emit_v),
        grid=(n // tm,),
        in_specs=[
            pl.BlockSpec((tm, d), lambda i: (i, 0)),
            pl.BlockSpec((1, d), lambda i: (0, 0)),
            pl.BlockSpec((d, d2), lambda i: (0, 0)),
            pl.BlockSpec((1, d2), lambda i: (0, 0)),
            pl.BlockSpec((1, d_sg), lambda i: (0, 0)),
            pl.BlockSpec(ws.shape, lambda i: (0, 0, 0)),
            pl.BlockSpec((c_len, SG_GROUPS), lambda i: (0, 0)),
            pl.BlockSpec((d_sg, d), lambda i: (0, 0)),
        ],
        out_specs=tuple(out_specs),
        out_shape=tuple(out_shape),
        scratch_shapes=[pltpu.VMEM((tm, d_sg), BF16)],
        compiler_params=_params(),
        name="sgu",
    )(h, g.reshape(1, d), win, b_in.reshape(1, d2), gv.reshape(1, d_sg), ws, bs_t, wout)
    return res


FFN_CHUNK = 256


def _ffn_body(x, p_ref, gf_ref, wup_ref, cw_ref, cb_ref, wdn_ref, gp_ref, wple_ref, wgate_ref, gfin_ref,
              o_ref, act_ref, shifted, emit_up, final_norm):
    d_ff = wdn_ref.shape[0]
    xn = _rms(x, gf_ref[...]).astype(BF16)

    def conv(cols):
        up = _dot(xn, wup_ref[:, cols])
        m1, m2 = shifted(up, cols)
        emit_up(up, cols)
        return cb_ref[:, cols] + cw_ref[0:1, cols] * m2 + cw_ref[1:2, cols] * m1 + cw_ref[2:3, cols] * up

    for c in range(d_ff // FFN_CHUNK):
        gate = conv(slice(c * FFN_CHUNK, (c + 1) * FFN_CHUNK))
        val = conv(slice(d_ff + c * FFN_CHUNK, d_ff + (c + 1) * FFN_CHUNK))
        act_ref[:, c * FFN_CHUNK:(c + 1) * FFN_CHUNK] = (gate * _sigmoid(gate) * val).astype(BF16)
    h2 = x + _dot(act_ref[...], wdn_ref[...])
    gate = _sigmoid(_dot(_rms(h2, gp_ref[...]).astype(BF16), wgate_ref[...]))
    h3 = h2 + _dot(p_ref[...].astype(BF16), wple_ref[...]) * gate
    o_ref[...] = _rms(h3, gfin_ref[...]) if final_norm else h3


def _ffn_prompt_kernel(*refs, tiles_per_seq, final_norm, with_attn):
    if with_attn:
        att_ref, wo_ref, *refs = refs
    (h_ref, p_ref, gf_ref, wup_ref, cw_ref, cb_ref, wdn_ref, gp_ref, wple_ref, wgate_ref, gfin_ref,
     o_ref, tail_ref, carry_ref, act_ref) = refs
    tm = h_ref.shape[0]
    i = pl.program_id(0)

    @pl.when(i % tiles_per_seq == 0)
    def _():
        carry_ref[...] = jnp.zeros(carry_ref.shape, F32)

    row = lax.broadcasted_iota(I32, (tm, FFN_CHUNK), 0)

    def shifted(up, cols):
        prev = carry_ref[:, cols]
        p1 = prev[SUBLANES - 1:SUBLANES]
        p2 = prev[SUBLANES - 2:SUBLANES - 1]
        m1 = jnp.where(row >= 1, pltpu.roll(up, 1, 0), p1)
        m2 = jnp.where(row >= 2, pltpu.roll(up, 2, 0), jnp.where(row == 0, p2, p1))
        return m1, m2

    def emit_up(up, cols):
        last = up[tm - SUBLANES:tm]
        carry_ref[:, cols] = last
        tail_ref[0, :, cols] = last

    x = h_ref[...]
    if with_attn:
        x = x + _dot(att_ref[...], wo_ref[...])
    _ffn_body(x, p_ref, gf_ref, wup_ref, cw_ref, cb_ref, wdn_ref, gp_ref, wple_ref, wgate_ref, gfin_ref,
              o_ref, act_ref, shifted, emit_up, final_norm)


def _ffn_sample_kernel(h_ref, p_ref, pm1_ref, pm2_ref, gf_ref, wup_ref, cw_ref, cb_ref, wdn_ref, gp_ref, wple_ref,
                       wgate_ref, gfin_ref, o_ref, up_ref, act_ref, *, seg, final_norm):
    tm = h_ref.shape[0]
    row = lax.broadcasted_iota(I32, (tm, FFN_CHUNK), 0) & ((1 << _log2(seg)) - 1)

    def shifted(up, cols):
        m1 = jnp.where(row >= 1, pltpu.roll(up, 1, 0), pm1_ref[:, cols])
        m2 = jnp.where(row >= 2, pltpu.roll(up, 2, 0), pm2_ref[:, cols])
        return m1, m2

    def emit_up(up, cols):
        up_ref[:, cols] = up

    _ffn_body(h_ref[...], p_ref, gf_ref, wup_ref, cw_ref, cb_ref, wdn_ref, gp_ref, wple_ref, wgate_ref, gfin_ref,
              o_ref, act_ref, shifted, emit_up, final_norm)


def _ffn_weight_specs(d, f2, d_ff, ple):
    full = lambda shape: pl.BlockSpec(shape, lambda i: (0,) * len(shape))
    return [full((1, d)), full((d, f2)), full((CONV_W, f2)), full((1, f2)), full((d_ff, d)),
            full((1, d)), full((ple, d)), full((d, d)), full((1, d))]


def _ffn_prompt_call(h, att, wo, p_all, layer, weights, tm, tiles_per_seq, final_norm):
    n, d = h.shape
    gf, wup, cw, cb, wdn, gp, wple, wgate, gfin = weights
    f2 = wup.shape[1]
    d_ff = wdn.shape[0]
    ple = p_all.shape[2]
    nt = n // tm
    with_attn = att is not None
    attn_specs = [pl.BlockSpec((tm, d), lambda i: (i, 0)), pl.BlockSpec((d, d), lambda i: (0, 0))] if with_attn else []
    attn_args = (att, wo) if with_attn else ()
    return pl.pallas_call(
        functools.partial(_ffn_prompt_kernel, tiles_per_seq=tiles_per_seq, final_norm=final_norm,
                          with_attn=with_attn),
        grid=(nt,),
        in_specs=attn_specs
        + [pl.BlockSpec((tm, d), lambda i: (i, 0)), pl.BlockSpec((None, tm, ple), lambda i: (layer, i, 0))]
        + _ffn_weight_specs(d, f2, d_ff, ple),
        out_specs=(pl.BlockSpec((tm, d), lambda i: (i, 0)), pl.BlockSpec((1, SUBLANES, f2), lambda i: (i, 0, 0))),
        out_shape=(jax.ShapeDtypeStruct((n, d), F32), jax.ShapeDtypeStruct((nt, SUBLANES, f2), F32)),
        scratch_shapes=[pltpu.VMEM((SUBLANES, f2), F32), pltpu.VMEM((tm, d_ff), BF16)],
        compiler_params=_params(),
        name="ffn_prompt",
    )(*attn_args, h, p_all, gf, wup, cw, cb, wdn, gp, wple, wgate, gfin)


def _ffn_sample_call(h, p, pm1, pm2, weights, seg, final_norm):
    n, d = h.shape
    gf, wup, cw, cb, wdn, gp, wple, wgate, gfin = weights
    f2 = wup.shape[1]
    d_ff = wdn.shape[0]
    ple = p.shape[1]
    full = lambda shape: pl.BlockSpec(shape, lambda i: (0,) * len(shape))
    return pl.pallas_call(
        functools.partial(_ffn_sample_kernel, seg=seg, final_norm=final_norm),
        grid=(1,),
        in_specs=[full((n, d)), full((n, ple)), full((n, f2)), full((n, f2))] + _ffn_weight_specs(d, f2, d_ff, ple),
        out_specs=(full((n, d)), full((n, f2))),
        out_shape=(jax.ShapeDtypeStruct((n, d), F32), jax.ShapeDtypeStruct((n, f2), F32)),
        scratch_shapes=[pltpu.VMEM((n, d_ff), BF16)],
        compiler_params=_params(),
        name="ffn_sample",
    )(h, p, pm1, pm2, gf, wup, cw, cb, wdn, gp, wple, wgate, gfin)


PROMPT_TILE = 512


def kernel(x_prompt, x_sample, cache_k, cache_v, cache_idx_k, state_conv, page_table, p_prompt, p_sample,
           norm_mix, w_attn_in, w_attn_out, w_sg_in, b_sg_in, norm_sg_v, w_sg_spatial, b_sg_spatial, w_sg_out,
           norm_ffn, w_ffn_up, w_ffn_conv, b_ffn_conv, w_ffn_down, norm_ple, w_ple, w_ple_gate, norm_final):
    batch, seq, d = x_prompt.shape
    nb, ts, _ = x_sample.shape
    depth = norm_mix.shape[0]
    page = cache_k.shape[2]
    past = page_table.shape[1] * page
    f2 = w_ffn_up.shape[2]
    n_s = nb * ts
    tm = PROMPT_TILE

    hp = x_prompt.reshape(batch * seq, d)
    hs = x_sample.reshape(n_s, d)
    tabs_p = _rope_tables(jnp.arange(seq))
    tabs_s = tuple(jnp.tile(t, (nb, 1)) for t in _rope_tables(past + jnp.arange(ts)))

    kp_l, vp_l, ikp_l, ks_l, vs_l, iks_l, cp_l, cs_l, sgv_l = [], [], [], [], [], [], [], [], []
    for i in range(depth):
        j = i // 2
        att = w_out = None
        if i % 2 == 0:
            w_in = jnp.pad(w_attn_in[j], ((0, 0), (0, PROJ_W - w_attn_in.shape[2]))).astype(BF16)
            w_out = w_attn_out[j].astype(BF16)
            q8, k, v, kb, vt, iq4, ikt, misct, ikb = _proj_call(hp, norm_mix[i], w_in, tabs_p, tm, True)
            att = _dsa_prompt_call(q8, iq4, misct, ikb, kb, vt, batch, seq)
            kp_l.append(k.reshape(batch, seq, N_KV_HEADS, HEAD_DIM))
            vp_l.append(v.reshape(batch, seq, N_KV_HEADS, HEAD_DIM))
            ikp_l.append(jnp.swapaxes(ikt, 1, 2))
            q8, k, v, iq4, misc = _proj_call(hs, norm_mix[i], w_in, tabs_s, n_s, False)
            hs = _dsa_sample_call(hs, q8, iq4, misc, k, v, cache_k, cache_v, cache_idx_k, j,
                                  page_table, w_out, ts)
            ks_l.append(k.reshape(nb, ts, N_KV_HEADS, HEAD_DIM))
            vs_l.append(v.reshape(nb, ts, N_KV_HEADS, HEAD_DIM))
            iks_l.append(misc[:, :IDX_DIM].reshape(nb, ts, IDX_DIM))
        else:
            win = w_sg_in[j].astype(BF16)
            wout = w_sg_out[j].astype(BF16)
            (hp,) = _sgu_call(hp, norm_mix[i], win, b_sg_in[j], norm_sg_v[j], w_sg_spatial[j],
                              b_sg_spatial[j].T, wout, tm, SG_CHUNK, False)
            ws_s = jnp.tile(jnp.pad(w_sg_spatial[j][:, :ts, :ts], ((0, 0), (0, 0), (0, LANES - ts))), (1, nb, 1))
            bs_s = jnp.tile(b_sg_spatial[j].T[:ts], (nb, 1))
            hs, v_rows = _sgu_call(hs, norm_mix[i], win, b_sg_in[j], norm_sg_v[j], ws_s, bs_s, wout, n_s, ts, True)
            sgv_l.append(v_rows.reshape(nb, ts, -1))

        final = i == depth - 1
        weights = (norm_ffn[i].reshape(1, d), w_ffn_up[i].astype(BF16), w_ffn_conv[i], b_ffn_conv[i].reshape(1, f2),
                   w_ffn_down[i].astype(BF16), norm_ple[i].reshape(1, d), w_ple[i].astype(BF16),
                   w_ple_gate[i].astype(BF16), norm_final.reshape(1, d))
        hp, tail = _ffn_prompt_call(hp, att, w_out, p_prompt.reshape(depth, batch * seq, -1), i, weights,
                                    tm, seq // tm, final)
        cp_l.append(tail.reshape(batch, seq // tm, SUBLANES, f2)[:, -1, SUBLANES - (CONV_W - 1):])
        st = state_conv[i]
        zeros = jnp.zeros((nb, ts - 1, f2), F32)
        pm1 = jnp.concatenate([st[:, 1:2], zeros], axis=1).reshape(n_s, f2)
        pm2 = jnp.concatenate([st, zeros[:, 1:]], axis=1).reshape(n_s, f2)
        hs, up_s = _ffn_sample_call(hs, p_sample[i].reshape(n_s, -1), pm1, pm2, weights, ts, final)
        cs_l.append(up_s.reshape(nb, ts, f2)[:, ts - (CONV_W - 1):])

    return (hp.reshape(batch, seq, d), hs.reshape(nb, ts, d),
            jnp.stack(kp_l), jnp.stack(vp_l), jnp.stack(ikp_l),
            jnp.stack(ks_l), jnp.stack(vs_l), jnp.stack(iks_l),
            jnp.stack(cp_l), jnp.stack(cs_l), jnp.stack(sgv_l))
```

```python
import functools

import jax
import jax.numpy as jnp
from jax import lax
from jax.experimental import pallas as pl
from jax.experimental.pallas import tpu as pltpu

F32 = jnp.float32
BF16 = jnp.bfloat16
I32 = jnp.int32

N_HEADS = 8
N_KV_HEADS = 2
GROUP = N_HEADS // N_KV_HEADS
HEAD_DIM = 128
IDX_HEADS = 4
IDX_DIM = 64
IDX_W_SCALE = (IDX_HEADS * IDX_DIM) ** -0.5
TOPK_MAX = 256
SG_CHUNK = 128
SG_GROUPS = 8
CONV_W = 3
ROPE_THETA = 10000.0
EPS = 1e-6

LANES = 128
SUBLANES = 8
V7X_VMEM_BYTES = 64 * 1024 * 1024
VMEM_LIMIT_BYTES = V7X_VMEM_BYTES - 8 * 1024 * 1024

NEG_INF = float("-inf")
LOG2_E = 1.4426950408889634
INT_MIN = -(2 ** 31)


def _params(grid_rank=1):
    return pltpu.CompilerParams(dimension_semantics=("arbitrary",) * grid_rank,
                                vmem_limit_bytes=VMEM_LIMIT_BYTES)


def _log2(n):
    assert n > 0 and n & (n - 1) == 0, n
    return n.bit_length() - 1


def _rms(x, g):
    ms = jnp.mean(x * x, axis=-1, keepdims=True)
    return x * lax.rsqrt(ms + EPS) * g


def _sigmoid(x):
    return 1.0 / (1.0 + jnp.exp(-x))


def _dot(a, b):
    return jnp.dot(a, b, preferred_element_type=F32)


def _dot_nt(a, b):
    return lax.dot_general(a, b, (((1,), (1,)), ((), ())), preferred_element_type=F32)


def _sortable_key(score):
    b = pltpu.bitcast(score, I32)
    b = jnp.where(b == INT_MIN, 0, b)
    return b ^ ((b >> 31) & 0x7FFFFFFF)


Q_OFF = 0
K_OFF = N_HEADS * HEAD_DIM
V_OFF = K_OFF + N_KV_HEADS * HEAD_DIM
IQ_OFF = V_OFF + N_KV_HEADS * HEAD_DIM
MISC_OFF = IQ_OFF + IDX_HEADS * IDX_DIM
PROJ_W = MISC_OFF + LANES
KV_W = N_KV_HEADS * HEAD_DIM
ATTN_TILE = 256


def _project(h_ref, g_ref, w_ref, tab_refs, q8_ref, k_ref, v_ref, iq4_ref):
    c128_ref, s128_ref, c64_ref, s64_ref, cm_ref, sm_ref = tab_refs
    tm = h_ref.shape[0]
    xn = _rms(h_ref[...], g_ref[...]).astype(BF16)
    y = _dot(xn, w_ref[...])
    c128 = c128_ref[...]
    s128 = s128_ref[...]

    def rope128(t):
        return t * c128 + pltpu.roll(t, HEAD_DIM // 2, 1) * s128

    for h in range(N_HEADS):
        q8_ref[h] = rope128(y[:, Q_OFF + h * HEAD_DIM:Q_OFF + (h + 1) * HEAD_DIM]).astype(q8_ref.dtype)
    ks = []
    for h in range(N_KV_HEADS):
        kh = rope128(y[:, K_OFF + h * HEAD_DIM:K_OFF + (h + 1) * HEAD_DIM])
        k_ref[pl.ds(h, tm, stride=N_KV_HEADS), :] = kh
        v_ref[pl.ds(h, tm, stride=N_KV_HEADS), :] = y[:, V_OFF + h * HEAD_DIM:V_OFF + (h + 1) * HEAD_DIM]
        ks.append(kh)
    v = y[:, V_OFF:IQ_OFF]

    lane = lax.broadcasted_iota(I32, (tm, LANES), 1)
    low_half = (lane & (IDX_DIM - 1)) < (IDX_DIM // 2)

    def rope64(t, c, s):
        partner = jnp.where(low_half, pltpu.roll(t, LANES - IDX_DIM // 2, 1), pltpu.roll(t, IDX_DIM // 2, 1))
        return t * c + partner * s

    c64 = c64_ref[...]
    s64 = s64_ref[...]
    for pair in range(IDX_HEADS // 2):
        t = rope64(y[:, IQ_OFF + pair * LANES:IQ_OFF + (pair + 1) * LANES], c64, s64)
        iq4_ref[2 * pair] = t[:, :IDX_DIM].astype(iq4_ref.dtype)
        iq4_ref[2 * pair + 1] = t[:, IDX_DIM:].astype(iq4_ref.dtype)
    m = rope64(y[:, MISC_OFF:MISC_OFF + LANES], cm_ref[...], sm_ref[...])
    return ks, v, m


def _proj_prompt_kernel(h_ref, g_ref, w_ref, c128_ref, s128_ref, c64_ref, s64_ref, cm_ref, sm_ref,
                        q8_ref, k_ref, v_ref, kb_ref, vt_ref, iq4_ref, ikt_ref, misct_ref, ikb_ref):
    tm = h_ref.shape[0]
    ks, v, m = _project(h_ref, g_ref, w_ref, (c128_ref, s128_ref, c64_ref, s64_ref, cm_ref, sm_ref),
                        q8_ref, k_ref, v_ref, iq4_ref)
    for h in range(N_KV_HEADS):
        kb_ref[:, h * HEAD_DIM:(h + 1) * HEAD_DIM] = ks[h].astype(BF16)
    for c in range(tm // ATTN_TILE):
        vt_ref[c] = v[c * ATTN_TILE:(c + 1) * ATTN_TILE].T.astype(BF16)
    mt = m.T
    misct_ref[...] = mt
    ikt_ref[0] = mt[:IDX_DIM, :]
    ikb_ref[...] = m[:, :IDX_DIM].astype(BF16)


def _proj_sample_kernel(h_ref, g_ref, w_ref, c128_ref, s128_ref, c64_ref, s64_ref, cm_ref, sm_ref,
                        q8_ref, k_ref, v_ref, iq4_ref, misc_ref):
    _, _, m = _project(h_ref, g_ref, w_ref, (c128_ref, s128_ref, c64_ref, s64_ref, cm_ref, sm_ref),
                       q8_ref, k_ref, v_ref, iq4_ref)
    misc_ref[...] = m


def _proj_call(h, g, w, tabs, tm, prompt):
    n, d = h.shape
    t_tab = tabs[0].shape[0]
    n_tab = t_tab // tm
    tab_spec = pl.BlockSpec((tm, LANES), lambda i: (i % n_tab, 0))
    row = lambda w_: pl.BlockSpec((tm, w_), lambda i: (i, 0))
    qdtype = BF16 if prompt else F32
    q8 = (jax.ShapeDtypeStruct((N_HEADS, n, HEAD_DIM), qdtype), pl.BlockSpec((N_HEADS, tm, HEAD_DIM), lambda i: (0, i, 0)))
    iq4 = (jax.ShapeDtypeStruct((IDX_HEADS, n, IDX_DIM), qdtype), pl.BlockSpec((IDX_HEADS, tm, IDX_DIM), lambda i: (0, i, 0)))
    kf = (jax.ShapeDtypeStruct((n * N_KV_HEADS, HEAD_DIM), F32),
          pl.BlockSpec((tm * N_KV_HEADS, HEAD_DIM), lambda i: (i, 0)))
    if prompt:
        assert tm % ATTN_TILE == 0
        outs = [q8, kf, kf, (jax.ShapeDtypeStruct((n, KV_W), BF16), row(KV_W)),
                (jax.ShapeDtypeStruct((n // ATTN_TILE, KV_W, ATTN_TILE), BF16),
                 pl.BlockSpec((tm // ATTN_TILE, KV_W, ATTN_TILE), lambda i: (i, 0, 0))),
                iq4,
                (jax.ShapeDtypeStruct((n // t_tab, IDX_DIM, t_tab), F32),
                 pl.BlockSpec((1, IDX_DIM, tm), lambda i: (i // n_tab, 0, i % n_tab))),
                (jax.ShapeDtypeStruct((LANES, n), F32), pl.BlockSpec((LANES, tm), lambda i: (0, i))),
                (jax.ShapeDtypeStruct((n, IDX_DIM), BF16), row(IDX_DIM))]
        body = _proj_prompt_kernel
    else:
        outs = [q8, kf, kf, iq4, (jax.ShapeDtypeStruct((n, LANES), F32), row(LANES))]
        body = _proj_sample_kernel
    return pl.pallas_call(
        body,
        grid=(n // tm,),
        in_specs=[
            pl.BlockSpec((tm, d), lambda i: (i, 0)),
            pl.BlockSpec((1, d), lambda i: (0, 0)),
            pl.BlockSpec((d, PROJ_W), lambda i: (0, 0)),
        ] + [tab_spec] * 6,
        out_specs=tuple(o[1] for o in outs),
        out_shape=tuple(o[0] for o in outs),
        compiler_params=_params(),
        name="attn_proj_prompt" if prompt else "attn_proj_sample",
    )(h, g.reshape(1, d), w, *tabs)


def _rope_tables(pos):
    t = pos.shape[0]

    def tab(half):
        inv = ROPE_THETA ** (-jnp.arange(half, dtype=F32) / half)
        ang = pos.astype(F32)[:, None] * inv[None, :]
        return jnp.cos(ang), jnp.sin(ang)

    c64, s64 = tab(HEAD_DIM // 2)
    c32, s32 = tab(IDX_DIM // 2)
    c128 = jnp.concatenate([c64, c64], axis=1)
    s128 = jnp.concatenate([-s64, s64], axis=1)
    cq = jnp.concatenate([c32, c32, c32, c32], axis=1)
    sq = jnp.concatenate([-s32, s32, -s32, s32], axis=1)
    cm = jnp.concatenate([c32, c32, jnp.full((t, LANES - IDX_DIM), IDX_W_SCALE, F32)], axis=1)
    sm = jnp.concatenate([-s32, s32, jnp.zeros((t, LANES - IDX_DIM), F32)], axis=1)
    return (c128, s128, cq, sq, cm, sm)


COUNT_SLAB = 64


def _col_count(key_ref, n_rows, indicator):
    w = key_ref.shape[1]
    acc = jnp.zeros((COUNT_SLAB, w), I32)
    for c in range(n_rows // COUNT_SLAB):
        acc = acc + indicator(key_ref[c * COUNT_SLAB:(c + 1) * COUNT_SLAB, :], c * COUNT_SLAB)
    return jnp.sum(acc, axis=0, keepdims=True)


I16 = jnp.int16
I16_MIN = -(2 ** 15)
HALF_SLAB = 128
PACK_ROWS = 16


def _bcast16(x, rows):
    one = jnp.broadcast_to(x, (PACK_ROWS, x.shape[1])).astype(I16)
    return jnp.concatenate([one] * (rows // PACK_ROWS), axis=0)


def _col_count16(half_ref, n_rows, indicator):
    w = half_ref.shape[1]
    acc = jnp.zeros((HALF_SLAB, w), I16)
    for c in range(n_rows // HALF_SLAB):
        acc = acc + indicator(half_ref[c * HALF_SLAB:(c + 1) * HALF_SLAB, :])
    return jnp.sum(acc.astype(I32), axis=0, keepdims=True)


def _kth_largest16(half_ref, n_rows, kk):
    w = half_ref.shape[1]
    one = jnp.ones((HALF_SLAB, w), I16)
    zero = jnp.zeros((HALF_SLAB, w), I16)

    def body(i, t):
        cand = t + lax.shift_left(jnp.int32(1), 15 - i)
        candb = _bcast16(cand, HALF_SLAB)
        cnt = _col_count16(half_ref, n_rows, lambda k: jnp.where(k >= candb, one, zero))
        return jnp.where(cnt >= kk, cand, t)

    return lax.fori_loop(0, 16, body, jnp.full((1, w), I16_MIN, I32))


def _select_bias_cols(key_ref, hi_ref, lo_ref, bias_ref, n_rows, kq):
    w = key_ref.shape[1]
    one = jnp.ones((HALF_SLAB, w), I16)
    zero = jnp.zeros((HALF_SLAB, w), I16)

    t_hi = _kth_largest16(hi_ref, n_rows, kq)
    t_hib = _bcast16(t_hi, HALF_SLAB)
    above = _col_count16(hi_ref, n_rows, lambda k: jnp.where(k > t_hib, one, zero))
    lowest = jnp.full((HALF_SLAB, w), I16_MIN, I16)
    for c in range(n_rows // HALF_SLAB):
        rows = slice(c * HALF_SLAB, (c + 1) * HALF_SLAB)
        lo_ref[rows, :] = jnp.where(hi_ref[rows, :] == t_hib, lo_ref[rows, :], lowest)
    t_lo = _kth_largest16(lo_ref, n_rows, kq - above)
    thr = t_hi * 65536 + (t_lo - I16_MIN)

    cnt_gt = _col_count(key_ref, n_rows, lambda k, r0: jnp.where(k > thr, 1, 0))
    cnt_ge = _col_count(key_ref, n_rows, lambda k, r0: jnp.where(k >= thr, 1, 0))
    need = kq - cnt_gt

    drop_ties = jnp.max(cnt_ge - kq) > 0

    @pl.when(jnp.logical_not(drop_ties))
    def _():
        for c in range(n_rows // COUNT_SLAB):
            rows = slice(c * COUNT_SLAB, (c + 1) * COUNT_SLAB)
            bias_ref[rows, :] = jnp.where(key_ref[rows, :] >= thr, 0.0, NEG_INF)

    @pl.when(drop_ties)
    def _():
        nbits = max(1, (n_rows - 1).bit_length())
        sub = lax.broadcasted_iota(I32, (COUNT_SLAB, w), 0)

        def lt_body(i, j):
            cand = j + lax.shift_left(jnp.int32(1), nbits - 1 - i)
            cnt = _col_count(key_ref, n_rows,
                             lambda k, r0: jnp.where(k == thr, jnp.where(sub + r0 < cand, 1, 0), 0))
            return jnp.where(cnt < need, cand, j)

        jmax = lax.fori_loop(0, nbits, lt_body, jnp.zeros((1, w), I32))
        for c in range(n_rows // COUNT_SLAB):
            rows = slice(c * COUNT_SLAB, (c + 1) * COUNT_SLAB)
            k = key_ref[rows, :]
            keep_tie = jnp.where(sub + c * COUNT_SLAB <= jmax, 0.0, NEG_INF)
            bias_ref[rows, :] = jnp.where(k > thr, 0.0, jnp.where(k == thr, keep_tie, NEG_INF))


def _for_chunks(n, body):
    def pair(p, carry):
        body([2 * p, 2 * p + 1])
        return carry

    lax.fori_loop(0, lax.shift_right_logical(n, 1), pair, 0)

    @pl.when((n & 1) == 1)
    def _():
        body([n - 1])


def _dsa_prompt_kernel(q8_ref, iq4_ref, misct_ref, ik_ref, k_ref, vt_ref, att_ref,
                       key_ref, hi_ref, lo_ref, bias_ref, logit_ref, acc_ref, stat_ref, *, k_sel):
    t = q8_ref.shape[1]
    nq = key_ref.shape[0] // t
    gw = GROUP * t
    qi = pl.program_id(1)
    n_chunks = qi + 1
    qpos = qi * t + lax.broadcasted_iota(I32, (1, t), 1)
    rows_of = lambda c: pl.ds(pl.multiple_of(c * t, t), t)

    iq = iq4_ref[...].reshape(IDX_HEADS * t, IDX_DIM)
    iw = misct_ref[IDX_DIM:IDX_DIM + SUBLANES, :]

    def score_chunks(cs):
        for c in cs:
            s = _dot_nt(ik_ref[rows_of(c), :], iq)
            score = jnp.zeros((t, t), F32)
            for h in range(IDX_HEADS):
                score = score + jnp.maximum(s[:, h * t:(h + 1) * t], 0.0) * iw[h:h + 1, :]
            kpos = c * t + lax.broadcasted_iota(I32, (t, t), 0)
            key = _sortable_key(jnp.where(kpos <= qpos, score, NEG_INF))
            key_ref[rows_of(c), :] = key
            hi_ref[rows_of(c), :] = (key >> 16).astype(I16)
            lo_ref[rows_of(c), :] = ((key & 0xFFFF) + I16_MIN).astype(I16)

    _for_chunks(n_chunks, score_chunks)

    kq = jnp.minimum(k_sel, qpos + 1)
    for j in range(nq):
        @pl.when(qi == j)
        def _():
            _select_bias_cols(key_ref, hi_ref, lo_ref, bias_ref, (j + 1) * t, kq)

    qs = [q8_ref[g * GROUP:(g + 1) * GROUP].reshape(gw, HEAD_DIM) for g in range(N_KV_HEADS)]
    head_cols = lambda g: slice(g * HEAD_DIM, (g + 1) * HEAD_DIM)
    lanes_of = lambda g: slice(g * gw, (g + 1) * gw)
    stat_ref[0:1, :] = jnp.full((1, N_KV_HEADS * gw), NEG_INF, F32)
    stat_ref[1:2, :] = jnp.zeros((1, N_KV_HEADS * gw), F32)
    acc_ref[...] = jnp.zeros(acc_ref.shape, F32)

    def logits_chunks(cs):
        for g in range(N_KV_HEADS):
            m = stat_ref[0:1, lanes_of(g)]
            for c in cs:
                lg = _dot_nt(k_ref[rows_of(c), head_cols(g)], qs[g]) * (HEAD_DIM ** -0.5 * LOG2_E)
                lg = lg + jnp.concatenate([bias_ref[rows_of(c), :]] * GROUP, axis=1)
                logit_ref[rows_of(c), lanes_of(g)] = lg
                m = jnp.maximum(m, jnp.max(lg, axis=0, keepdims=True))
            stat_ref[0:1, lanes_of(g)] = m

    _for_chunks(n_chunks, logits_chunks)

    def pv_chunks(cs):
        for g in range(N_KV_HEADS):
            m = stat_ref[0:1, lanes_of(g)]
            den = stat_ref[1:2, lanes_of(g)]
            for c in cs:
                e = jnp.exp2(logit_ref[rows_of(c), lanes_of(g)] - m)
                acc_ref[g] += _dot(vt_ref[c, head_cols(g), :], e.astype(BF16))
                den = den + jnp.sum(e, axis=0, keepdims=True)
            stat_ref[1:2, lanes_of(g)] = den

    _for_chunks(n_chunks, pv_chunks)

    for g in range(N_KV_HEADS):
        o = acc_ref[g] / stat_ref[1:2, lanes_of(g)]
        for hh in range(GROUP):
            head = g * GROUP + hh
            att_ref[:, head * HEAD_DIM:(head + 1) * HEAD_DIM] = o[:, hh * t:(hh + 1) * t].T.astype(BF16)


def _dsa_prompt_call(q8, iq4, misct, ikb, kb, vt, batch, seq):
    t = ATTN_TILE
    n = batch * seq
    d = N_HEADS * HEAD_DIM
    nq = seq // t
    k_sel = min(TOPK_MAX, seq // 4)
    return pl.pallas_call(
        functools.partial(_dsa_prompt_kernel, k_sel=k_sel),
        grid=(batch, nq),
        in_specs=[
            pl.BlockSpec((N_HEADS, t, HEAD_DIM), lambda b, q: (0, b * nq + q, 0)),
            pl.BlockSpec((IDX_HEADS, t, IDX_DIM), lambda b, q: (0, b * nq + q, 0)),
            pl.BlockSpec((LANES, t), lambda b, q: (0, b * nq + q)),
            pl.BlockSpec((seq, IDX_DIM), lambda b, q: (b, 0)),
            pl.BlockSpec((seq, KV_W), lambda b, q: (b, 0)),
            pl.BlockSpec((nq, KV_W, t), lambda b, q: (b, 0, 0)),
        ],
        out_specs=pl.BlockSpec((t, d), lambda b, q: (b * nq + q, 0)),
        out_shape=jax.ShapeDtypeStruct((n, d), BF16),
        scratch_shapes=[
            pltpu.VMEM((seq, t), I32),
            pltpu.VMEM((seq, t), I16),
            pltpu.VMEM((seq, t), I16),
            pltpu.VMEM((seq, t), F32),
            pltpu.VMEM((seq, N_KV_HEADS * GROUP * t), F32),
            pltpu.VMEM((N_KV_HEADS, HEAD_DIM, GROUP * t), F32),
            pltpu.VMEM((SUBLANES, N_KV_HEADS * GROUP * t), F32),
        ],
        compiler_params=_params(grid_rank=2),
        name="dsa_prompt",
    )(q8, iq4, misct, ikb, kb, vt)


def _row_count(key_ref, indicator):
    r, l = key_ref.shape
    acc = jnp.zeros((r, LANES), I32)
    for c in range(l // LANES):
        acc = acc + indicator(key_ref[:, c * LANES:(c + 1) * LANES], c)
    return jnp.sum(acc, axis=-1, keepdims=True)


def _select_bias_rows(key_ref, bias_ref, jmax_ref, kq):
    r, l = key_ref.shape

    def ge_body(i, t):
        cand = t + lax.shift_left(jnp.int32(1), 31 - i)
        candb = jnp.broadcast_to(cand, (r, LANES))
        cnt = _row_count(key_ref, lambda k, c: jnp.where(k >= candb, 1, 0))
        return jnp.where(cnt >= kq, cand, t)

    thr = lax.fori_loop(0, 32, ge_body, jnp.full((r, 1), INT_MIN, I32))
    thrb = jnp.broadcast_to(thr, (r, LANES))
    cnt_gt = _row_count(key_ref, lambda k, c: jnp.where(k > thrb, 1, 0))
    cnt_ge = _row_count(key_ref, lambda k, c: jnp.where(k >= thrb, 1, 0))
    need = kq - cnt_gt

    jmax_ref[...] = jnp.full(jmax_ref.shape, l, I32)
    lane = lax.broadcasted_iota(I32, (r, LANES), 1)

    @pl.when(jnp.max(cnt_ge - kq) > 0)
    def _():
        nbits = max(1, (l - 1).bit_length())

        def lt_body(i, j):
            cand = j + lax.shift_left(jnp.int32(1), nbits - 1 - i)
            candb = jnp.broadcast_to(cand, (r, LANES))
            cnt = _row_count(key_ref, lambda k, c: jnp.where(k == thrb, jnp.where(lane + c * LANES < candb, 1, 0), 0))
            return jnp.where(cnt < need, cand, j)

        jmax_ref[...] = jnp.broadcast_to(lax.fori_loop(0, nbits, lt_body, jnp.zeros((r, 1), I32)), jmax_ref.shape)

    jmaxb = jmax_ref[...]
    for c in range(l // LANES):
        k = key_ref[:, c * LANES:(c + 1) * LANES]
        keep_tie = jnp.where(lane + c * LANES <= jmaxb, 0.0, NEG_INF)
        bias_ref[:, c * LANES:(c + 1) * LANES] = jnp.where(k > thrb, 0.0, jnp.where(k == thrb, keep_tie, NEG_INF))


SELECT_SEQS = 8
ATTEND_SEQS = 2


def _sample_select_kernel(pt_ref, iq4_ref, misc_ref, cikt_ref, bias_ref, iktbuf, iknew, key_ref, jmax_ref, sem,
                          *, k_sel, layer, ts):
    page = cikt_ref.shape[3]
    n_pages = pt_ref.shape[1]
    past = n_pages * page
    l_pad = past + LANES
    seqs = iktbuf.shape[0]
    step = pl.program_id(0)

    def page_copy(i, p):
        lanes = pl.ds(pl.multiple_of(p * page, page), page)
        return pltpu.make_async_copy(cikt_ref.at[layer, pt_ref[step * seqs + i, p]], iktbuf.at[i, :, lanes], sem.at[0])

    for i in range(seqs):
        lax.fori_loop(0, n_pages, lambda p, c, i=i: (page_copy(i, p).start(), c)[1], 0)

    iknew[...] = jnp.zeros(iknew.shape, F32)
    for i in range(seqs):
        iknew[i, 0:ts, :] = misc_ref[i * ts:(i + 1) * ts, :IDX_DIM]

    for i in range(seqs):
        lax.fori_loop(0, n_pages, lambda p, c, i=i: (page_copy(i, p).wait(), c)[1], 0)

    qpos = past + lax.broadcasted_iota(I32, (ts, 1), 0)
    kpos = lax.broadcasted_iota(I32, (ts, l_pad), 1)
    for i in range(seqs):
        rows = slice(i * ts, (i + 1) * ts)
        iq = iq4_ref[:, rows, :].reshape(IDX_HEADS * ts, IDX_DIM).astype(BF16)
        s = jnp.concatenate([_dot(iq, iktbuf[i].astype(BF16)), _dot_nt(iq, iknew[i].astype(BF16))], axis=1)
        iw = misc_ref[rows, IDX_DIM:IDX_DIM + IDX_HEADS]
        score = jnp.zeros((ts, l_pad), F32)
        for h in range(IDX_HEADS):
            score = score + jnp.maximum(s[h * ts:(h + 1) * ts], 0.0) * iw[:, h:h + 1]
        key_ref[rows, :] = _sortable_key(jnp.where(kpos <= qpos, score, NEG_INF))
    kq = jnp.minimum(k_sel, jnp.concatenate([qpos] * seqs, axis=0) + 1)
    _select_bias_rows(key_ref, bias_ref, jmax_ref, kq)


def _sample_attend_kernel(pt_ref, h_ref, q8_ref, kn_ref, vn_ref, bias_ref, ck_ref, cv_ref, wo_ref, o_ref,
                          kbuf, vbuf, knew, vnew, att_ref, sems, *, layer, ts):
    page_rows = ck_ref.shape[2]
    n_pages = pt_ref.shape[1]
    past = n_pages * page_rows // N_KV_HEADS
    seqs = kbuf.shape[0]
    step = pl.program_id(0)
    n_steps = pl.num_programs(0)

    def page_copies(i, p):
        src = pt_ref[step * seqs + i, p]
        rows = pl.ds(pl.multiple_of(p * page_rows, page_rows), page_rows)
        return (pltpu.make_async_copy(ck_ref.at[layer, src], kbuf.at[i, rows], sems.at[0, i]),
                pltpu.make_async_copy(cv_ref.at[layer, src], vbuf.at[i, rows], sems.at[1, i]))

    def start_page(i):
        def body(p, c):
            for cp in page_copies(i, p):
                cp.start()
            return c
        return body

    def wait_page(i):
        def body(p, c):
            for cp in page_copies(i, p):
                cp.wait()
            return c
        return body

    for i in range(seqs):
        lax.fori_loop(0, n_pages, start_page(i), 0)

    pad = jnp.zeros((LANES - ts, KV_W), F32)
    for i in range(seqs):
        rows = slice(i * ts, (i + 1) * ts)
        lax.fori_loop(0, n_pages, wait_page(i), 0)
        for g in range(N_KV_HEADS):
            new_rows = pl.ds(i * ts * N_KV_HEADS + g, ts, stride=N_KV_HEADS)
            knew[0:ts, g * HEAD_DIM:(g + 1) * HEAD_DIM] = kn_ref[new_rows, :]
            vnew[0:ts, g * HEAD_DIM:(g + 1) * HEAD_DIM] = vn_ref[new_rows, :]
        knew[ts:, :] = pad
        vnew[ts:, :] = pad
        bias = jnp.concatenate([bias_ref[rows, :]] * GROUP, axis=0)
        row0 = pl.multiple_of((step * seqs + i) * ts, ts)
        for g in range(N_KV_HEADS):
            cols = slice(g * HEAD_DIM, (g + 1) * HEAD_DIM)
            k_past = kbuf[i, pl.ds(g, past, stride=N_KV_HEADS), :].astype(BF16)
            v_past = vbuf[i, pl.ds(g, past, stride=N_KV_HEADS), :].astype(BF16)
            q = q8_ref[g * GROUP:(g + 1) * GROUP, rows, :].reshape(GROUP * ts, HEAD_DIM).astype(BF16)
            logits = jnp.concatenate([_dot_nt(q, k_past), _dot_nt(q, knew[:, cols].astype(BF16))], axis=1)
            logits = logits * (HEAD_DIM ** -0.5) + bias
            m = jnp.max(logits, axis=-1, keepdims=True)
            e = jnp.exp(logits - m)
            den = jnp.sum(e, axis=-1, keepdims=True)
            eb = e.astype(BF16)
            o = (_dot(eb[:, :past], v_past) + _dot(eb[:, past:], vnew[:, cols].astype(BF16))) / den
            for hh in range(GROUP):
                head = g * GROUP + hh
                att_ref[pl.ds(row0, ts), head * HEAD_DIM:(head + 1) * HEAD_DIM] = o[hh * ts:(hh + 1) * ts]

    @pl.when(step == n_steps - 1)
    def _():
        o_ref[...] = h_ref[...] + _dot(att_ref[...].astype(BF16), wo_ref[...])


def _dsa_sample_call(h, q8, iq4, misc, k_new, v_new, cache_k, cache_v, cache_ik, layer, page_table, wo, ts):
    n, d = h.shape
    nb = n // ts
    n_layers, n_pool, page = cache_k.shape[:3]
    n_pages = page_table.shape[1]
    past = n_pages * page
    l_pad = past + LANES
    k_sel = min(TOPK_MAX, (past + ts) // 4)
    ck = cache_k.reshape(n_layers, n_pool, page * N_KV_HEADS, HEAD_DIM)
    cv = cache_v.reshape(n_layers, n_pool, page * N_KV_HEADS, HEAD_DIM)
    cikt = jnp.swapaxes(cache_ik, 2, 3)
    sel_rows = SELECT_SEQS * ts
    bias = pl.pallas_call(
        functools.partial(_sample_select_kernel, k_sel=k_sel, layer=layer, ts=ts),
        grid_spec=pltpu.PrefetchScalarGridSpec(
            num_scalar_prefetch=1,
            grid=(nb // SELECT_SEQS,),
            in_specs=[
                pl.BlockSpec((IDX_HEADS, sel_rows, IDX_DIM), lambda s, pt: (0, s, 0)),
                pl.BlockSpec((sel_rows, LANES), lambda s, pt: (s, 0)),
                pl.BlockSpec(memory_space=pl.ANY),
            ],
            out_specs=pl.BlockSpec((sel_rows, l_pad), lambda s, pt: (s, 0)),
            scratch_shapes=[
                pltpu.VMEM((SELECT_SEQS, IDX_DIM, past), F32),
                pltpu.VMEM((SELECT_SEQS, LANES, IDX_DIM), F32),
                pltpu.VMEM((sel_rows, l_pad), I32),
                pltpu.VMEM((sel_rows, LANES), I32),
                pltpu.SemaphoreType.DMA((1,)),
            ],
        ),
        out_shape=jax.ShapeDtypeStruct((n, l_pad), F32),
        compiler_params=_params(),
        name="sample_select",
    )(page_table, iq4, misc, cikt)

    att_rows = ATTEND_SEQS * ts
    return pl.pallas_call(
        functools.partial(_sample_attend_kernel, layer=layer, ts=ts),
        grid_spec=pltpu.PrefetchScalarGridSpec(
            num_scalar_prefetch=1,
            grid=(nb // ATTEND_SEQS,),
            in_specs=[
                pl.BlockSpec((n, d), lambda s, pt: (0, 0)),
                pl.BlockSpec((N_HEADS, att_rows, HEAD_DIM), lambda s, pt: (0, s, 0)),
                pl.BlockSpec((att_rows * N_KV_HEADS, HEAD_DIM), lambda s, pt: (s, 0)),
                pl.BlockSpec((att_rows * N_KV_HEADS, HEAD_DIM), lambda s, pt: (s, 0)),
                pl.BlockSpec((att_rows, l_pad), lambda s, pt: (s, 0)),
                pl.BlockSpec(memory_space=pl.ANY),
                pl.BlockSpec(memory_space=pl.ANY),
                pl.BlockSpec((d, d), lambda s, pt: (0, 0)),
            ],
            out_specs=pl.BlockSpec((n, d), lambda s, pt: (0, 0)),
            scratch_shapes=[
                pltpu.VMEM((ATTEND_SEQS, past * N_KV_HEADS, HEAD_DIM), F32),
                pltpu.VMEM((ATTEND_SEQS, past * N_KV_HEADS, HEAD_DIM), F32),
                pltpu.VMEM((LANES, KV_W), F32),
                pltpu.VMEM((LANES, KV_W), F32),
                pltpu.VMEM((n, d), F32),
                pltpu.SemaphoreType.DMA((2, ATTEND_SEQS)),
            ],
        ),
        out_shape=jax.ShapeDtypeStruct((n, d), F32),
        compiler_params=_params(),
        name="sample_attend",
    )(page_table, h, q8, k_new, v_new, bias, ck, cv, wo)


def _sgu_kernel(h_ref, g_ref, win_ref, bin_ref, gv_ref, ws_ref, bs_ref, wout_ref, o_ref, *rest,
                seg, emit_v):
    if emit_v:
        v_ref, gated_ref = rest
    else:
        (gated_ref,) = rest
    tm, d = h_ref.shape
    c_len = ws_ref.shape[1]
    d_sg = gv_ref.shape[1]
    gw = d_sg // SG_GROUPS
    x = h_ref[...]
    xn = _rms(x, g_ref[...]).astype(BF16)
    z = _dot(xn, win_ref[...]) + bin_ref[...]
    z = 0.5 * z * (1.0 + lax.erf(z * (0.5 ** 0.5)))
    u = z[:, :d_sg]
    v = _rms(z[:, d_sg:], gv_ref[...])
    if emit_v:
        v_ref[...] = v
    vb = v.astype(BF16)
    row = lax.broadcasted_iota(I32, (c_len, c_len), 0)
    col = lax.broadcasted_iota(I32, (c_len, c_len), 1)
    same_seq = (row >> _log2(seg)) == (col >> _log2(seg))
    if seg != c_len:
        src = lax.broadcasted_iota(I32, (ws_ref.shape[2], c_len), 0)
        dst = lax.broadcasted_iota(I32, (ws_ref.shape[2], c_len), 1)
        spread = jnp.where((dst & (seg - 1)) == src, 1.0, 0.0).astype(BF16)
    for g in range(SG_GROUPS):
        wsg = ws_ref[g] if seg == c_len else _dot(ws_ref[g].astype(BF16), spread)
        wg = jnp.where(col <= row, jnp.where(same_seq, wsg, 0.0), 0.0).astype(BF16)
        bg = bs_ref[:, g:g + 1]
        for ch in range(tm // c_len):
            rows = slice(ch * c_len, (ch + 1) * c_len)
            cols = slice(g * gw, (g + 1) * gw)
            mixed = _dot(wg, vb[rows, cols]) + bg
            gated_ref[rows, cols] = (u[rows, cols] * mixed).astype(BF16)
    o_ref[...] = x + _dot(gated_ref[...], wout_ref[...])


def _sgu_call(h, g, win, b_in, gv, ws, bs_t, wout, tm, seg, emit_v):
    n, d = h.shape
    d2 = win.shape[1]
    d_sg = d2 // 2
    c_len = ws.shape[1]
    out_shape = [jax.ShapeDtypeStruct((n, d), F32)]
    out_specs = [pl.BlockSpec((tm, d), lambda i: (i, 0))]
    if emit_v:
        out_shape.append(jax.ShapeDtypeStruct((n, d_sg), F32))
        out_specs.append(pl.BlockSpec((tm, d_sg), lambda i: (i, 0)))
    res = pl.pallas_call(
        functools.partial(_sgu_kernel, seg=seg, emit_v=emit_v),
        grid=(n // tm,),
        in_specs=[
            pl.BlockSpec((tm, d), lambda i: (i, 0)),
            pl.BlockSpec((1, d), lambda i: (0, 0)),
            pl.BlockSpec((d, d2), lambda i: (0, 0)),
            pl.BlockSpec((1, d2), lambda i: (0, 0)),
            pl.BlockSpec((1, d_sg), lambda i: (0, 0)),
            pl.BlockSpec(ws.shape, lambda i: (0, 0, 0)),
            pl.BlockSpec((c_len, SG_GROUPS), lambda i: (0, 0)),
            pl.BlockSpec((d_sg, d), lambda i: (0, 0)),
        ],
        out_specs=tuple(out_specs),
        out_shape=tuple(out_shape),
        scratch_shapes=[pltpu.VMEM((tm, d_sg), BF16)],
        compiler_params=_params(),
        name="sgu",
    )(h, g.reshape(1, d), win, b_in.reshape(1, d2), gv.reshape(1, d_sg), ws, bs_t, wout)
    return res


FFN_CHUNK = 256


def _ffn_body(x, p_ref, gf_ref, wup_ref, cw_ref, cb_ref, wdn_ref, gp_ref, wple_ref, wgate_ref, gfin_ref,
              o_ref, act_ref, shifted, emit_up, final_norm):
    d_ff = wdn_ref.shape[0]
    xn = _rms(x, gf_ref[...]).astype(BF16)

    def conv(cols):
        up = _dot(xn, wup_ref[:, cols])
        m1, m2 = shifted(up, cols)
        emit_up(up, cols)
        return cb_ref[:, cols] + cw_ref[0:1, cols] * m2 + cw_ref[1:2, cols] * m1 + cw_ref[2:3, cols] * up

    for c in range(d_ff // FFN_CHUNK):
        gate = conv(slice(c * FFN_CHUNK, (c + 1) * FFN_CHUNK))
        val = conv(slice(d_ff + c * FFN_CHUNK, d_ff + (c + 1) * FFN_CHUNK))
        act_ref[:, c * FFN_CHUNK:(c + 1) * FFN_CHUNK] = (gate * _sigmoid(gate) * val).astype(BF16)
    h2 = x + _dot(act_ref[...], wdn_ref[...])
    gate = _sigmoid(_dot(_rms(h2, gp_ref[...]).astype(BF16), wgate_ref[...]))
    h3 = h2 + _dot(p_ref[...].astype(BF16), wple_ref[...]) * gate
    o_ref[...] = _rms(h3, gfin_ref[...]) if final_norm else h3


def _ffn_prompt_kernel(*refs, tiles_per_seq, final_norm, with_attn):
    if with_attn:
        att_ref, wo_ref, *refs = refs
    (h_ref, p_ref, gf_ref, wup_ref, cw_ref, cb_ref, wdn_ref, gp_ref, wple_ref, wgate_ref, gfin_ref,
     o_ref, tail_ref, carry_ref, act_ref) = refs
    tm = h_ref.shape[0]
    i = pl.program_id(0)

    @pl.when(i % tiles_per_seq == 0)
    def _():
        carry_ref[...] = jnp.zeros(carry_ref.shape, F32)

    row = lax.broadcasted_iota(I32, (tm, FFN_CHUNK), 0)

    def shifted(up, cols):
        prev = carry_ref[:, cols]
        p1 = prev[SUBLANES - 1:SUBLANES]
        p2 = prev[SUBLANES - 2:SUBLANES - 1]
        m1 = jnp.where(row >= 1, pltpu.roll(up, 1, 0), p1)
        m2 = jnp.where(row >= 2, pltpu.roll(up, 2, 0), jnp.where(row == 0, p2, p1))
        return m1, m2

    def emit_up(up, cols):
        last = up[tm - SUBLANES:tm]
        carry_ref[:, cols] = last
        tail_ref[0, :, cols] = last

    x = h_ref[...]
    if with_attn:
        x = x + _dot(att_ref[...], wo_ref[...])
    _ffn_body(x, p_ref, gf_ref, wup_ref, cw_ref, cb_ref, wdn_ref, gp_ref, wple_ref, wgate_ref, gfin_ref,
              o_ref, act_ref, shifted, emit_up, final_norm)


def _ffn_sample_kernel(h_ref, p_ref, pm1_ref, pm2_ref, gf_ref, wup_ref, cw_ref, cb_ref, wdn_ref, gp_ref, wple_ref,
                       wgate_ref, gfin_ref, o_ref, up_ref, act_ref, *, seg, final_norm):
    tm = h_ref.shape[0]
    row = lax.broadcasted_iota(I32, (tm, FFN_CHUNK), 0) & ((1 << _log2(seg)) - 1)

    def shifted(up, cols):
        m1 = jnp.where(row >= 1, pltpu.roll(up, 1, 0), pm1_ref[:, cols])
        m2 = jnp.where(row >= 2, pltpu.roll(up, 2, 0), pm2_ref[:, cols])
        return m1, m2

    def emit_up(up, cols):
        up_ref[:, cols] = up

    _ffn_body(h_ref[...], p_ref, gf_ref, wup_ref, cw_ref, cb_ref, wdn_ref, gp_ref, wple_ref, wgate_ref, gfin_ref,
              o_ref, act_ref, shifted, emit_up, final_norm)


def _ffn_weight_specs(d, f2, d_ff, ple, layer):
    full = lambda shape: pl.BlockSpec(shape, lambda i: (0,) * len(shape))
    of_layer = lambda shape: pl.BlockSpec((None,) + shape, lambda i: (layer, 0, 0), pipeline_mode=pl.Buffered(1))
    return [full((1, d)), of_layer((d, f2)), full((CONV_W, f2)), full((1, f2)), of_layer((d_ff, d)),
            full((1, d)), full((ple, d)), full((d, d)), full((1, d))]


def _ffn_prompt_call(h, att, wo, p_all, layer, weights, tm, tiles_per_seq, final_norm):
    n, d = h.shape
    gf, wup, cw, cb, wdn, gp, wple, wgate, gfin = weights
    f2 = wup.shape[2]
    d_ff = wdn.shape[1]
    ple = p_all.shape[2]
    nt = n // tm
    with_attn = att is not None
    attn_specs = [pl.BlockSpec((tm, d), lambda i: (i, 0)), pl.BlockSpec((d, d), lambda i: (0, 0))] if with_attn else []
    attn_args = (att, wo) if with_attn else ()
    return pl.pallas_call(
        functools.partial(_ffn_prompt_kernel, tiles_per_seq=tiles_per_seq, final_norm=final_norm,
                          with_attn=with_attn),
        grid=(nt,),
        in_specs=attn_specs
        + [pl.BlockSpec((tm, d), lambda i: (i, 0)), pl.BlockSpec((None, tm, ple), lambda i: (layer, i, 0))]
        + _ffn_weight_specs(d, f2, d_ff, ple, layer),
        out_specs=(pl.BlockSpec((tm, d), lambda i: (i, 0)), pl.BlockSpec((1, SUBLANES, f2), lambda i: (i, 0, 0))),
        out_shape=(jax.ShapeDtypeStruct((n, d), F32), jax.ShapeDtypeStruct((nt, SUBLANES, f2), F32)),
        scratch_shapes=[pltpu.VMEM((SUBLANES, f2), F32), pltpu.VMEM((tm, d_ff), BF16)],
        compiler_params=_params(),
        name="ffn_prompt",
    )(*attn_args, h, p_all, gf, wup, cw, cb, wdn, gp, wple, wgate, gfin)


def _ffn_sample_call(h, p, pm1, pm2, layer, weights, seg, final_norm):
    n, d = h.shape
    gf, wup, cw, cb, wdn, gp, wple, wgate, gfin = weights
    f2 = wup.shape[2]
    d_ff = wdn.shape[1]
    ple = p.shape[1]
    full = lambda shape: pl.BlockSpec(shape, lambda i: (0,) * len(shape))
    return pl.pallas_call(
        functools.partial(_ffn_sample_kernel, seg=seg, final_norm=final_norm),
        grid=(1,),
        in_specs=[full((n, d)), full((n, ple)), full((n, f2)), full((n, f2))] + _ffn_weight_specs(d, f2, d_ff, ple, layer),
        out_specs=(full((n, d)), full((n, f2))),
        out_shape=(jax.ShapeDtypeStruct((n, d), F32), jax.ShapeDtypeStruct((n, f2), F32)),
        scratch_shapes=[pltpu.VMEM((n, d_ff), BF16)],
        compiler_params=_params(),
        name="ffn_sample",
    )(h, p, pm1, pm2, gf, wup, cw, cb, wdn, gp, wple, wgate, gfin)


PROMPT_TILE = 512


def kernel(x_prompt, x_sample, cache_k, cache_v, cache_idx_k, state_conv, page_table, p_prompt, p_sample,
           norm_mix, w_attn_in, w_attn_out, w_sg_in, b_sg_in, norm_sg_v, w_sg_spatial, b_sg_spatial, w_sg_out,
           norm_ffn, w_ffn_up, w_ffn_conv, b_ffn_conv, w_ffn_down, norm_ple, w_ple, w_ple_gate, norm_final):
    batch, seq, d = x_prompt.shape
    nb, ts, _ = x_sample.shape
    depth = norm_mix.shape[0]
    page = cache_k.shape[2]
    past = page_table.shape[1] * page
    f2 = w_ffn_up.shape[2]
    n_s = nb * ts
    tm = PROMPT_TILE

    hp = x_prompt.reshape(batch * seq, d)
    hs = x_sample.reshape(n_s, d)
    tabs_p = _rope_tables(jnp.arange(seq))
    tabs_s = tuple(jnp.tile(t, (nb, 1)) for t in _rope_tables(past + jnp.arange(ts)))
    wup_all = w_ffn_up.astype(BF16)
    wdn_all = w_ffn_down.astype(BF16)

    kp_l, vp_l, ikp_l, ks_l, vs_l, iks_l, cp_l, cs_l, sgv_l = [], [], [], [], [], [], [], [], []
    for i in range(depth):
        j = i // 2
        att = w_out = None
        if i % 2 == 0:
            w_in = jnp.pad(w_attn_in[j], ((0, 0), (0, PROJ_W - w_attn_in.shape[2]))).astype(BF16)
            w_out = w_attn_out[j].astype(BF16)
            q8, k, v, kb, vt, iq4, ikt, misct, ikb = _proj_call(hp, norm_mix[i], w_in, tabs_p, tm, True)
            att = _dsa_prompt_call(q8, iq4, misct, ikb, kb, vt, batch, seq)
            kp_l.append(k.reshape(batch, seq, N_KV_HEADS, HEAD_DIM))
            vp_l.append(v.reshape(batch, seq, N_KV_HEADS, HEAD_DIM))
            ikp_l.append(jnp.swapaxes(ikt, 1, 2))
            q8, k, v, iq4, misc = _proj_call(hs, norm_mix[i], w_in, tabs_s, n_s, False)
            hs = _dsa_sample_call(hs, q8, iq4, misc, k, v, cache_k, cache_v, cache_idx_k, j,
                                  page_table, w_out, ts)
            ks_l.append(k.reshape(nb, ts, N_KV_HEADS, HEAD_DIM))
            vs_l.append(v.reshape(nb, ts, N_KV_HEADS, HEAD_DIM))
            iks_l.append(misc[:, :IDX_DIM].reshape(nb, ts, IDX_DIM))
        else:
            win = w_sg_in[j].astype(BF16)
            wout = w_sg_out[j].astype(BF16)
            (hp,) = _sgu_call(hp, norm_mix[i], win, b_sg_in[j], norm_sg_v[j], w_sg_spatial[j],
                              b_sg_spatial[j].T, wout, tm, SG_CHUNK, False)
            ws_s = jnp.tile(jnp.pad(w_sg_spatial[j][:, :ts, :ts], ((0, 0), (0, 0), (0, LANES - ts))), (1, nb, 1))
            bs_s = jnp.tile(b_sg_spatial[j].T[:ts], (nb, 1))
            hs, v_rows = _sgu_call(hs, norm_mix[i], win, b_sg_in[j], norm_sg_v[j], ws_s, bs_s, wout, n_s, ts, True)
            sgv_l.append(v_rows.reshape(nb, ts, -1))

        final = i == depth - 1
        weights = (norm_ffn[i].reshape(1, d), wup_all, w_ffn_conv[i], b_ffn_conv[i].reshape(1, f2),
                   wdn_all, norm_ple[i].reshape(1, d), w_ple[i].astype(BF16),
                   w_ple_gate[i].astype(BF16), norm_final.reshape(1, d))
        hp, tail = _ffn_prompt_call(hp, att, w_out, p_prompt.reshape(depth, batch * seq, -1), i, weights,
                                    tm, seq // tm, final)
        cp_l.append(tail.reshape(batch, seq // tm, SUBLANES, f2)[:, -1, SUBLANES - (CONV_W - 1):])
        st = state_conv[i]
        zeros = jnp.zeros((nb, ts - 1, f2), F32)
        pm1 = jnp.concatenate([st[:, 1:2], zeros], axis=1).reshape(n_s, f2)
        pm2 = jnp.concatenate([st, zeros[:, 1:]], axis=1).reshape(n_s, f2)
        hs, up_s = _ffn_sample_call(hs, p_sample[i].reshape(n_s, -1), pm1, pm2, i, weights, ts, final)
        cs_l.append(up_s.reshape(nb, ts, f2)[:, ts - (CONV_W - 1):])

    return (hp.reshape(batch, seq, d), hs.reshape(nb, ts, d),
            jnp.stack(kp_l), jnp.stack(vp_l), jnp.stack(ikp_l),
            jnp.stack(ks_l), jnp.stack(vs_l), jnp.stack(iks_l),
            jnp.stack(cp_l), jnp.stack(cs_l), jnp.stack(sgv_l))
```

```python
import functools

import jax
import jax.numpy as jnp
from jax import lax
from jax.experimental import pallas as pl
from jax.experimental.pallas import tpu as pltpu

F32 = jnp.float32
BF16 = jnp.bfloat16
I32 = jnp.int32

N_HEADS = 8
N_KV_HEADS = 2
GROUP = N_HEADS // N_KV_HEADS
HEAD_DIM = 128
IDX_HEADS = 4
IDX_DIM = 64
IDX_W_SCALE = (IDX_HEADS * IDX_DIM) ** -0.5
TOPK_MAX = 256
SG_CHUNK = 128
SG_GROUPS = 8
CONV_W = 3
ROPE_THETA = 10000.0
EPS = 1e-6

LANES = 128
SUBLANES = 8
V7X_VMEM_BYTES = 64 * 1024 * 1024
VMEM_LIMIT_BYTES = V7X_VMEM_BYTES - 8 * 1024 * 1024

NEG_INF = float("-inf")
LOG2_E = 1.4426950408889634
INT_MIN = -(2 ** 31)


def _params(grid_rank=1):
    return pltpu.CompilerParams(dimension_semantics=("arbitrary",) * grid_rank,
                                vmem_limit_bytes=VMEM_LIMIT_BYTES)


def _log2(n):
    assert n > 0 and n & (n - 1) == 0, n
    return n.bit_length() - 1


def _rms(x, g):
    ms = jnp.mean(x * x, axis=-1, keepdims=True)
    return x * lax.rsqrt(ms + EPS) * g


def _sigmoid(x):
    return 1.0 / (1.0 + jnp.exp(-x))


def _dot(a, b):
    return jnp.dot(a, b, preferred_element_type=F32)


def _dot_nt(a, b):
    return lax.dot_general(a, b, (((1,), (1,)), ((), ())), preferred_element_type=F32)


def _sortable_key(score):
    b = pltpu.bitcast(score, I32)
    b = jnp.where(b == INT_MIN, 0, b)
    return b ^ ((b >> 31) & 0x7FFFFFFF)


Q_OFF = 0
K_OFF = N_HEADS * HEAD_DIM
V_OFF = K_OFF + N_KV_HEADS * HEAD_DIM
IQ_OFF = V_OFF + N_KV_HEADS * HEAD_DIM
MISC_OFF = IQ_OFF + IDX_HEADS * IDX_DIM
PROJ_W = MISC_OFF + LANES
KV_W = N_KV_HEADS * HEAD_DIM
ATTN_TILE = 256


def _project(h_ref, g_ref, w_ref, tab_refs, q8_ref, k_ref, v_ref, iq4_ref):
    c128_ref, s128_ref, c64_ref, s64_ref, cm_ref, sm_ref = tab_refs
    tm = h_ref.shape[0]
    xn = _rms(h_ref[...], g_ref[...]).astype(BF16)
    y = _dot(xn, w_ref[...])
    c128 = c128_ref[...]
    s128 = s128_ref[...]

    def rope128(t):
        return t * c128 + pltpu.roll(t, HEAD_DIM // 2, 1) * s128

    for h in range(N_HEADS):
        q8_ref[h] = rope128(y[:, Q_OFF + h * HEAD_DIM:Q_OFF + (h + 1) * HEAD_DIM]).astype(q8_ref.dtype)
    ks = []
    for h in range(N_KV_HEADS):
        kh = rope128(y[:, K_OFF + h * HEAD_DIM:K_OFF + (h + 1) * HEAD_DIM])
        k_ref[pl.ds(h, tm, stride=N_KV_HEADS), :] = kh
        v_ref[pl.ds(h, tm, stride=N_KV_HEADS), :] = y[:, V_OFF + h * HEAD_DIM:V_OFF + (h + 1) * HEAD_DIM]
        ks.append(kh)
    v = y[:, V_OFF:IQ_OFF]

    lane = lax.broadcasted_iota(I32, (tm, LANES), 1)
    low_half = (lane & (IDX_DIM - 1)) < (IDX_DIM // 2)

    def rope64(t, c, s):
        partner = jnp.where(low_half, pltpu.roll(t, LANES - IDX_DIM // 2, 1), pltpu.roll(t, IDX_DIM // 2, 1))
        return t * c + partner * s

    c64 = c64_ref[...]
    s64 = s64_ref[...]
    for pair in range(IDX_HEADS // 2):
        t = rope64(y[:, IQ_OFF + pair * LANES:IQ_OFF + (pair + 1) * LANES], c64, s64)
        iq4_ref[2 * pair] = t[:, :IDX_DIM].astype(iq4_ref.dtype)
        iq4_ref[2 * pair + 1] = t[:, IDX_DIM:].astype(iq4_ref.dtype)
    m = rope64(y[:, MISC_OFF:MISC_OFF + LANES], cm_ref[...], sm_ref[...])
    return ks, v, m


def _proj_prompt_kernel(h_ref, g_ref, w_ref, c128_ref, s128_ref, c64_ref, s64_ref, cm_ref, sm_ref,
                        q8_ref, k_ref, v_ref, kb_ref, vt_ref, iq4_ref, ikt_ref, misct_ref, ikb_ref):
    tm = h_ref.shape[0]
    ks, v, m = _project(h_ref, g_ref, w_ref, (c128_ref, s128_ref, c64_ref, s64_ref, cm_ref, sm_ref),
                        q8_ref, k_ref, v_ref, iq4_ref)
    for h in range(N_KV_HEADS):
        kb_ref[:, h * HEAD_DIM:(h + 1) * HEAD_DIM] = ks[h].astype(BF16)
    for c in range(tm // ATTN_TILE):
        vt_ref[c] = v[c * ATTN_TILE:(c + 1) * ATTN_TILE].T.astype(BF16)
    mt = m.T
    misct_ref[...] = mt
    ikt_ref[0] = mt[:IDX_DIM, :]
    ikb_ref[...] = m[:, :IDX_DIM].astype(BF16)


def _proj_sample_kernel(h_ref, g_ref, w_ref, c128_ref, s128_ref, c64_ref, s64_ref, cm_ref, sm_ref,
                        q8_ref, k_ref, v_ref, iq4_ref, misc_ref):
    _, _, m = _project(h_ref, g_ref, w_ref, (c128_ref, s128_ref, c64_ref, s64_ref, cm_ref, sm_ref),
                       q8_ref, k_ref, v_ref, iq4_ref)
    misc_ref[...] = m


def _proj_call(h, g, w, tabs, tm, prompt):
    n, d = h.shape
    t_tab = tabs[0].shape[0]
    n_tab = t_tab // tm
    tab_spec = pl.BlockSpec((tm, LANES), lambda i: (i % n_tab, 0))
    row = lambda w_: pl.BlockSpec((tm, w_), lambda i: (i, 0))
    qdtype = BF16 if prompt else F32
    q8 = (jax.ShapeDtypeStruct((N_HEADS, n, HEAD_DIM), qdtype), pl.BlockSpec((N_HEADS, tm, HEAD_DIM), lambda i: (0, i, 0)))
    iq4 = (jax.ShapeDtypeStruct((IDX_HEADS, n, IDX_DIM), qdtype), pl.BlockSpec((IDX_HEADS, tm, IDX_DIM), lambda i: (0, i, 0)))
    kf = (jax.ShapeDtypeStruct((n * N_KV_HEADS, HEAD_DIM), F32),
          pl.BlockSpec((tm * N_KV_HEADS, HEAD_DIM), lambda i: (i, 0)))
    if prompt:
        assert tm % ATTN_TILE == 0
        outs = [q8, kf, kf, (jax.ShapeDtypeStruct((n, KV_W), BF16), row(KV_W)),
                (jax.ShapeDtypeStruct((n // ATTN_TILE, KV_W, ATTN_TILE), BF16),
                 pl.BlockSpec((tm // ATTN_TILE, KV_W, ATTN_TILE), lambda i: (i, 0, 0))),
                iq4,
                (jax.ShapeDtypeStruct((n // t_tab, IDX_DIM, t_tab), F32),
                 pl.BlockSpec((1, IDX_DIM, tm), lambda i: (i // n_tab, 0, i % n_tab))),
                (jax.ShapeDtypeStruct((LANES, n), F32), pl.BlockSpec((LANES, tm), lambda i: (0, i))),
                (jax.ShapeDtypeStruct((n, IDX_DIM), BF16), row(IDX_DIM))]
        body = _proj_prompt_kernel
    else:
        outs = [q8, kf, kf, iq4, (jax.ShapeDtypeStruct((n, LANES), F32), row(LANES))]
        body = _proj_sample_kernel
    return pl.pallas_call(
        body,
        grid=(n // tm,),
        in_specs=[
            pl.BlockSpec((tm, d), lambda i: (i, 0)),
            pl.BlockSpec((1, d), lambda i: (0, 0)),
            pl.BlockSpec((d, PROJ_W), lambda i: (0, 0)),
        ] + [tab_spec] * 6,
        out_specs=tuple(o[1] for o in outs),
        out_shape=tuple(o[0] for o in outs),
        compiler_params=_params(),
        name="attn_proj_prompt" if prompt else "attn_proj_sample",
    )(h, g.reshape(1, d), w, *tabs)


def _rope_tables(pos):
    t = pos.shape[0]

    def tab(half):
        inv = ROPE_THETA ** (-jnp.arange(half, dtype=F32) / half)
        ang = pos.astype(F32)[:, None] * inv[None, :]
        return jnp.cos(ang), jnp.sin(ang)

    c64, s64 = tab(HEAD_DIM // 2)
    c32, s32 = tab(IDX_DIM // 2)
    c128 = jnp.concatenate([c64, c64], axis=1)
    s128 = jnp.concatenate([-s64, s64], axis=1)
    cq = jnp.concatenate([c32, c32, c32, c32], axis=1)
    sq = jnp.concatenate([-s32, s32, -s32, s32], axis=1)
    cm = jnp.concatenate([c32, c32, jnp.full((t, LANES - IDX_DIM), IDX_W_SCALE, F32)], axis=1)
    sm = jnp.concatenate([-s32, s32, jnp.zeros((t, LANES - IDX_DIM), F32)], axis=1)
    return (c128, s128, cq, sq, cm, sm)


COUNT_SLAB = 64


def _col_count(key_ref, n_rows, indicator):
    w = key_ref.shape[1]
    acc = jnp.zeros((COUNT_SLAB, w), I32)
    for c in range(n_rows // COUNT_SLAB):
        acc = acc + indicator(key_ref[c * COUNT_SLAB:(c + 1) * COUNT_SLAB, :], c * COUNT_SLAB)
    return jnp.sum(acc, axis=0, keepdims=True)


I16 = jnp.int16
I16_MIN = -(2 ** 15)
HALF_SLAB = 128
PACK_ROWS = 16


def _bcast16(x, rows):
    one = jnp.broadcast_to(x, (PACK_ROWS, x.shape[1])).astype(I16)
    return jnp.concatenate([one] * (rows // PACK_ROWS), axis=0)


def _col_count16(half_ref, n_rows, indicator):
    w = half_ref.shape[1]
    acc = jnp.zeros((HALF_SLAB, w), I16)
    for c in range(n_rows // HALF_SLAB):
        acc = acc + indicator(half_ref[c * HALF_SLAB:(c + 1) * HALF_SLAB, :])
    return jnp.sum(acc.astype(I32), axis=0, keepdims=True)


def _kth_largest16(half_ref, n_rows, kk):
    w = half_ref.shape[1]
    one = jnp.ones((HALF_SLAB, w), I16)
    zero = jnp.zeros((HALF_SLAB, w), I16)

    def body(i, t):
        cand = t + lax.shift_left(jnp.int32(1), 15 - i)
        candb = _bcast16(cand, HALF_SLAB)
        cnt = _col_count16(half_ref, n_rows, lambda k: jnp.where(k >= candb, one, zero))
        return jnp.where(cnt >= kk, cand, t)

    return lax.fori_loop(0, 16, body, jnp.full((1, w), I16_MIN, I32))


def _select_bias_cols(key_ref, hi_ref, lo_ref, bias_ref, n_rows, kq):
    w = key_ref.shape[1]
    one = jnp.ones((HALF_SLAB, w), I16)
    zero = jnp.zeros((HALF_SLAB, w), I16)

    t_hi = _kth_largest16(hi_ref, n_rows, kq)
    t_hib = _bcast16(t_hi, HALF_SLAB)
    above = _col_count16(hi_ref, n_rows, lambda k: jnp.where(k > t_hib, one, zero))
    lowest = jnp.full((HALF_SLAB, w), I16_MIN, I16)
    for c in range(n_rows // HALF_SLAB):
        rows = slice(c * HALF_SLAB, (c + 1) * HALF_SLAB)
        lo_ref[rows, :] = jnp.where(hi_ref[rows, :] == t_hib, lo_ref[rows, :], lowest)
    t_lo = _kth_largest16(lo_ref, n_rows, kq - above)
    thr = t_hi * 65536 + (t_lo - I16_MIN)

    cnt_gt = _col_count(key_ref, n_rows, lambda k, r0: jnp.where(k > thr, 1, 0))
    cnt_ge = _col_count(key_ref, n_rows, lambda k, r0: jnp.where(k >= thr, 1, 0))
    need = kq - cnt_gt

    drop_ties = jnp.max(cnt_ge - kq) > 0

    @pl.when(jnp.logical_not(drop_ties))
    def _():
        for c in range(n_rows // COUNT_SLAB):
            rows = slice(c * COUNT_SLAB, (c + 1) * COUNT_SLAB)
            bias_ref[rows, :] = jnp.where(key_ref[rows, :] >= thr, 0.0, NEG_INF)

    @pl.when(drop_ties)
    def _():
        nbits = max(1, (n_rows - 1).bit_length())
        sub = lax.broadcasted_iota(I32, (COUNT_SLAB, w), 0)

        def lt_body(i, j):
            cand = j + lax.shift_left(jnp.int32(1), nbits - 1 - i)
            cnt = _col_count(key_ref, n_rows,
                             lambda k, r0: jnp.where(k == thr, jnp.where(sub + r0 < cand, 1, 0), 0))
            return jnp.where(cnt < need, cand, j)

        jmax = lax.fori_loop(0, nbits, lt_body, jnp.zeros((1, w), I32))
        for c in range(n_rows // COUNT_SLAB):
            rows = slice(c * COUNT_SLAB, (c + 1) * COUNT_SLAB)
            k = key_ref[rows, :]
            keep_tie = jnp.where(sub + c * COUNT_SLAB <= jmax, 0.0, NEG_INF)
            bias_ref[rows, :] = jnp.where(k > thr, 0.0, jnp.where(k == thr, keep_tie, NEG_INF))


def _for_chunks(n, body):
    def pair(p, carry):
        body([2 * p, 2 * p + 1])
        return carry

    lax.fori_loop(0, lax.shift_right_logical(n, 1), pair, 0)

    @pl.when((n & 1) == 1)
    def _():
        body([n - 1])


def _dsa_prompt_kernel(q8_ref, iq4_ref, misct_ref, ik_ref, k_ref, vt_ref, att_ref,
                       key_ref, hi_ref, lo_ref, bias_ref, logit_ref, acc_ref, stat_ref, *, k_sel):
    t = q8_ref.shape[1]
    nq = key_ref.shape[0] // t
    gw = GROUP * t
    qi = pl.program_id(1)
    n_chunks = qi + 1
    qpos = qi * t + lax.broadcasted_iota(I32, (1, t), 1)
    rows_of = lambda c: pl.ds(pl.multiple_of(c * t, t), t)

    iq = iq4_ref[...].reshape(IDX_HEADS * t, IDX_DIM)
    iw = misct_ref[IDX_DIM:IDX_DIM + SUBLANES, :]

    def score_chunks(cs):
        for c in cs:
            s = _dot_nt(ik_ref[rows_of(c), :], iq)
            score = jnp.zeros((t, t), F32)
            for h in range(IDX_HEADS):
                score = score + jnp.maximum(s[:, h * t:(h + 1) * t], 0.0) * iw[h:h + 1, :]
            kpos = c * t + lax.broadcasted_iota(I32, (t, t), 0)
            key = _sortable_key(jnp.where(kpos <= qpos, score, NEG_INF))
            key_ref[rows_of(c), :] = key
            hi_ref[rows_of(c), :] = (key >> 16).astype(I16)
            lo_ref[rows_of(c), :] = ((key & 0xFFFF) + I16_MIN).astype(I16)

    _for_chunks(n_chunks, score_chunks)

    kq = jnp.minimum(k_sel, qpos + 1)
    for j in range(nq):
        @pl.when(qi == j)
        def _():
            _select_bias_cols(key_ref, hi_ref, lo_ref, bias_ref, (j + 1) * t, kq)

    qs = [q8_ref[g * GROUP:(g + 1) * GROUP].reshape(gw, HEAD_DIM) for g in range(N_KV_HEADS)]
    head_cols = lambda g: slice(g * HEAD_DIM, (g + 1) * HEAD_DIM)
    lanes_of = lambda g: slice(g * gw, (g + 1) * gw)
    stat_ref[0:1, :] = jnp.full((1, N_KV_HEADS * gw), NEG_INF, F32)
    stat_ref[1:2, :] = jnp.zeros((1, N_KV_HEADS * gw), F32)
    acc_ref[...] = jnp.zeros(acc_ref.shape, F32)

    def logits_chunks(cs):
        for g in range(N_KV_HEADS):
            m = stat_ref[0:1, lanes_of(g)]
            for c in cs:
                lg = _dot_nt(k_ref[rows_of(c), head_cols(g)], qs[g]) * (HEAD_DIM ** -0.5 * LOG2_E)
                lg = lg + jnp.concatenate([bias_ref[rows_of(c), :]] * GROUP, axis=1)
                logit_ref[rows_of(c), lanes_of(g)] = lg
                m = jnp.maximum(m, jnp.max(lg, axis=0, keepdims=True))
            stat_ref[0:1, lanes_of(g)] = m

    _for_chunks(n_chunks, logits_chunks)

    def pv_chunks(cs):
        for g in range(N_KV_HEADS):
            m = stat_ref[0:1, lanes_of(g)]
            den = stat_ref[1:2, lanes_of(g)]
            for c in cs:
                e = jnp.exp2(logit_ref[rows_of(c), lanes_of(g)] - m)
                acc_ref[g] += _dot(vt_ref[c, head_cols(g), :], e.astype(BF16))
                den = den + jnp.sum(e, axis=0, keepdims=True)
            stat_ref[1:2, lanes_of(g)] = den

    _for_chunks(n_chunks, pv_chunks)

    for g in range(N_KV_HEADS):
        o = acc_ref[g] / stat_ref[1:2, lanes_of(g)]
        for hh in range(GROUP):
            head = g * GROUP + hh
            att_ref[:, head * HEAD_DIM:(head + 1) * HEAD_DIM] = o[:, hh * t:(hh + 1) * t].T.astype(BF16)


def _dsa_prompt_call(q8, iq4, misct, ikb, kb, vt, batch, seq):
    t = ATTN_TILE
    n = batch * seq
    d = N_HEADS * HEAD_DIM
    nq = seq // t
    k_sel = min(TOPK_MAX, seq // 4)
    return pl.pallas_call(
        functools.partial(_dsa_prompt_kernel, k_sel=k_sel),
        grid=(batch, nq),
        in_specs=[
            pl.BlockSpec((N_HEADS, t, HEAD_DIM), lambda b, q: (0, b * nq + q, 0)),
            pl.BlockSpec((IDX_HEADS, t, IDX_DIM), lambda b, q: (0, b * nq + q, 0)),
            pl.BlockSpec((LANES, t), lambda b, q: (0, b * nq + q)),
            pl.BlockSpec((seq, IDX_DIM), lambda b, q: (b, 0)),
            pl.BlockSpec((seq, KV_W), lambda b, q: (b, 0)),
            pl.BlockSpec((nq, KV_W, t), lambda b, q: (b, 0, 0)),
        ],
        out_specs=pl.BlockSpec((t, d), lambda b, q: (b * nq + q, 0)),
        out_shape=jax.ShapeDtypeStruct((n, d), BF16),
        scratch_shapes=[
            pltpu.VMEM((seq, t), I32),
            pltpu.VMEM((seq, t), I16),
            pltpu.VMEM((seq, t), I16),
            pltpu.VMEM((seq, t), F32),
            pltpu.VMEM((seq, N_KV_HEADS * GROUP * t), F32),
            pltpu.VMEM((N_KV_HEADS, HEAD_DIM, GROUP * t), F32),
            pltpu.VMEM((SUBLANES, N_KV_HEADS * GROUP * t), F32),
        ],
        compiler_params=_params(grid_rank=2),
        name="dsa_prompt",
    )(q8, iq4, misct, ikb, kb, vt)


def _row_count(key_ref, indicator):
    r, l = key_ref.shape
    acc = jnp.zeros((r, LANES), I32)
    for c in range(l // LANES):
        acc = acc + indicator(key_ref[:, c * LANES:(c + 1) * LANES], c)
    return jnp.sum(acc, axis=-1, keepdims=True)


def _select_bias_rows(key_ref, bias_ref, jmax_ref, kq):
    r, l = key_ref.shape

    def ge_body(i, t):
        cand = t + lax.shift_left(jnp.int32(1), 31 - i)
        candb = jnp.broadcast_to(cand, (r, LANES))
        cnt = _row_count(key_ref, lambda k, c: jnp.where(k >= candb, 1, 0))
        return jnp.where(cnt >= kq, cand, t)

    thr = lax.fori_loop(0, 32, ge_body, jnp.full((r, 1), INT_MIN, I32))
    thrb = jnp.broadcast_to(thr, (r, LANES))
    cnt_gt = _row_count(key_ref, lambda k, c: jnp.where(k > thrb, 1, 0))
    cnt_ge = _row_count(key_ref, lambda k, c: jnp.where(k >= thrb, 1, 0))
    need = kq - cnt_gt

    jmax_ref[...] = jnp.full(jmax_ref.shape, l, I32)
    lane = lax.broadcasted_iota(I32, (r, LANES), 1)

    @pl.when(jnp.max(cnt_ge - kq) > 0)
    def _():
        nbits = max(1, (l - 1).bit_length())

        def lt_body(i, j):
            cand = j + lax.shift_left(jnp.int32(1), nbits - 1 - i)
            candb = jnp.broadcast_to(cand, (r, LANES))
            cnt = _row_count(key_ref, lambda k, c: jnp.where(k == thrb, jnp.where(lane + c * LANES < candb, 1, 0), 0))
            return jnp.where(cnt < need, cand, j)

        jmax_ref[...] = jnp.broadcast_to(lax.fori_loop(0, nbits, lt_body, jnp.zeros((r, 1), I32)), jmax_ref.shape)

    jmaxb = jmax_ref[...]
    for c in range(l // LANES):
        k = key_ref[:, c * LANES:(c + 1) * LANES]
        keep_tie = jnp.where(lane + c * LANES <= jmaxb, 0.0, NEG_INF)
        bias_ref[:, c * LANES:(c + 1) * LANES] = jnp.where(k > thrb, 0.0, jnp.where(k == thrb, keep_tie, NEG_INF))


SELECT_SEQS = 8
ATTEND_SEQS = 2


def _sample_select_kernel(pt_ref, iq4_ref, misc_ref, cikt_ref, bias_ref, iktbuf, iknew, key_ref, jmax_ref, sem,
                          *, k_sel, layer, ts):
    page = cikt_ref.shape[3]
    n_pages = pt_ref.shape[1]
    past = n_pages * page
    l_pad = past + LANES
    seqs = iktbuf.shape[0]
    step = pl.program_id(0)

    def page_copy(i, p):
        lanes = pl.ds(pl.multiple_of(p * page, page), page)
        return pltpu.make_async_copy(cikt_ref.at[layer, pt_ref[step * seqs + i, p]], iktbuf.at[i, :, lanes], sem.at[0])

    for i in range(seqs):
        lax.fori_loop(0, n_pages, lambda p, c, i=i: (page_copy(i, p).start(), c)[1], 0)

    iknew[...] = jnp.zeros(iknew.shape, F32)
    for i in range(seqs):
        iknew[i, 0:ts, :] = misc_ref[i * ts:(i + 1) * ts, :IDX_DIM]

    for i in range(seqs):
        lax.fori_loop(0, n_pages, lambda p, c, i=i: (page_copy(i, p).wait(), c)[1], 0)

    qpos = past + lax.broadcasted_iota(I32, (ts, 1), 0)
    kpos = lax.broadcasted_iota(I32, (ts, l_pad), 1)
    for i in range(seqs):
        rows = slice(i * ts, (i + 1) * ts)
        iq = iq4_ref[:, rows, :].reshape(IDX_HEADS * ts, IDX_DIM).astype(BF16)
        s = jnp.concatenate([_dot(iq, iktbuf[i].astype(BF16)), _dot_nt(iq, iknew[i].astype(BF16))], axis=1)
        iw = misc_ref[rows, IDX_DIM:IDX_DIM + IDX_HEADS]
        score = jnp.zeros((ts, l_pad), F32)
        for h in range(IDX_HEADS):
            score = score + jnp.maximum(s[h * ts:(h + 1) * ts], 0.0) * iw[:, h:h + 1]
        key_ref[rows, :] = _sortable_key(jnp.where(kpos <= qpos, score, NEG_INF))
    kq = jnp.minimum(k_sel, jnp.concatenate([qpos] * seqs, axis=0) + 1)
    _select_bias_rows(key_ref, bias_ref, jmax_ref, kq)


def _sample_attend_kernel(pt_ref, h_ref, q8_ref, kn_ref, vn_ref, bias_ref, ck_ref, cv_ref, wo_ref, o_ref,
                          kbuf, vbuf, knew, vnew, att_ref, sems, *, layer, ts):
    page_rows = ck_ref.shape[2]
    n_pages = pt_ref.shape[1]
    past = n_pages * page_rows // N_KV_HEADS
    seqs = kbuf.shape[0]
    step = pl.program_id(0)
    n_steps = pl.num_programs(0)

    first = step * seqs

    def page_copies(seq, slot, p):
        src = pt_ref[seq, p]
        rows = pl.ds(pl.multiple_of(p * page_rows, page_rows), page_rows)
        return (pltpu.make_async_copy(ck_ref.at[layer, src], kbuf.at[slot, rows], sems.at[0, slot]),
                pltpu.make_async_copy(cv_ref.at[layer, src], vbuf.at[slot, rows], sems.at[1, slot]))

    def start_seq(seq, slot):
        def body(p, c):
            for cp in page_copies(seq, slot, p):
                cp.start()
            return c
        lax.fori_loop(0, n_pages, body, 0)

    def wait_seq(seq, slot):
        def body(p, c):
            for cp in page_copies(seq, slot, p):
                cp.wait()
            return c
        lax.fori_loop(0, n_pages, body, 0)

    @pl.when(step == 0)
    def _():
        start_seq(first, 0)

    pad = jnp.zeros((LANES - ts, KV_W), F32)
    for i in range(seqs):
        rows = slice(i * ts, (i + 1) * ts)
        if i + 1 < seqs:
            start_seq(first + i + 1, i + 1)
        else:
            @pl.when(step + 1 < n_steps)
            def _():
                start_seq(first + seqs, 0)
        wait_seq(first + i, i)
        for g in range(N_KV_HEADS):
            new_rows = pl.ds(i * ts * N_KV_HEADS + g, ts, stride=N_KV_HEADS)
            knew[0:ts, g * HEAD_DIM:(g + 1) * HEAD_DIM] = kn_ref[new_rows, :]
            vnew[0:ts, g * HEAD_DIM:(g + 1) * HEAD_DIM] = vn_ref[new_rows, :]
        knew[ts:, :] = pad
        vnew[ts:, :] = pad
        bias = jnp.concatenate([bias_ref[rows, :]] * GROUP, axis=0)
        row0 = pl.multiple_of((step * seqs + i) * ts, ts)
        for g in range(N_KV_HEADS):
            cols = slice(g * HEAD_DIM, (g + 1) * HEAD_DIM)
            k_past = kbuf[i, pl.ds(g, past, stride=N_KV_HEADS), :].astype(BF16)
            v_past = vbuf[i, pl.ds(g, past, stride=N_KV_HEADS), :].astype(BF16)
            q = q8_ref[g * GROUP:(g + 1) * GROUP, rows, :].reshape(GROUP * ts, HEAD_DIM).astype(BF16)
            logits = jnp.concatenate([_dot_nt(q, k_past), _dot_nt(q, knew[:, cols].astype(BF16))], axis=1)
            logits = logits * (HEAD_DIM ** -0.5) + bias
            m = jnp.max(logits, axis=-1, keepdims=True)
            e = jnp.exp(logits - m)
            den = jnp.sum(e, axis=-1, keepdims=True)
            eb = e.astype(BF16)
            o = (_dot(eb[:, :past], v_past) + _dot(eb[:, past:], vnew[:, cols].astype(BF16))) / den
            for hh in range(GROUP):
                head = g * GROUP + hh
                att_ref[pl.ds(row0, ts), head * HEAD_DIM:(head + 1) * HEAD_DIM] = o[hh * ts:(hh + 1) * ts]

    @pl.when(step == n_steps - 1)
    def _():
        o_ref[...] = h_ref[...] + _dot(att_ref[...].astype(BF16), wo_ref[...])


def _dsa_sample_call(h, q8, iq4, misc, k_new, v_new, cache_k, cache_v, cache_ik, layer, page_table, wo, ts):
    n, d = h.shape
    nb = n // ts
    n_layers, n_pool, page = cache_k.shape[:3]
    n_pages = page_table.shape[1]
    past = n_pages * page
    l_pad = past + LANES
    k_sel = min(TOPK_MAX, (past + ts) // 4)
    ck = cache_k.reshape(n_layers, n_pool, page * N_KV_HEADS, HEAD_DIM)
    cv = cache_v.reshape(n_layers, n_pool, page * N_KV_HEADS, HEAD_DIM)
    cikt = jnp.swapaxes(cache_ik, 2, 3)
    sel_rows = SELECT_SEQS * ts
    bias = pl.pallas_call(
        functools.partial(_sample_select_kernel, k_sel=k_sel, layer=layer, ts=ts),
        grid_spec=pltpu.PrefetchScalarGridSpec(
            num_scalar_prefetch=1,
            grid=(nb // SELECT_SEQS,),
            in_specs=[
                pl.BlockSpec((IDX_HEADS, sel_rows, IDX_DIM), lambda s, pt: (0, s, 0)),
                pl.BlockSpec((sel_rows, LANES), lambda s, pt: (s, 0)),
                pl.BlockSpec(memory_space=pl.ANY),
            ],
            out_specs=pl.BlockSpec((sel_rows, l_pad), lambda s, pt: (s, 0)),
            scratch_shapes=[
                pltpu.VMEM((SELECT_SEQS, IDX_DIM, past), F32),
                pltpu.VMEM((SELECT_SEQS, LANES, IDX_DIM), F32),
                pltpu.VMEM((sel_rows, l_pad), I32),
                pltpu.VMEM((sel_rows, LANES), I32),
                pltpu.SemaphoreType.DMA((1,)),
            ],
        ),
        out_shape=jax.ShapeDtypeStruct((n, l_pad), F32),
        compiler_params=_params(),
        name="sample_select",
    )(page_table, iq4, misc, cikt)

    att_rows = ATTEND_SEQS * ts
    return pl.pallas_call(
        functools.partial(_sample_attend_kernel, layer=layer, ts=ts),
        grid_spec=pltpu.PrefetchScalarGridSpec(
            num_scalar_prefetch=1,
            grid=(nb // ATTEND_SEQS,),
            in_specs=[
                pl.BlockSpec((n, d), lambda s, pt: (0, 0)),
                pl.BlockSpec((N_HEADS, att_rows, HEAD_DIM), lambda s, pt: (0, s, 0)),
                pl.BlockSpec((att_rows * N_KV_HEADS, HEAD_DIM), lambda s, pt: (s, 0)),
                pl.BlockSpec((att_rows * N_KV_HEADS, HEAD_DIM), lambda s, pt: (s, 0)),
                pl.BlockSpec((att_rows, l_pad), lambda s, pt: (s, 0)),
                pl.BlockSpec(memory_space=pl.ANY),
                pl.BlockSpec(memory_space=pl.ANY),
                pl.BlockSpec((d, d), lambda s, pt: (0, 0)),
            ],
            out_specs=pl.BlockSpec((n, d), lambda s, pt: (0, 0)),
            scratch_shapes=[
                pltpu.VMEM((ATTEND_SEQS, past * N_KV_HEADS, HEAD_DIM), F32),
                pltpu.VMEM((ATTEND_SEQS, past * N_KV_HEADS, HEAD_DIM), F32),
                pltpu.VMEM((LANES, KV_W), F32),
                pltpu.VMEM((LANES, KV_W), F32),
                pltpu.VMEM((n, d), F32),
                pltpu.SemaphoreType.DMA((2, ATTEND_SEQS)),
            ],
        ),
        out_shape=jax.ShapeDtypeStruct((n, d), F32),
        compiler_params=_params(),
        name="sample_attend",
    )(page_table, h, q8, k_new, v_new, bias, ck, cv, wo)


def _sgu_kernel(h_ref, g_ref, win_ref, bin_ref, gv_ref, ws_ref, bs_ref, wout_ref, o_ref, *rest,
                seg, emit_v):
    if emit_v:
        v_ref, gated_ref = rest
    else:
        (gated_ref,) = rest
    tm, d = h_ref.shape
    c_len = ws_ref.shape[1]
    d_sg = gv_ref.shape[1]
    gw = d_sg // SG_GROUPS
    x = h_ref[...]
    xn = _rms(x, g_ref[...]).astype(BF16)
    z = _dot(xn, win_ref[...]) + bin_ref[...]
    z = 0.5 * z * (1.0 + lax.erf(z * (0.5 ** 0.5)))
    u = z[:, :d_sg]
    v = _rms(z[:, d_sg:], gv_ref[...])
    if emit_v:
        v_ref[...] = v
    vb = v.astype(BF16)
    row = lax.broadcasted_iota(I32, (c_len, c_len), 0)
    col = lax.broadcasted_iota(I32, (c_len, c_len), 1)
    same_seq = (row >> _log2(seg)) == (col >> _log2(seg))
    if seg != c_len:
        src = lax.broadcasted_iota(I32, (ws_ref.shape[2], c_len), 0)
        dst = lax.broadcasted_iota(I32, (ws_ref.shape[2], c_len), 1)
        spread = jnp.where((dst & (seg - 1)) == src, 1.0, 0.0).astype(BF16)
    for g in range(SG_GROUPS):
        wsg = ws_ref[g] if seg == c_len else _dot(ws_ref[g].astype(BF16), spread)
        wg = jnp.where(col <= row, jnp.where(same_seq, wsg, 0.0), 0.0).astype(BF16)
        bg = bs_ref[:, g:g + 1]
        for ch in range(tm // c_len):
            rows = slice(ch * c_len, (ch + 1) * c_len)
            cols = slice(g * gw, (g + 1) * gw)
            mixed = _dot(wg, vb[rows, cols]) + bg
            gated_ref[rows, cols] = (u[rows, cols] * mixed).astype(BF16)
    o_ref[...] = x + _dot(gated_ref[...], wout_ref[...])


def _sgu_call(h, g, win, b_in, gv, ws, bs_t, wout, tm, seg, emit_v):
    n, d = h.shape
    d2 = win.shape[1]
    d_sg = d2 // 2
    c_len = ws.shape[1]
    out_shape = [jax.ShapeDtypeStruct((n, d), F32)]
    out_specs = [pl.BlockSpec((tm, d), lambda i: (i, 0))]
    if emit_v:
        out_shape.append(jax.ShapeDtypeStruct((n, d_sg), F32))
        out_specs.append(pl.BlockSpec((tm, d_sg), lambda i: (i, 0)))
    res = pl.pallas_call(
        functools.partial(_sgu_kernel, seg=seg, emit_v=emit_v),
        grid=(n // tm,),
        in_specs=[
            pl.BlockSpec((tm, d), lambda i: (i, 0)),
            pl.BlockSpec((1, d), lambda i: (0, 0)),
            pl.BlockSpec((d, d2), lambda i: (0, 0)),
            pl.BlockSpec((1, d2), lambda i: (0, 0)),
            pl.BlockSpec((1, d_sg), lambda i: (0, 0)),
            pl.BlockSpec(ws.shape, lambda i: (0, 0, 0)),
            pl.BlockSpec((c_len, SG_GROUPS), lambda i: (0, 0)),
            pl.BlockSpec((d_sg, d), lambda i: (0, 0)),
        ],
        out_specs=tuple(out_specs),
        out_shape=tuple(out_shape),
        scratch_shapes=[pltpu.VMEM((tm, d_sg), BF16)],
        compiler_params=_params(),
        name="sgu",
    )(h, g.reshape(1, d), win, b_in.reshape(1, d2), gv.reshape(1, d_sg), ws, bs_t, wout)
    return res


FFN_CHUNK = 256


def _ffn_body(x, p_ref, gf_ref, wup_ref, cw_ref, cb_ref, wdn_ref, gp_ref, wple_ref, wgate_ref, gfin_ref,
              o_ref, act_ref, shifted, emit_up, final_norm):
    d_ff = wdn_ref.shape[0]
    xn = _rms(x, gf_ref[...]).astype(BF16)

    def conv(cols):
        up = _dot(xn, wup_ref[:, cols])
        m1, m2 = shifted(up, cols)
        emit_up(up, cols)
        return cb_ref[:, cols] + cw_ref[0:1, cols] * m2 + cw_ref[1:2, cols] * m1 + cw_ref[2:3, cols] * up

    for c in range(d_ff // FFN_CHUNK):
        gate = conv(slice(c * FFN_CHUNK, (c + 1) * FFN_CHUNK))
        val = conv(slice(d_ff + c * FFN_CHUNK, d_ff + (c + 1) * FFN_CHUNK))
        act_ref[:, c * FFN_CHUNK:(c + 1) * FFN_CHUNK] = (gate * _sigmoid(gate) * val).astype(BF16)
    h2 = x + _dot(act_ref[...], wdn_ref[...])
    gate = _sigmoid(_dot(_rms(h2, gp_ref[...]).astype(BF16), wgate_ref[...]))
    h3 = h2 + _dot(p_ref[...].astype(BF16), wple_ref[...]) * gate
    o_ref[...] = _rms(h3, gfin_ref[...]) if final_norm else h3


def _ffn_prompt_kernel(*refs, tiles_per_seq, final_norm, with_attn):
    if with_attn:
        att_ref, wo_ref, *refs = refs
    (h_ref, p_ref, gf_ref, wup_ref, cw_ref, cb_ref, wdn_ref, gp_ref, wple_ref, wgate_ref, gfin_ref,
     o_ref, tail_ref, carry_ref, act_ref) = refs
    tm = h_ref.shape[0]
    i = pl.program_id(0)

    @pl.when(i % tiles_per_seq == 0)
    def _():
        carry_ref[...] = jnp.zeros(carry_ref.shape, F32)

    row = lax.broadcasted_iota(I32, (tm, FFN_CHUNK), 0)

    def shifted(up, cols):
        prev = carry_ref[:, cols]
        p1 = prev[SUBLANES - 1:SUBLANES]
        p2 = prev[SUBLANES - 2:SUBLANES - 1]
        m1 = jnp.where(row >= 1, pltpu.roll(up, 1, 0), p1)
        m2 = jnp.where(row >= 2, pltpu.roll(up, 2, 0), jnp.where(row == 0, p2, p1))
        return m1, m2

    def emit_up(up, cols):
        last = up[tm - SUBLANES:tm]
        carry_ref[:, cols] = last
        tail_ref[0, :, cols] = last

    x = h_ref[...]
    if with_attn:
        x = x + _dot(att_ref[...], wo_ref[...])
    _ffn_body(x, p_ref, gf_ref, wup_ref, cw_ref, cb_ref, wdn_ref, gp_ref, wple_ref, wgate_ref, gfin_ref,
              o_ref, act_ref, shifted, emit_up, final_norm)


def _ffn_sample_kernel(h_ref, p_ref, pm1_ref, pm2_ref, gf_ref, wup_ref, cw_ref, cb_ref, wdn_ref, gp_ref, wple_ref,
                       wgate_ref, gfin_ref, o_ref, up_ref, act_ref, *, seg, final_norm):
    tm = h_ref.shape[0]
    row = lax.broadcasted_iota(I32, (tm, FFN_CHUNK), 0) & ((1 << _log2(seg)) - 1)

    def shifted(up, cols):
        m1 = jnp.where(row >= 1, pltpu.roll(up, 1, 0), pm1_ref[:, cols])
        m2 = jnp.where(row >= 2, pltpu.roll(up, 2, 0), pm2_ref[:, cols])
        return m1, m2

    def emit_up(up, cols):
        up_ref[:, cols] = up

    _ffn_body(h_ref[...], p_ref, gf_ref, wup_ref, cw_ref, cb_ref, wdn_ref, gp_ref, wple_ref, wgate_ref, gfin_ref,
              o_ref, act_ref, shifted, emit_up, final_norm)


def _ffn_weight_specs(d, f2, d_ff, ple, layer):
    full = lambda shape: pl.BlockSpec(shape, lambda i: (0,) * len(shape))
    of_layer = lambda shape: pl.BlockSpec((None,) + shape, lambda i: (layer, 0, 0), pipeline_mode=pl.Buffered(1))
    return [of_layer((1, d)), of_layer((d, f2)), of_layer((CONV_W, f2)), of_layer((1, f2)), of_layer((d_ff, d)),
            of_layer((1, d)), of_layer((ple, d)), of_layer((d, d)), full((1, d))]


def _ffn_prompt_call(h, att, wo, p_all, layer, weights, tm, tiles_per_seq, final_norm):
    n, d = h.shape
    gf, wup, cw, cb, wdn, gp, wple, wgate, gfin = weights
    f2 = wup.shape[2]
    d_ff = wdn.shape[1]
    ple = p_all.shape[2]
    nt = n // tm
    with_attn = att is not None
    attn_specs = [pl.BlockSpec((tm, d), lambda i: (i, 0)), pl.BlockSpec((d, d), lambda i: (0, 0))] if with_attn else []
    attn_args = (att, wo) if with_attn else ()
    return pl.pallas_call(
        functools.partial(_ffn_prompt_kernel, tiles_per_seq=tiles_per_seq, final_norm=final_norm,
                          with_attn=with_attn),
        grid=(nt,),
        in_specs=attn_specs
        + [pl.BlockSpec((tm, d), lambda i: (i, 0)), pl.BlockSpec((None, tm, ple), lambda i: (layer, i, 0))]
        + _ffn_weight_specs(d, f2, d_ff, ple, layer),
        out_specs=(pl.BlockSpec((tm, d), lambda i: (i, 0)), pl.BlockSpec((1, SUBLANES, f2), lambda i: (i, 0, 0))),
        out_shape=(jax.ShapeDtypeStruct((n, d), F32), jax.ShapeDtypeStruct((nt, SUBLANES, f2), F32)),
        scratch_shapes=[pltpu.VMEM((SUBLANES, f2), F32), pltpu.VMEM((tm, d_ff), BF16)],
        compiler_params=_params(),
        name="ffn_prompt",
    )(*attn_args, h, p_all, gf, wup, cw, cb, wdn, gp, wple, wgate, gfin)


def _ffn_sample_call(h, p, pm1, pm2, layer, weights, seg, final_norm):
    n, d = h.shape
    gf, wup, cw, cb, wdn, gp, wple, wgate, gfin = weights
    f2 = wup.shape[2]
    d_ff = wdn.shape[1]
    ple = p.shape[2]
    full = lambda shape: pl.BlockSpec(shape, lambda i: (0,) * len(shape))
    of_layer = lambda shape: pl.BlockSpec((None,) + shape, lambda i: (layer, 0, 0))
    return pl.pallas_call(
        functools.partial(_ffn_sample_kernel, seg=seg, final_norm=final_norm),
        grid=(1,),
        in_specs=[full((n, d)), of_layer((n, ple)), of_layer((n, f2)), of_layer((n, f2))]
        + _ffn_weight_specs(d, f2, d_ff, ple, layer),
        out_specs=(full((n, d)), full((n, f2))),
        out_shape=(jax.ShapeDtypeStruct((n, d), F32), jax.ShapeDtypeStruct((n, f2), F32)),
        scratch_shapes=[pltpu.VMEM((n, d_ff), BF16)],
        compiler_params=_params(),
        name="ffn_sample",
    )(h, p, pm1, pm2, gf, wup, cw, cb, wdn, gp, wple, wgate, gfin)


PROMPT_TILE = 512


def kernel(x_prompt, x_sample, cache_k, cache_v, cache_idx_k, state_conv, page_table, p_prompt, p_sample,
           norm_mix, w_attn_in, w_attn_out, w_sg_in, b_sg_in, norm_sg_v, w_sg_spatial, b_sg_spatial, w_sg_out,
           norm_ffn, w_ffn_up, w_ffn_conv, b_ffn_conv, w_ffn_down, norm_ple, w_ple, w_ple_gate, norm_final):
    batch, seq, d = x_prompt.shape
    nb, ts, _ = x_sample.shape
    depth = norm_mix.shape[0]
    page = cache_k.shape[2]
    past = page_table.shape[1] * page
    f2 = w_ffn_up.shape[2]
    n_s = nb * ts
    tm = PROMPT_TILE

    hp = x_prompt.reshape(batch * seq, d)
    hs = x_sample.reshape(n_s, d)
    tabs_p = _rope_tables(jnp.arange(seq))
    tabs_s = tuple(jnp.tile(t, (nb, 1)) for t in _rope_tables(past + jnp.arange(ts)))
    zeros = jnp.zeros((depth, nb, ts - 1, f2), F32)
    pm1 = jnp.concatenate([state_conv[:, :, 1:2], zeros], axis=2).reshape(depth, n_s, f2)
    pm2 = jnp.concatenate([state_conv, zeros[:, :, 1:]], axis=2).reshape(depth, n_s, f2)
    ffn_weights = (norm_ffn.reshape(depth, 1, d), w_ffn_up.astype(BF16), w_ffn_conv, b_ffn_conv.reshape(depth, 1, f2),
                   w_ffn_down.astype(BF16), norm_ple.reshape(depth, 1, d), w_ple.astype(BF16),
                   w_ple_gate.astype(BF16), norm_final.reshape(1, d))

    kp_l, vp_l, ikp_l, ks_l, vs_l, iks_l, cp_l, cs_l, sgv_l = [], [], [], [], [], [], [], [], []
    for i in range(depth):
        j = i // 2
        att = w_out = None
        if i % 2 == 0:
            w_in = jnp.pad(w_attn_in[j], ((0, 0), (0, PROJ_W - w_attn_in.shape[2]))).astype(BF16)
            w_out = w_attn_out[j].astype(BF16)
            q8, k, v, kb, vt, iq4, ikt, misct, ikb = _proj_call(hp, norm_mix[i], w_in, tabs_p, tm, True)
            att = _dsa_prompt_call(q8, iq4, misct, ikb, kb, vt, batch, seq)
            kp_l.append(k.reshape(batch, seq, N_KV_HEADS, HEAD_DIM))
            vp_l.append(v.reshape(batch, seq, N_KV_HEADS, HEAD_DIM))
            ikp_l.append(jnp.swapaxes(ikt, 1, 2))
            q8, k, v, iq4, misc = _proj_call(hs, norm_mix[i], w_in, tabs_s, n_s, False)
            hs = _dsa_sample_call(hs, q8, iq4, misc, k, v, cache_k, cache_v, cache_idx_k, j,
                                  page_table, w_out, ts)
            ks_l.append(k.reshape(nb, ts, N_KV_HEADS, HEAD_DIM))
            vs_l.append(v.reshape(nb, ts, N_KV_HEADS, HEAD_DIM))
            iks_l.append(misc[:, :IDX_DIM].reshape(nb, ts, IDX_DIM))
        else:
            win = w_sg_in[j].astype(BF16)
            wout = w_sg_out[j].astype(BF16)
            (hp,) = _sgu_call(hp, norm_mix[i], win, b_sg_in[j], norm_sg_v[j], w_sg_spatial[j],
                              b_sg_spatial[j].T, wout, tm, SG_CHUNK, False)
            ws_s = jnp.tile(jnp.pad(w_sg_spatial[j][:, :ts, :ts], ((0, 0), (0, 0), (0, LANES - ts))), (1, nb, 1))
            bs_s = jnp.tile(b_sg_spatial[j].T[:ts], (nb, 1))
            hs, v_rows = _sgu_call(hs, norm_mix[i], win, b_sg_in[j], norm_sg_v[j], ws_s, bs_s, wout, n_s, ts, True)
            sgv_l.append(v_rows.reshape(nb, ts, -1))

        final = i == depth - 1
        weights = ffn_weights
        hp, tail = _ffn_prompt_call(hp, att, w_out, p_prompt.reshape(depth, batch * seq, -1), i, weights,
                                    tm, seq // tm, final)
        cp_l.append(tail.reshape(batch, seq // tm, SUBLANES, f2)[:, -1, SUBLANES - (CONV_W - 1):])
        hs, up_s = _ffn_sample_call(hs, p_sample.reshape(depth, n_s, -1), pm1, pm2, i, weights, ts, final)
        cs_l.append(up_s.reshape(nb, ts, f2)[:, ts - (CONV_W - 1):])

    return (hp.reshape(batch, seq, d), hs.reshape(nb, ts, d),
            jnp.stack(kp_l), jnp.stack(vp_l), jnp.stack(ikp_l),
            jnp.stack(ks_l), jnp.stack(vs_l), jnp.stack(iks_l),
            jnp.stack(cp_l), jnp.stack(cs_l), jnp.stack(sgv_l))
```

```python
import functools

import jax
import jax.numpy as jnp
from jax import lax
from jax.experimental import pallas as pl
from jax.experimental.pallas import tpu as pltpu

F32 = jnp.float32
BF16 = jnp.bfloat16
I32 = jnp.int32

N_HEADS = 8
N_KV_HEADS = 2
GROUP = N_HEADS // N_KV_HEADS
HEAD_DIM = 128
IDX_HEADS = 4
IDX_DIM = 64
IDX_W_SCALE = (IDX_HEADS * IDX_DIM) ** -0.5
TOPK_MAX = 256
SG_CHUNK = 128
SG_GROUPS = 8
CONV_W = 3
ROPE_THETA = 10000.0
EPS = 1e-6

LANES = 128
SUBLANES = 8
V7X_VMEM_BYTES = 64 * 1024 * 1024
VMEM_LIMIT_BYTES = V7X_VMEM_BYTES - 8 * 1024 * 1024

NEG_INF = float("-inf")
LOG2_E = 1.4426950408889634
INT_MIN = -(2 ** 31)


def _params(grid_rank=1):
    return pltpu.CompilerParams(dimension_semantics=("arbitrary",) * grid_rank,
                                vmem_limit_bytes=VMEM_LIMIT_BYTES)


def _log2(n):
    assert n > 0 and n & (n - 1) == 0, n
    return n.bit_length() - 1


def _rms(x, g):
    ms = jnp.mean(x * x, axis=-1, keepdims=True)
    return x * lax.rsqrt(ms + EPS) * g


def _sigmoid(x):
    return 1.0 / (1.0 + jnp.exp(-x))


def _dot(a, b):
    return jnp.dot(a, b, preferred_element_type=F32)


def _dot_nt(a, b):
    return lax.dot_general(a, b, (((1,), (1,)), ((), ())), preferred_element_type=F32)


def _sortable_key(score):
    b = pltpu.bitcast(score, I32)
    b = jnp.where(b == INT_MIN, 0, b)
    return b ^ ((b >> 31) & 0x7FFFFFFF)


Q_OFF = 0
K_OFF = N_HEADS * HEAD_DIM
V_OFF = K_OFF + N_KV_HEADS * HEAD_DIM
IQ_OFF = V_OFF + N_KV_HEADS * HEAD_DIM
MISC_OFF = IQ_OFF + IDX_HEADS * IDX_DIM
PROJ_W = MISC_OFF + LANES
KV_W = N_KV_HEADS * HEAD_DIM
ATTN_TILE = 256


def _project(h_ref, g_ref, w_ref, tab_refs, q8_ref, k_ref, v_ref, iq4_ref, queries_on_lanes):
    c128_ref, s128_ref, c64_ref, s64_ref, cm_ref, sm_ref = tab_refs
    tm = h_ref.shape[0]
    xn = _rms(h_ref[...], g_ref[...]).astype(BF16)
    y = _dot(xn, w_ref[...])
    c128 = c128_ref[...]
    s128 = s128_ref[...]

    def rope128(t):
        return t * c128 + pltpu.roll(t, HEAD_DIM // 2, 1) * s128

    for h in range(N_HEADS):
        qh = rope128(y[:, Q_OFF + h * HEAD_DIM:Q_OFF + (h + 1) * HEAD_DIM])
        q8_ref[h] = (qh.T if queries_on_lanes else qh).astype(q8_ref.dtype)
    ks = []
    for h in range(N_KV_HEADS):
        kh = rope128(y[:, K_OFF + h * HEAD_DIM:K_OFF + (h + 1) * HEAD_DIM])
        k_ref[pl.ds(h, tm, stride=N_KV_HEADS), :] = kh
        v_ref[pl.ds(h, tm, stride=N_KV_HEADS), :] = y[:, V_OFF + h * HEAD_DIM:V_OFF + (h + 1) * HEAD_DIM]
        ks.append(kh)
    v = y[:, V_OFF:IQ_OFF]

    lane = lax.broadcasted_iota(I32, (tm, LANES), 1)
    low_half = (lane & (IDX_DIM - 1)) < (IDX_DIM // 2)

    def rope64(t, c, s):
        partner = jnp.where(low_half, pltpu.roll(t, LANES - IDX_DIM // 2, 1), pltpu.roll(t, IDX_DIM // 2, 1))
        return t * c + partner * s

    c64 = c64_ref[...]
    s64 = s64_ref[...]
    for pair in range(IDX_HEADS // 2):
        t = rope64(y[:, IQ_OFF + pair * LANES:IQ_OFF + (pair + 1) * LANES], c64, s64)
        if queries_on_lanes:
            tt = t.T
            iq4_ref[2 * pair] = tt[:IDX_DIM, :].astype(iq4_ref.dtype)
            iq4_ref[2 * pair + 1] = tt[IDX_DIM:, :].astype(iq4_ref.dtype)
        else:
            iq4_ref[2 * pair] = t[:, :IDX_DIM].astype(iq4_ref.dtype)
            iq4_ref[2 * pair + 1] = t[:, IDX_DIM:].astype(iq4_ref.dtype)
    m = rope64(y[:, MISC_OFF:MISC_OFF + LANES], cm_ref[...], sm_ref[...])
    return ks, v, m


def _proj_prompt_kernel(h_ref, g_ref, w_ref, c128_ref, s128_ref, c64_ref, s64_ref, cm_ref, sm_ref,
                        q8_ref, k_ref, v_ref, kb_ref, vt_ref, iq4_ref, ikt_ref, misct_ref, ikb_ref):
    tm = h_ref.shape[0]
    ks, v, m = _project(h_ref, g_ref, w_ref, (c128_ref, s128_ref, c64_ref, s64_ref, cm_ref, sm_ref),
                        q8_ref, k_ref, v_ref, iq4_ref, True)
    for h in range(N_KV_HEADS):
        kb_ref[:, h * HEAD_DIM:(h + 1) * HEAD_DIM] = ks[h].astype(BF16)
    for c in range(tm // ATTN_TILE):
        vt_ref[c] = v[c * ATTN_TILE:(c + 1) * ATTN_TILE].T.astype(BF16)
    mt = m.T
    misct_ref[...] = mt
    ikt_ref[0] = mt[:IDX_DIM, :]
    ikb_ref[...] = m[:, :IDX_DIM].astype(BF16)


def _proj_sample_kernel(h_ref, g_ref, w_ref, c128_ref, s128_ref, c64_ref, s64_ref, cm_ref, sm_ref,
                        q8_ref, k_ref, v_ref, iq4_ref, misc_ref):
    _, _, m = _project(h_ref, g_ref, w_ref, (c128_ref, s128_ref, c64_ref, s64_ref, cm_ref, sm_ref),
                       q8_ref, k_ref, v_ref, iq4_ref, False)
    misc_ref[...] = m


def _proj_call(h, g, w, tabs, tm, prompt):
    n, d = h.shape
    t_tab = tabs[0].shape[0]
    n_tab = t_tab // tm
    tab_spec = pl.BlockSpec((tm, LANES), lambda i: (i % n_tab, 0))
    row = lambda w_: pl.BlockSpec((tm, w_), lambda i: (i, 0))
    if prompt:
        q8 = (jax.ShapeDtypeStruct((N_HEADS, HEAD_DIM, n), BF16), pl.BlockSpec((N_HEADS, HEAD_DIM, tm), lambda i: (0, 0, i)))
        iq4 = (jax.ShapeDtypeStruct((IDX_HEADS, IDX_DIM, n), BF16), pl.BlockSpec((IDX_HEADS, IDX_DIM, tm), lambda i: (0, 0, i)))
    else:
        q8 = (jax.ShapeDtypeStruct((N_HEADS, n, HEAD_DIM), F32), pl.BlockSpec((N_HEADS, tm, HEAD_DIM), lambda i: (0, i, 0)))
        iq4 = (jax.ShapeDtypeStruct((IDX_HEADS, n, IDX_DIM), F32), pl.BlockSpec((IDX_HEADS, tm, IDX_DIM), lambda i: (0, i, 0)))
    kf = (jax.ShapeDtypeStruct((n * N_KV_HEADS, HEAD_DIM), F32),
          pl.BlockSpec((tm * N_KV_HEADS, HEAD_DIM), lambda i: (i, 0)))
    if prompt:
        assert tm % ATTN_TILE == 0
        outs = [q8, kf, kf, (jax.ShapeDtypeStruct((n, KV_W), BF16), row(KV_W)),
                (jax.ShapeDtypeStruct((n // ATTN_TILE, KV_W, ATTN_TILE), BF16),
                 pl.BlockSpec((tm // ATTN_TILE, KV_W, ATTN_TILE), lambda i: (i, 0, 0))),
                iq4,
                (jax.ShapeDtypeStruct((n // t_tab, IDX_DIM, t_tab), F32),
                 pl.BlockSpec((1, IDX_DIM, tm), lambda i: (i // n_tab, 0, i % n_tab))),
                (jax.ShapeDtypeStruct((LANES, n), F32), pl.BlockSpec((LANES, tm), lambda i: (0, i))),
                (jax.ShapeDtypeStruct((n, IDX_DIM), BF16), row(IDX_DIM))]
        body = _proj_prompt_kernel
    else:
        outs = [q8, kf, kf, iq4, (jax.ShapeDtypeStruct((n, LANES), F32), row(LANES))]
        body = _proj_sample_kernel
    return pl.pallas_call(
        body,
        grid=(n // tm,),
        in_specs=[
            pl.BlockSpec((tm, d), lambda i: (i, 0)),
            pl.BlockSpec((1, d), lambda i: (0, 0)),
            pl.BlockSpec((d, PROJ_W), lambda i: (0, 0)),
        ] + [tab_spec] * 6,
        out_specs=tuple(o[1] for o in outs),
        out_shape=tuple(o[0] for o in outs),
        compiler_params=_params(),
        name="attn_proj_prompt" if prompt else "attn_proj_sample",
    )(h, g.reshape(1, d), w, *tabs)


def _rope_tables(pos):
    t = pos.shape[0]

    def tab(half):
        inv = ROPE_THETA ** (-jnp.arange(half, dtype=F32) / half)
        ang = pos.astype(F32)[:, None] * inv[None, :]
        return jnp.cos(ang), jnp.sin(ang)

    c64, s64 = tab(HEAD_DIM // 2)
    c32, s32 = tab(IDX_DIM // 2)
    c128 = jnp.concatenate([c64, c64], axis=1)
    s128 = jnp.concatenate([-s64, s64], axis=1)
    cq = jnp.concatenate([c32, c32, c32, c32], axis=1)
    sq = jnp.concatenate([-s32, s32, -s32, s32], axis=1)
    cm = jnp.concatenate([c32, c32, jnp.full((t, LANES - IDX_DIM), IDX_W_SCALE, F32)], axis=1)
    sm = jnp.concatenate([-s32, s32, jnp.zeros((t, LANES - IDX_DIM), F32)], axis=1)
    return (c128, s128, cq, sq, cm, sm)


COUNT_SLAB = 64


def _col_count(key_ref, n_rows, indicator):
    w = key_ref.shape[1]
    acc = jnp.zeros((COUNT_SLAB, w), I32)
    for c in range(n_rows // COUNT_SLAB):
        acc = acc + indicator(key_ref[c * COUNT_SLAB:(c + 1) * COUNT_SLAB, :], c * COUNT_SLAB)
    return jnp.sum(acc, axis=0, keepdims=True)


I16 = jnp.int16
I16_MIN = -(2 ** 15)
HALF_SLAB = 128
PACK_ROWS = 16


def _bcast16(x, rows):
    one = jnp.broadcast_to(x, (PACK_ROWS, x.shape[1])).astype(I16)
    return jnp.concatenate([one] * (rows // PACK_ROWS), axis=0)


def _col_count16(half_ref, n_rows, indicator):
    w = half_ref.shape[1]
    acc = jnp.zeros((HALF_SLAB, w), I16)
    for c in range(n_rows // HALF_SLAB):
        acc = acc + indicator(half_ref[c * HALF_SLAB:(c + 1) * HALF_SLAB, :])
    return jnp.sum(acc.astype(I32), axis=0, keepdims=True)


def _kth_largest16(half_ref, n_rows, kk):
    w = half_ref.shape[1]
    one = jnp.ones((HALF_SLAB, w), I16)
    zero = jnp.zeros((HALF_SLAB, w), I16)

    def body(i, t):
        cand = t + lax.shift_left(jnp.int32(1), 15 - i)
        candb = _bcast16(cand, HALF_SLAB)
        cnt = _col_count16(half_ref, n_rows, lambda k: jnp.where(k >= candb, one, zero))
        return jnp.where(cnt >= kk, cand, t)

    return lax.fori_loop(0, 16, body, jnp.full((1, w), I16_MIN, I32))


def _select_bias_cols(key_ref, hi_ref, lo_ref, bias_ref, n_rows, kq):
    w = key_ref.shape[1]
    one = jnp.ones((HALF_SLAB, w), I16)
    zero = jnp.zeros((HALF_SLAB, w), I16)

    t_hi = _kth_largest16(hi_ref, n_rows, kq)
    t_hib = _bcast16(t_hi, HALF_SLAB)
    above = _col_count16(hi_ref, n_rows, lambda k: jnp.where(k > t_hib, one, zero))
    lowest = jnp.full((HALF_SLAB, w), I16_MIN, I16)
    for c in range(n_rows // HALF_SLAB):
        rows = slice(c * HALF_SLAB, (c + 1) * HALF_SLAB)
        lo_ref[rows, :] = jnp.where(hi_ref[rows, :] == t_hib, lo_ref[rows, :], lowest)
    t_lo = _kth_largest16(lo_ref, n_rows, kq - above)
    thr = t_hi * 65536 + (t_lo - I16_MIN)

    cnt_gt = _col_count(key_ref, n_rows, lambda k, r0: jnp.where(k > thr, 1, 0))
    cnt_ge = _col_count(key_ref, n_rows, lambda k, r0: jnp.where(k >= thr, 1, 0))
    need = kq - cnt_gt

    drop_ties = jnp.max(cnt_ge - kq) > 0

    @pl.when(jnp.logical_not(drop_ties))
    def _():
        for c in range(n_rows // COUNT_SLAB):
            rows = slice(c * COUNT_SLAB, (c + 1) * COUNT_SLAB)
            bias_ref[rows, :] = jnp.where(key_ref[rows, :] >= thr, 0.0, NEG_INF)

    @pl.when(drop_ties)
    def _():
        nbits = max(1, (n_rows - 1).bit_length())
        sub = lax.broadcasted_iota(I32, (COUNT_SLAB, w), 0)

        def lt_body(i, j):
            cand = j + lax.shift_left(jnp.int32(1), nbits - 1 - i)
            cnt = _col_count(key_ref, n_rows,
                             lambda k, r0: jnp.where(k == thr, jnp.where(sub + r0 < cand, 1, 0), 0))
            return jnp.where(cnt < need, cand, j)

        jmax = lax.fori_loop(0, nbits, lt_body, jnp.zeros((1, w), I32))
        for c in range(n_rows // COUNT_SLAB):
            rows = slice(c * COUNT_SLAB, (c + 1) * COUNT_SLAB)
            k = key_ref[rows, :]
            keep_tie = jnp.where(sub + c * COUNT_SLAB <= jmax, 0.0, NEG_INF)
            bias_ref[rows, :] = jnp.where(k > thr, 0.0, jnp.where(k == thr, keep_tie, NEG_INF))


def _for_chunks(n, body):
    def pair(p, carry):
        body([2 * p, 2 * p + 1])
        return carry

    lax.fori_loop(0, lax.shift_right_logical(n, 1), pair, 0)

    @pl.when((n & 1) == 1)
    def _():
        body([n - 1])


def _dsa_prompt_kernel(q8_ref, iq4_ref, misct_ref, ik_ref, k_ref, vt_ref, att_ref,
                       key_ref, hi_ref, lo_ref, bias_ref, logit_ref, acc_ref, stat_ref, *, k_sel):
    t = q8_ref.shape[2]
    nq = key_ref.shape[0] // t
    gw = GROUP * t
    qi = pl.program_id(1)
    n_chunks = qi + 1
    qpos = qi * t + lax.broadcasted_iota(I32, (1, t), 1)
    rows_of = lambda c: pl.ds(pl.multiple_of(c * t, t), t)

    iq = jnp.concatenate([iq4_ref[h] for h in range(IDX_HEADS)], axis=1)
    iw = misct_ref[IDX_DIM:IDX_DIM + SUBLANES, :]

    def score_chunks(cs):
        for c in cs:
            s = _dot(ik_ref[rows_of(c), :], iq)
            score = jnp.zeros((t, t), F32)
            for h in range(IDX_HEADS):
                score = score + jnp.maximum(s[:, h * t:(h + 1) * t], 0.0) * iw[h:h + 1, :]
            kpos = c * t + lax.broadcasted_iota(I32, (t, t), 0)
            key = _sortable_key(jnp.where(kpos <= qpos, score, NEG_INF))
            key_ref[rows_of(c), :] = key
            hi_ref[rows_of(c), :] = (key >> 16).astype(I16)
            lo_ref[rows_of(c), :] = ((key & 0xFFFF) + I16_MIN).astype(I16)

    _for_chunks(n_chunks, score_chunks)

    kq = jnp.minimum(k_sel, qpos + 1)
    for j in range(nq):
        @pl.when(qi == j)
        def _():
            _select_bias_cols(key_ref, hi_ref, lo_ref, bias_ref, (j + 1) * t, kq)

    qs = [jnp.concatenate([q8_ref[g * GROUP + hh] for hh in range(GROUP)], axis=1)
          for g in range(N_KV_HEADS)]
    head_cols = lambda g: slice(g * HEAD_DIM, (g + 1) * HEAD_DIM)
    lanes_of = lambda g: slice(g * gw, (g + 1) * gw)
    stat_ref[0:1, :] = jnp.full((1, N_KV_HEADS * gw), NEG_INF, F32)
    stat_ref[1:2, :] = jnp.zeros((1, N_KV_HEADS * gw), F32)
    acc_ref[...] = jnp.zeros(acc_ref.shape, F32)

    def logits_chunks(cs):
        for g in range(N_KV_HEADS):
            m = stat_ref[0:1, lanes_of(g)]
            for c in cs:
                lg = _dot(k_ref[rows_of(c), head_cols(g)], qs[g]) * (HEAD_DIM ** -0.5 * LOG2_E)
                lg = lg + jnp.concatenate([bias_ref[rows_of(c), :]] * GROUP, axis=1)
                logit_ref[rows_of(c), lanes_of(g)] = lg
                m = jnp.maximum(m, jnp.max(lg, axis=0, keepdims=True))
            stat_ref[0:1, lanes_of(g)] = m

    _for_chunks(n_chunks, logits_chunks)

    def pv_chunks(cs):
        for g in range(N_KV_HEADS):
            m = stat_ref[0:1, lanes_of(g)]
            den = stat_ref[1:2, lanes_of(g)]
            for c in cs:
                e = jnp.exp2(logit_ref[rows_of(c), lanes_of(g)] - m)
                acc_ref[g] += _dot(vt_ref[c, head_cols(g), :], e.astype(BF16))
                den = den + jnp.sum(e, axis=0, keepdims=True)
            stat_ref[1:2, lanes_of(g)] = den

    _for_chunks(n_chunks, pv_chunks)

    for g in range(N_KV_HEADS):
        o = acc_ref[g] / stat_ref[1:2, lanes_of(g)]
        for hh in range(GROUP):
            head = g * GROUP + hh
            att_ref[:, head * HEAD_DIM:(head + 1) * HEAD_DIM] = o[:, hh * t:(hh + 1) * t].T.astype(BF16)


def _dsa_prompt_call(q8, iq4, misct, ikb, kb, vt, batch, seq):
    t = ATTN_TILE
    n = batch * seq
    d = N_HEADS * HEAD_DIM
    nq = seq // t
    k_sel = min(TOPK_MAX, seq // 4)
    return pl.pallas_call(
        functools.partial(_dsa_prompt_kernel, k_sel=k_sel),
        grid=(batch, nq),
        in_specs=[
            pl.BlockSpec((N_HEADS, HEAD_DIM, t), lambda b, q: (0, 0, b * nq + q)),
            pl.BlockSpec((IDX_HEADS, IDX_DIM, t), lambda b, q: (0, 0, b * nq + q)),
            pl.BlockSpec((LANES, t), lambda b, q: (0, b * nq + q)),
            pl.BlockSpec((seq, IDX_DIM), lambda b, q: (b, 0)),
            pl.BlockSpec((seq, KV_W), lambda b, q: (b, 0)),
            pl.BlockSpec((nq, KV_W, t), lambda b, q: (b, 0, 0)),
        ],
        out_specs=pl.BlockSpec((t, d), lambda b, q: (b * nq + q, 0)),
        out_shape=jax.ShapeDtypeStruct((n, d), BF16),
        scratch_shapes=[
            pltpu.VMEM((seq, t), I32),
            pltpu.VMEM((seq, t), I16),
            pltpu.VMEM((seq, t), I16),
            pltpu.VMEM((seq, t), F32),
            pltpu.VMEM((seq, N_KV_HEADS * GROUP * t), F32),
            pltpu.VMEM((N_KV_HEADS, HEAD_DIM, GROUP * t), F32),
            pltpu.VMEM((SUBLANES, N_KV_HEADS * GROUP * t), F32),
        ],
        compiler_params=_params(grid_rank=2),
        name="dsa_prompt",
    )(q8, iq4, misct, ikb, kb, vt)


def _row_count(key_ref, indicator):
    r, l = key_ref.shape
    acc = jnp.zeros((r, LANES), I32)
    for c in range(l // LANES):
        acc = acc + indicator(key_ref[:, c * LANES:(c + 1) * LANES], c)
    return jnp.sum(acc, axis=-1, keepdims=True)


def _select_bias_rows(key_ref, bias_ref, jmax_ref, kq):
    r, l = key_ref.shape

    def ge_body(i, t):
        cand = t + lax.shift_left(jnp.int32(1), 31 - i)
        candb = jnp.broadcast_to(cand, (r, LANES))
        cnt = _row_count(key_ref, lambda k, c: jnp.where(k >= candb, 1, 0))
        return jnp.where(cnt >= kq, cand, t)

    thr = lax.fori_loop(0, 32, ge_body, jnp.full((r, 1), INT_MIN, I32))
    thrb = jnp.broadcast_to(thr, (r, LANES))
    cnt_gt = _row_count(key_ref, lambda k, c: jnp.where(k > thrb, 1, 0))
    cnt_ge = _row_count(key_ref, lambda k, c: jnp.where(k >= thrb, 1, 0))
    need = kq - cnt_gt

    jmax_ref[...] = jnp.full(jmax_ref.shape, l, I32)
    lane = lax.broadcasted_iota(I32, (r, LANES), 1)

    @pl.when(jnp.max(cnt_ge - kq) > 0)
    def _():
        nbits = max(1, (l - 1).bit_length())

        def lt_body(i, j):
            cand = j + lax.shift_left(jnp.int32(1), nbits - 1 - i)
            candb = jnp.broadcast_to(cand, (r, LANES))
            cnt = _row_count(key_ref, lambda k, c: jnp.where(k == thrb, jnp.where(lane + c * LANES < candb, 1, 0), 0))
            return jnp.where(cnt < need, cand, j)

        jmax_ref[...] = jnp.broadcast_to(lax.fori_loop(0, nbits, lt_body, jnp.zeros((r, 1), I32)), jmax_ref.shape)

    jmaxb = jmax_ref[...]
    for c in range(l // LANES):
        k = key_ref[:, c * LANES:(c + 1) * LANES]
        keep_tie = jnp.where(lane + c * LANES <= jmaxb, 0.0, NEG_INF)
        bias_ref[:, c * LANES:(c + 1) * LANES] = jnp.where(k > thrb, 0.0, jnp.where(k == thrb, keep_tie, NEG_INF))


SELECT_SEQS = 8
ATTEND_SEQS = 2


def _sample_select_kernel(pt_ref, iq4_ref, misc_ref, cikt_ref, bias_ref, iktbuf, iknew, key_ref, jmax_ref, sem,
                          *, k_sel, layer, ts):
    page = cikt_ref.shape[3]
    n_pages = pt_ref.shape[1]
    past = n_pages * page
    l_pad = past + LANES
    seqs = iktbuf.shape[0]
    step = pl.program_id(0)

    def page_copy(i, p):
        lanes = pl.ds(pl.multiple_of(p * page, page), page)
        return pltpu.make_async_copy(cikt_ref.at[layer, pt_ref[step * seqs + i, p]], iktbuf.at[i, :, lanes], sem.at[0])

    for i in range(seqs):
        lax.fori_loop(0, n_pages, lambda p, c, i=i: (page_copy(i, p).start(), c)[1], 0)

    iknew[...] = jnp.zeros(iknew.shape, F32)
    for i in range(seqs):
        iknew[i, 0:ts, :] = misc_ref[i * ts:(i + 1) * ts, :IDX_DIM]

    for i in range(seqs):
        lax.fori_loop(0, n_pages, lambda p, c, i=i: (page_copy(i, p).wait(), c)[1], 0)

    qpos = past + lax.broadcasted_iota(I32, (ts, 1), 0)
    kpos = lax.broadcasted_iota(I32, (ts, l_pad), 1)
    for i in range(seqs):
        rows = slice(i * ts, (i + 1) * ts)
        iq = iq4_ref[:, rows, :].reshape(IDX_HEADS * ts, IDX_DIM).astype(BF16)
        s = jnp.concatenate([_dot(iq, iktbuf[i].astype(BF16)), _dot_nt(iq, iknew[i].astype(BF16))], axis=1)
        iw = misc_ref[rows, IDX_DIM:IDX_DIM + IDX_HEADS]
        score = jnp.zeros((ts, l_pad), F32)
        for h in range(IDX_HEADS):
            score = score + jnp.maximum(s[h * ts:(h + 1) * ts], 0.0) * iw[:, h:h + 1]
        key_ref[rows, :] = _sortable_key(jnp.where(kpos <= qpos, score, NEG_INF))
    kq = jnp.minimum(k_sel, jnp.concatenate([qpos] * seqs, axis=0) + 1)
    _select_bias_rows(key_ref, bias_ref, jmax_ref, kq)


def _sample_attend_kernel(pt_ref, h_ref, q8_ref, kn_ref, vn_ref, bias_ref, ck_ref, cv_ref, wo_ref, o_ref,
                          kbuf, vbuf, knew, vnew, att_ref, sems, *, layer, ts):
    page_rows = ck_ref.shape[2]
    n_pages = pt_ref.shape[1]
    past = n_pages * page_rows // N_KV_HEADS
    seqs = kbuf.shape[0]
    step = pl.program_id(0)
    n_steps = pl.num_programs(0)

    first = step * seqs

    def page_copies(seq, slot, p):
        src = pt_ref[seq, p]
        rows = pl.ds(pl.multiple_of(p * page_rows, page_rows), page_rows)
        return (pltpu.make_async_copy(ck_ref.at[layer, src], kbuf.at[slot, rows], sems.at[0, slot]),
                pltpu.make_async_copy(cv_ref.at[layer, src], vbuf.at[slot, rows], sems.at[1, slot]))

    def start_seq(seq, slot):
        def body(p, c):
            for cp in page_copies(seq, slot, p):
                cp.start()
            return c
        lax.fori_loop(0, n_pages, body, 0)

    def wait_seq(seq, slot):
        def body(p, c):
            for cp in page_copies(seq, slot, p):
                cp.wait()
            return c
        lax.fori_loop(0, n_pages, body, 0)

    @pl.when(step == 0)
    def _():
        start_seq(first, 0)

    pad = jnp.zeros((LANES - ts, KV_W), F32)
    for i in range(seqs):
        rows = slice(i * ts, (i + 1) * ts)
        if i + 1 < seqs:
            start_seq(first + i + 1, i + 1)
        else:
            @pl.when(step + 1 < n_steps)
            def _():
                start_seq(first + seqs, 0)
        wait_seq(first + i, i)
        for g in range(N_KV_HEADS):
            new_rows = pl.ds(i * ts * N_KV_HEADS + g, ts, stride=N_KV_HEADS)
            knew[0:ts, g * HEAD_DIM:(g + 1) * HEAD_DIM] = kn_ref[new_rows, :]
            vnew[0:ts, g * HEAD_DIM:(g + 1) * HEAD_DIM] = vn_ref[new_rows, :]
        knew[ts:, :] = pad
        vnew[ts:, :] = pad
        bias = jnp.concatenate([bias_ref[rows, :]] * GROUP, axis=0)
        row0 = pl.multiple_of((step * seqs + i) * ts, ts)
        for g in range(N_KV_HEADS):
            cols = slice(g * HEAD_DIM, (g + 1) * HEAD_DIM)
            k_past = kbuf[i, pl.ds(g, past, stride=N_KV_HEADS), :].astype(BF16)
            v_past = vbuf[i, pl.ds(g, past, stride=N_KV_HEADS), :].astype(BF16)
            q = q8_ref[g * GROUP:(g + 1) * GROUP, rows, :].reshape(GROUP * ts, HEAD_DIM).astype(BF16)
            logits = jnp.concatenate([_dot_nt(q, k_past), _dot_nt(q, knew[:, cols].astype(BF16))], axis=1)
            logits = logits * (HEAD_DIM ** -0.5) + bias
            m = jnp.max(logits, axis=-1, keepdims=True)
            e = jnp.exp(logits - m)
            den = jnp.sum(e, axis=-1, keepdims=True)
            eb = e.astype(BF16)
            o = (_dot(eb[:, :past], v_past) + _dot(eb[:, past:], vnew[:, cols].astype(BF16))) / den
            for hh in range(GROUP):
                head = g * GROUP + hh
                att_ref[pl.ds(row0, ts), head * HEAD_DIM:(head + 1) * HEAD_DIM] = o[hh * ts:(hh + 1) * ts]

    @pl.when(step == n_steps - 1)
    def _():
        o_ref[...] = h_ref[...] + _dot(att_ref[...].astype(BF16), wo_ref[...])


def _dsa_sample_call(h, q8, iq4, misc, k_new, v_new, cache_k, cache_v, cache_ik, layer, page_table, wo, ts):
    n, d = h.shape
    nb = n // ts
    n_layers, n_pool, page = cache_k.shape[:3]
    n_pages = page_table.shape[1]
    past = n_pages * page
    l_pad = past + LANES
    k_sel = min(TOPK_MAX, (past + ts) // 4)
    ck = cache_k.reshape(n_layers, n_pool, page * N_KV_HEADS, HEAD_DIM)
    cv = cache_v.reshape(n_layers, n_pool, page * N_KV_HEADS, HEAD_DIM)
    cikt = jnp.swapaxes(cache_ik, 2, 3)
    sel_rows = SELECT_SEQS * ts
    bias = pl.pallas_call(
        functools.partial(_sample_select_kernel, k_sel=k_sel, layer=layer, ts=ts),
        grid_spec=pltpu.PrefetchScalarGridSpec(
            num_scalar_prefetch=1,
            grid=(nb // SELECT_SEQS,),
            in_specs=[
                pl.BlockSpec((IDX_HEADS, sel_rows, IDX_DIM), lambda s, pt: (0, s, 0)),
                pl.BlockSpec((sel_rows, LANES), lambda s, pt: (s, 0)),
                pl.BlockSpec(memory_space=pl.ANY),
            ],
            out_specs=pl.BlockSpec((sel_rows, l_pad), lambda s, pt: (s, 0)),
            scratch_shapes=[
                pltpu.VMEM((SELECT_SEQS, IDX_DIM, past), F32),
                pltpu.VMEM((SELECT_SEQS, LANES, IDX_DIM), F32),
                pltpu.VMEM((sel_rows, l_pad), I32),
                pltpu.VMEM((sel_rows, LANES), I32),
                pltpu.SemaphoreType.DMA((1,)),
            ],
        ),
        out_shape=jax.ShapeDtypeStruct((n, l_pad), F32),
        compiler_params=_params(),
        name="sample_select",
    )(page_table, iq4, misc, cikt)

    att_rows = ATTEND_SEQS * ts
    return pl.pallas_call(
        functools.partial(_sample_attend_kernel, layer=layer, ts=ts),
        grid_spec=pltpu.PrefetchScalarGridSpec(
            num_scalar_prefetch=1,
            grid=(nb // ATTEND_SEQS,),
            in_specs=[
                pl.BlockSpec((n, d), lambda s, pt: (0, 0)),
                pl.BlockSpec((N_HEADS, att_rows, HEAD_DIM), lambda s, pt: (0, s, 0)),
                pl.BlockSpec((att_rows * N_KV_HEADS, HEAD_DIM), lambda s, pt: (s, 0)),
                pl.BlockSpec((att_rows * N_KV_HEADS, HEAD_DIM), lambda s, pt: (s, 0)),
                pl.BlockSpec((att_rows, l_pad), lambda s, pt: (s, 0)),
                pl.BlockSpec(memory_space=pl.ANY),
                pl.BlockSpec(memory_space=pl.ANY),
                pl.BlockSpec((d, d), lambda s, pt: (0, 0)),
            ],
            out_specs=pl.BlockSpec((n, d), lambda s, pt: (0, 0)),
            scratch_shapes=[
                pltpu.VMEM((ATTEND_SEQS, past * N_KV_HEADS, HEAD_DIM), F32),
                pltpu.VMEM((ATTEND_SEQS, past * N_KV_HEADS, HEAD_DIM), F32),
                pltpu.VMEM((LANES, KV_W), F32),
                pltpu.VMEM((LANES, KV_W), F32),
                pltpu.VMEM((n, d), F32),
                pltpu.SemaphoreType.DMA((2, ATTEND_SEQS)),
            ],
        ),
        out_shape=jax.ShapeDtypeStruct((n, d), F32),
        compiler_params=_params(),
        name="sample_attend",
    )(page_table, h, q8, k_new, v_new, bias, ck, cv, wo)


def _sgu_kernel(h_ref, g_ref, win_ref, bin_ref, gv_ref, ws_ref, bs_ref, wout_ref, o_ref, *rest,
                seg, emit_v):
    if emit_v:
        v_ref, gated_ref = rest
    else:
        (gated_ref,) = rest
    tm, d = h_ref.shape
    c_len = ws_ref.shape[1]
    d_sg = gv_ref.shape[1]
    gw = d_sg // SG_GROUPS
    x = h_ref[...]
    xn = _rms(x, g_ref[...]).astype(BF16)
    z = _dot(xn, win_ref[...]) + bin_ref[...]
    z = 0.5 * z * (1.0 + lax.erf(z * (0.5 ** 0.5)))
    u = z[:, :d_sg]
    v = _rms(z[:, d_sg:], gv_ref[...])
    if emit_v:
        v_ref[...] = v
    vb = v.astype(BF16)
    row = lax.broadcasted_iota(I32, (c_len, c_len), 0)
    col = lax.broadcasted_iota(I32, (c_len, c_len), 1)
    same_seq = (row >> _log2(seg)) == (col >> _log2(seg))
    if seg != c_len:
        src = lax.broadcasted_iota(I32, (ws_ref.shape[2], c_len), 0)
        dst = lax.broadcasted_iota(I32, (ws_ref.shape[2], c_len), 1)
        spread = jnp.where((dst & (seg - 1)) == src, 1.0, 0.0).astype(BF16)
    for g in range(SG_GROUPS):
        wsg = ws_ref[g] if seg == c_len else _dot(ws_ref[g].astype(BF16), spread)
        wg = jnp.where(col <= row, jnp.where(same_seq, wsg, 0.0), 0.0).astype(BF16)
        bg = bs_ref[:, g:g + 1]
        for ch in range(tm // c_len):
            rows = slice(ch * c_len, (ch + 1) * c_len)
            cols = slice(g * gw, (g + 1) * gw)
            mixed = _dot(wg, vb[rows, cols]) + bg
            gated_ref[rows, cols] = (u[rows, cols] * mixed).astype(BF16)
    o_ref[...] = x + _dot(gated_ref[...], wout_ref[...])


def _sgu_call(h, g, win, b_in, gv, ws, bs_t, wout, tm, seg, emit_v):
    n, d = h.shape
    d2 = win.shape[1]
    d_sg = d2 // 2
    c_len = ws.shape[1]
    out_shape = [jax.ShapeDtypeStruct((n, d), F32)]
    out_specs = [pl.BlockSpec((tm, d), lambda i: (i, 0))]
    if emit_v:
        out_shape.append(jax.ShapeDtypeStruct((n, d_sg), F32))
        out_specs.append(pl.BlockSpec((tm, d_sg), lambda i: (i, 0)))
    res = pl.pallas_call(
        functools.partial(_sgu_kernel, seg=seg, emit_v=emit_v),
        grid=(n // tm,),
        in_specs=[
            pl.BlockSpec((tm, d), lambda i: (i, 0)),
            pl.BlockSpec((1, d), lambda i: (0, 0)),
            pl.BlockSpec((d, d2), lambda i: (0, 0)),
            pl.BlockSpec((1, d2), lambda i: (0, 0)),
            pl.BlockSpec((1, d_sg), lambda i: (0, 0)),
            pl.BlockSpec(ws.shape, lambda i: (0, 0, 0)),
            pl.BlockSpec((c_len, SG_GROUPS), lambda i: (0, 0)),
            pl.BlockSpec((d_sg, d), lambda i: (0, 0)),
        ],
        out_specs=tuple(out_specs),
        out_shape=tuple(out_shape),
        scratch_shapes=[pltpu.VMEM((tm, d_sg), BF16)],
        compiler_params=_params(),
        name="sgu",
    )(h, g.reshape(1, d), win, b_in.reshape(1, d2), gv.reshape(1, d_sg), ws, bs_t, wout)
    return res


FFN_CHUNK = 256


def _ffn_body(x, p_ref, gf_ref, wup_ref, cw_ref, cb_ref, wdn_ref, gp_ref, wple_ref, wgate_ref, gfin_ref,
              o_ref, act_ref, shifted, emit_up, final_norm):
    d_ff = wdn_ref.shape[0]
    xn = _rms(x, gf_ref[...]).astype(BF16)

    def conv(cols):
        up = _dot(xn, wup_ref[:, cols])
        m1, m2 = shifted(up, cols)
        emit_up(up, cols)
        return cb_ref[:, cols] + cw_ref[0:1, cols] * m2 + cw_ref[1:2, cols] * m1 + cw_ref[2:3, cols] * up

    for c in range(d_ff // FFN_CHUNK):
        gate = conv(slice(c * FFN_CHUNK, (c + 1) * FFN_CHUNK))
        val = conv(slice(d_ff + c * FFN_CHUNK, d_ff + (c + 1) * FFN_CHUNK))
        act_ref[:, c * FFN_CHUNK:(c + 1) * FFN_CHUNK] = (gate * _sigmoid(gate) * val).astype(BF16)
    h2 = x + _dot(act_ref[...], wdn_ref[...])
    gate = _sigmoid(_dot(_rms(h2, gp_ref[...]).astype(BF16), wgate_ref[...]))
    h3 = h2 + _dot(p_ref[...].astype(BF16), wple_ref[...]) * gate
    o_ref[...] = _rms(h3, gfin_ref[...]) if final_norm else h3


def _ffn_prompt_kernel(*refs, tiles_per_seq, final_norm, with_attn):
    if with_attn:
        att_ref, wo_ref, *refs = refs
    (h_ref, p_ref, gf_ref, wup_ref, cw_ref, cb_ref, wdn_ref, gp_ref, wple_ref, wgate_ref, gfin_ref,
     o_ref, tail_ref, carry_ref, act_ref) = refs
    tm = h_ref.shape[0]
    i = pl.program_id(0)

    @pl.when(i % tiles_per_seq == 0)
    def _():
        carry_ref[...] = jnp.zeros(carry_ref.shape, F32)

    row = lax.broadcasted_iota(I32, (tm, FFN_CHUNK), 0)

    def shifted(up, cols):
        prev = carry_ref[:, cols]
        p1 = prev[SUBLANES - 1:SUBLANES]
        p2 = prev[SUBLANES - 2:SUBLANES - 1]
        m1 = jnp.where(row >= 1, pltpu.roll(up, 1, 0), p1)
        m2 = jnp.where(row >= 2, pltpu.roll(up, 2, 0), jnp.where(row == 0, p2, p1))
        return m1, m2

    def emit_up(up, cols):
        last = up[tm - SUBLANES:tm]
        carry_ref[:, cols] = last
        tail_ref[0, :, cols] = last

    x = h_ref[...]
    if with_attn:
        x = x + _dot(att_ref[...], wo_ref[...])
    _ffn_body(x, p_ref, gf_ref, wup_ref, cw_ref, cb_ref, wdn_ref, gp_ref, wple_ref, wgate_ref, gfin_ref,
              o_ref, act_ref, shifted, emit_up, final_norm)


def _ffn_sample_kernel(h_ref, p_ref, pm1_ref, pm2_ref, gf_ref, wup_ref, cw_ref, cb_ref, wdn_ref, gp_ref, wple_ref,
                       wgate_ref, gfin_ref, o_ref, up_ref, act_ref, *, seg, final_norm):
    tm = h_ref.shape[0]
    row = lax.broadcasted_iota(I32, (tm, FFN_CHUNK), 0) & ((1 << _log2(seg)) - 1)

    def shifted(up, cols):
        m1 = jnp.where(row >= 1, pltpu.roll(up, 1, 0), pm1_ref[:, cols])
        m2 = jnp.where(row >= 2, pltpu.roll(up, 2, 0), pm2_ref[:, cols])
        return m1, m2

    def emit_up(up, cols):
        up_ref[:, cols] = up

    _ffn_body(h_ref[...], p_ref, gf_ref, wup_ref, cw_ref, cb_ref, wdn_ref, gp_ref, wple_ref, wgate_ref, gfin_ref,
              o_ref, act_ref, shifted, emit_up, final_norm)


def _ffn_weight_specs(d, f2, d_ff, ple, layer):
    full = lambda shape: pl.BlockSpec(shape, lambda i: (0,) * len(shape))
    of_layer = lambda shape: pl.BlockSpec((None,) + shape, lambda i: (layer, 0, 0), pipeline_mode=pl.Buffered(1))
    return [of_layer((1, d)), of_layer((d, f2)), of_layer((CONV_W, f2)), of_layer((1, f2)), of_layer((d_ff, d)),
            of_layer((1, d)), of_layer((ple, d)), of_layer((d, d)), full((1, d))]


def _ffn_prompt_call(h, att, wo, p_all, layer, weights, tm, tiles_per_seq, final_norm):
    n, d = h.shape
    gf, wup, cw, cb, wdn, gp, wple, wgate, gfin = weights
    f2 = wup.shape[2]
    d_ff = wdn.shape[1]
    ple = p_all.shape[2]
    nt = n // tm
    with_attn = att is not None
    attn_specs = [pl.BlockSpec((tm, d), lambda i: (i, 0)), pl.BlockSpec((d, d), lambda i: (0, 0))] if with_attn else []
    attn_args = (att, wo) if with_attn else ()
    return pl.pallas_call(
        functools.partial(_ffn_prompt_kernel, tiles_per_seq=tiles_per_seq, final_norm=final_norm,
                          with_attn=with_attn),
        grid=(nt,),
        in_specs=attn_specs
        + [pl.BlockSpec((tm, d), lambda i: (i, 0)), pl.BlockSpec((None, tm, ple), lambda i: (layer, i, 0))]
        + _ffn_weight_specs(d, f2, d_ff, ple, layer),
        out_specs=(pl.BlockSpec((tm, d), lambda i: (i, 0)), pl.BlockSpec((1, SUBLANES, f2), lambda i: (i, 0, 0))),
        out_shape=(jax.ShapeDtypeStruct((n, d), F32), jax.ShapeDtypeStruct((nt, SUBLANES, f2), F32)),
        scratch_shapes=[pltpu.VMEM((SUBLANES, f2), F32), pltpu.VMEM((tm, d_ff), BF16)],
        compiler_params=_params(),
        name="ffn_prompt",
    )(*attn_args, h, p_all, gf, wup, cw, cb, wdn, gp, wple, wgate, gfin)


def _ffn_sample_call(h, p, pm1, pm2, layer, weights, seg, final_norm):
    n, d = h.shape
    gf, wup, cw, cb, wdn, gp, wple, wgate, gfin = weights
    f2 = wup.shape[2]
    d_ff = wdn.shape[1]
    ple = p.shape[2]
    full = lambda shape: pl.BlockSpec(shape, lambda i: (0,) * len(shape))
    of_layer = lambda shape: pl.BlockSpec((None,) + shape, lambda i: (layer, 0, 0))
    return pl.pallas_call(
        functools.partial(_ffn_sample_kernel, seg=seg, final_norm=final_norm),
        grid=(1,),
        in_specs=[full((n, d)), of_layer((n, ple)), of_layer((n, f2)), of_layer((n, f2))]
        + _ffn_weight_specs(d, f2, d_ff, ple, layer),
        out_specs=(full((n, d)), full((n, f2))),
        out_shape=(jax.ShapeDtypeStruct((n, d), F32), jax.ShapeDtypeStruct((n, f2), F32)),
        scratch_shapes=[pltpu.VMEM((n, d_ff), BF16)],
        compiler_params=_params(),
        name="ffn_sample",
    )(h, p, pm1, pm2, gf, wup, cw, cb, wdn, gp, wple, wgate, gfin)


PROMPT_TILE = 512


def kernel(x_prompt, x_sample, cache_k, cache_v, cache_idx_k, state_conv, page_table, p_prompt, p_sample,
           norm_mix, w_attn_in, w_attn_out, w_sg_in, b_sg_in, norm_sg_v, w_sg_spatial, b_sg_spatial, w_sg_out,
           norm_ffn, w_ffn_up, w_ffn_conv, b_ffn_conv, w_ffn_down, norm_ple, w_ple, w_ple_gate, norm_final):
    batch, seq, d = x_prompt.shape
    nb, ts, _ = x_sample.shape
    depth = norm_mix.shape[0]
    page = cache_k.shape[2]
    past = page_table.shape[1] * page
    f2 = w_ffn_up.shape[2]
    n_s = nb * ts
    tm = PROMPT_TILE

    hp = x_prompt.reshape(batch * seq, d)
    hs = x_sample.reshape(n_s, d)
    tabs_p = _rope_tables(jnp.arange(seq))
    tabs_s = tuple(jnp.tile(t, (nb, 1)) for t in _rope_tables(past + jnp.arange(ts)))
    zeros = jnp.zeros((depth, nb, ts - 1, f2), F32)
    pm1 = jnp.concatenate([state_conv[:, :, 1:2], zeros], axis=2).reshape(depth, n_s, f2)
    pm2 = jnp.concatenate([state_conv, zeros[:, :, 1:]], axis=2).reshape(depth, n_s, f2)
    ffn_weights = (norm_ffn.reshape(depth, 1, d), w_ffn_up.astype(BF16), w_ffn_conv, b_ffn_conv.reshape(depth, 1, f2),
                   w_ffn_down.astype(BF16), norm_ple.reshape(depth, 1, d), w_ple.astype(BF16),
                   w_ple_gate.astype(BF16), norm_final.reshape(1, d))

    kp_l, vp_l, ikp_l, ks_l, vs_l, iks_l, cp_l, cs_l, sgv_l = [], [], [], [], [], [], [], [], []
    for i in range(depth):
        j = i // 2
        att = w_out = None
        if i % 2 == 0:
            w_in = jnp.pad(w_attn_in[j], ((0, 0), (0, PROJ_W - w_attn_in.shape[2]))).astype(BF16)
            w_out = w_attn_out[j].astype(BF16)
            q8, k, v, kb, vt, iq4, ikt, misct, ikb = _proj_call(hp, norm_mix[i], w_in, tabs_p, tm, True)
            att = _dsa_prompt_call(q8, iq4, misct, ikb, kb, vt, batch, seq)
            kp_l.append(k.reshape(batch, seq, N_KV_HEADS, HEAD_DIM))
            vp_l.append(v.reshape(batch, seq, N_KV_HEADS, HEAD_DIM))
            ikp_l.append(jnp.swapaxes(ikt, 1, 2))
            q8, k, v, iq4, misc = _proj_call(hs, norm_mix[i], w_in, tabs_s, n_s, False)
            hs = _dsa_sample_call(hs, q8, iq4, misc, k, v, cache_k, cache_v, cache_idx_k, j,
                                  page_table, w_out, ts)
            ks_l.append(k.reshape(nb, ts, N_KV_HEADS, HEAD_DIM))
            vs_l.append(v.reshape(nb, ts, N_KV_HEADS, HEAD_DIM))
            iks_l.append(misc[:, :IDX_DIM].reshape(nb, ts, IDX_DIM))
        else:
            win = w_sg_in[j].astype(BF16)
            wout = w_sg_out[j].astype(BF16)
            (hp,) = _sgu_call(hp, norm_mix[i], win, b_sg_in[j], norm_sg_v[j], w_sg_spatial[j],
                              b_sg_spatial[j].T, wout, tm, SG_CHUNK, False)
            ws_s = jnp.tile(jnp.pad(w_sg_spatial[j][:, :ts, :ts], ((0, 0), (0, 0), (0, LANES - ts))), (1, nb, 1))
            bs_s = jnp.tile(b_sg_spatial[j].T[:ts], (nb, 1))
            hs, v_rows = _sgu_call(hs, norm_mix[i], win, b_sg_in[j], norm_sg_v[j], ws_s, bs_s, wout, n_s, ts, True)
            sgv_l.append(v_rows.reshape(nb, ts, -1))

        final = i == depth - 1
        weights = ffn_weights
        hp, tail = _ffn_prompt_call(hp, att, w_out, p_prompt.reshape(depth, batch * seq, -1), i, weights,
                                    tm, seq // tm, final)
        cp_l.append(tail.reshape(batch, seq // tm, SUBLANES, f2)[:, -1, SUBLANES - (CONV_W - 1):])
        hs, up_s = _ffn_sample_call(hs, p_sample.reshape(depth, n_s, -1), pm1, pm2, i, weights, ts, final)
        cs_l.append(up_s.reshape(nb, ts, f2)[:, ts - (CONV_W - 1):])

    return (hp.reshape(batch, seq, d), hs.reshape(nb, ts, d),
            jnp.stack(kp_l), jnp.stack(vp_l), jnp.stack(ikp_l),
            jnp.stack(ks_l), jnp.stack(vs_l), jnp.stack(iks_l),
            jnp.stack(cp_l), jnp.stack(cs_l), jnp.stack(sgv_l))
```

```python
import functools

import jax
import jax.numpy as jnp
from jax import lax
from jax.experimental import pallas as pl
from jax.experimental.pallas import tpu as pltpu

F32 = jnp.float32
BF16 = jnp.bfloat16
I32 = jnp.int32

N_HEADS = 8
N_KV_HEADS = 2
GROUP = N_HEADS // N_KV_HEADS
HEAD_DIM = 128
IDX_HEADS = 4
IDX_DIM = 64
IDX_W_SCALE = (IDX_HEADS * IDX_DIM) ** -0.5
TOPK_MAX = 256
SG_CHUNK = 128
SG_GROUPS = 8
CONV_W = 3
ROPE_THETA = 10000.0
EPS = 1e-6

LANES = 128
SUBLANES = 8
V7X_VMEM_BYTES = 64 * 1024 * 1024
VMEM_LIMIT_BYTES = V7X_VMEM_BYTES - 8 * 1024 * 1024

NEG_INF = float("-inf")
LOG2_E = 1.4426950408889634
INT_MIN = -(2 ** 31)


def _params(grid_rank=1):
    return pltpu.CompilerParams(dimension_semantics=("arbitrary",) * grid_rank,
                                vmem_limit_bytes=VMEM_LIMIT_BYTES)


def _log2(n):
    assert n > 0 and n & (n - 1) == 0, n
    return n.bit_length() - 1


def _rms(x, g):
    ms = jnp.mean(x * x, axis=-1, keepdims=True)
    return x * lax.rsqrt(ms + EPS) * g


def _sigmoid(x):
    return 1.0 / (1.0 + jnp.exp(-x))


def _dot(a, b):
    return jnp.dot(a, b, preferred_element_type=F32)


def _dot_nt(a, b):
    return lax.dot_general(a, b, (((1,), (1,)), ((), ())), preferred_element_type=F32)


def _sortable_key(score):
    b = pltpu.bitcast(score, I32)
    b = jnp.where(b == INT_MIN, 0, b)
    return b ^ ((b >> 31) & 0x7FFFFFFF)


Q_OFF = 0
K_OFF = N_HEADS * HEAD_DIM
V_OFF = K_OFF + N_KV_HEADS * HEAD_DIM
IQ_OFF = V_OFF + N_KV_HEADS * HEAD_DIM
MISC_OFF = IQ_OFF + IDX_HEADS * IDX_DIM
PROJ_W = MISC_OFF + LANES
KV_W = N_KV_HEADS * HEAD_DIM
ATTN_TILE = 256


def _project(h_ref, g_ref, w_ref, tab_refs, q8_ref, k_ref, v_ref, iq4_ref, queries_on_lanes):
    c128_ref, s128_ref, c64_ref, s64_ref, cm_ref, sm_ref = tab_refs
    tm = h_ref.shape[0]
    xn = _rms(h_ref[...], g_ref[...]).astype(BF16)
    y = _dot(xn, w_ref[...])
    c128 = c128_ref[...]
    s128 = s128_ref[...]

    def rope128(t):
        return t * c128 + pltpu.roll(t, HEAD_DIM // 2, 1) * s128

    for h in range(N_HEADS):
        qh = rope128(y[:, Q_OFF + h * HEAD_DIM:Q_OFF + (h + 1) * HEAD_DIM])
        q8_ref[h] = (qh.T if queries_on_lanes else qh).astype(q8_ref.dtype)
    ks = []
    for h in range(N_KV_HEADS):
        kh = rope128(y[:, K_OFF + h * HEAD_DIM:K_OFF + (h + 1) * HEAD_DIM])
        k_ref[pl.ds(h, tm, stride=N_KV_HEADS), :] = kh
        v_ref[pl.ds(h, tm, stride=N_KV_HEADS), :] = y[:, V_OFF + h * HEAD_DIM:V_OFF + (h + 1) * HEAD_DIM]
        ks.append(kh)
    v = y[:, V_OFF:IQ_OFF]

    lane = lax.broadcasted_iota(I32, (tm, LANES), 1)
    low_half = (lane & (IDX_DIM - 1)) < (IDX_DIM // 2)

    def rope64(t, c, s):
        partner = jnp.where(low_half, pltpu.roll(t, LANES - IDX_DIM // 2, 1), pltpu.roll(t, IDX_DIM // 2, 1))
        return t * c + partner * s

    c64 = c64_ref[...]
    s64 = s64_ref[...]
    for pair in range(IDX_HEADS // 2):
        t = rope64(y[:, IQ_OFF + pair * LANES:IQ_OFF + (pair + 1) * LANES], c64, s64)
        if queries_on_lanes:
            tt = t.T
            iq4_ref[2 * pair] = tt[:IDX_DIM, :].astype(iq4_ref.dtype)
            iq4_ref[2 * pair + 1] = tt[IDX_DIM:, :].astype(iq4_ref.dtype)
        else:
            iq4_ref[2 * pair] = t[:, :IDX_DIM].astype(iq4_ref.dtype)
            iq4_ref[2 * pair + 1] = t[:, IDX_DIM:].astype(iq4_ref.dtype)
    m = rope64(y[:, MISC_OFF:MISC_OFF + LANES], cm_ref[...], sm_ref[...])
    return ks, v, m


def _proj_prompt_kernel(h_ref, g_ref, w_ref, c128_ref, s128_ref, c64_ref, s64_ref, cm_ref, sm_ref,
                        q8_ref, k_ref, v_ref, kb_ref, vt_ref, iq4_ref, ikt_ref, misct_ref, ikb_ref):
    tm = h_ref.shape[0]
    ks, v, m = _project(h_ref, g_ref, w_ref, (c128_ref, s128_ref, c64_ref, s64_ref, cm_ref, sm_ref),
                        q8_ref, k_ref, v_ref, iq4_ref, True)
    for h in range(N_KV_HEADS):
        kb_ref[:, h * HEAD_DIM:(h + 1) * HEAD_DIM] = ks[h].astype(BF16)
    for c in range(tm // ATTN_TILE):
        vt_ref[c] = v[c * ATTN_TILE:(c + 1) * ATTN_TILE].T.astype(BF16)
    mt = m.T
    misct_ref[...] = mt
    ikt_ref[0] = mt[:IDX_DIM, :]
    ikb_ref[...] = m[:, :IDX_DIM].astype(BF16)


def _proj_sample_kernel(h_ref, g_ref, w_ref, c128_ref, s128_ref, c64_ref, s64_ref, cm_ref, sm_ref,
                        q8_ref, k_ref, v_ref, iq4_ref, misc_ref):
    _, _, m = _project(h_ref, g_ref, w_ref, (c128_ref, s128_ref, c64_ref, s64_ref, cm_ref, sm_ref),
                       q8_ref, k_ref, v_ref, iq4_ref, False)
    misc_ref[...] = m


def _proj_call(h, g, w, tabs, tm, prompt):
    n, d = h.shape
    t_tab = tabs[0].shape[0]
    n_tab = t_tab // tm
    tab_spec = pl.BlockSpec((tm, LANES), lambda i: (i % n_tab, 0))
    row = lambda w_: pl.BlockSpec((tm, w_), lambda i: (i, 0))
    if prompt:
        q8 = (jax.ShapeDtypeStruct((N_HEADS, HEAD_DIM, n), BF16), pl.BlockSpec((N_HEADS, HEAD_DIM, tm), lambda i: (0, 0, i)))
        iq4 = (jax.ShapeDtypeStruct((IDX_HEADS, IDX_DIM, n), BF16), pl.BlockSpec((IDX_HEADS, IDX_DIM, tm), lambda i: (0, 0, i)))
    else:
        q8 = (jax.ShapeDtypeStruct((N_HEADS, n, HEAD_DIM), F32), pl.BlockSpec((N_HEADS, tm, HEAD_DIM), lambda i: (0, i, 0)))
        iq4 = (jax.ShapeDtypeStruct((IDX_HEADS, n, IDX_DIM), F32), pl.BlockSpec((IDX_HEADS, tm, IDX_DIM), lambda i: (0, i, 0)))
    kf = (jax.ShapeDtypeStruct((n * N_KV_HEADS, HEAD_DIM), F32),
          pl.BlockSpec((tm * N_KV_HEADS, HEAD_DIM), lambda i: (i, 0)))
    if prompt:
        assert tm % ATTN_TILE == 0
        outs = [q8, kf, kf, (jax.ShapeDtypeStruct((n, KV_W), BF16), row(KV_W)),
                (jax.ShapeDtypeStruct((n // ATTN_TILE, KV_W, ATTN_TILE), BF16),
                 pl.BlockSpec((tm // ATTN_TILE, KV_W, ATTN_TILE), lambda i: (i, 0, 0))),
                iq4,
                (jax.ShapeDtypeStruct((n // t_tab, IDX_DIM, t_tab), F32),
                 pl.BlockSpec((1, IDX_DIM, tm), lambda i: (i // n_tab, 0, i % n_tab))),
                (jax.ShapeDtypeStruct((LANES, n), F32), pl.BlockSpec((LANES, tm), lambda i: (0, i))),
                (jax.ShapeDtypeStruct((n, IDX_DIM), BF16), row(IDX_DIM))]
        body = _proj_prompt_kernel
    else:
        outs = [q8, kf, kf, iq4, (jax.ShapeDtypeStruct((n, LANES), F32), row(LANES))]
        body = _proj_sample_kernel
    return pl.pallas_call(
        body,
        grid=(n // tm,),
        in_specs=[
            pl.BlockSpec((tm, d), lambda i: (i, 0)),
            pl.BlockSpec((1, d), lambda i: (0, 0)),
            pl.BlockSpec((d, PROJ_W), lambda i: (0, 0)),
        ] + [tab_spec] * 6,
        out_specs=tuple(o[1] for o in outs),
        out_shape=tuple(o[0] for o in outs),
        compiler_params=_params(),
        name="attn_proj_prompt" if prompt else "attn_proj_sample",
    )(h, g.reshape(1, d), w, *tabs)


def _rope_tables(pos):
    t = pos.shape[0]

    def tab(half):
        inv = ROPE_THETA ** (-jnp.arange(half, dtype=F32) / half)
        ang = pos.astype(F32)[:, None] * inv[None, :]
        return jnp.cos(ang), jnp.sin(ang)

    c64, s64 = tab(HEAD_DIM // 2)
    c32, s32 = tab(IDX_DIM // 2)
    c128 = jnp.concatenate([c64, c64], axis=1)
    s128 = jnp.concatenate([-s64, s64], axis=1)
    cq = jnp.concatenate([c32, c32, c32, c32], axis=1)
    sq = jnp.concatenate([-s32, s32, -s32, s32], axis=1)
    cm = jnp.concatenate([c32, c32, jnp.full((t, LANES - IDX_DIM), IDX_W_SCALE, F32)], axis=1)
    sm = jnp.concatenate([-s32, s32, jnp.zeros((t, LANES - IDX_DIM), F32)], axis=1)
    return (c128, s128, cq, sq, cm, sm)


COUNT_SLAB = 64


def _col_count(key_ref, n_rows, indicator):
    w = key_ref.shape[1]
    acc = jnp.zeros((COUNT_SLAB, w), I32)
    for c in range(n_rows // COUNT_SLAB):
        acc = acc + indicator(key_ref[c * COUNT_SLAB:(c + 1) * COUNT_SLAB, :], c * COUNT_SLAB)
    return jnp.sum(acc, axis=0, keepdims=True)


I16 = jnp.int16
I16_MIN = -(2 ** 15)
HALF_SLAB = 128
PACK_ROWS = 16


def _bcast16(x, rows):
    one = jnp.broadcast_to(x, (PACK_ROWS, x.shape[1])).astype(I16)
    return jnp.concatenate([one] * (rows // PACK_ROWS), axis=0)


def _col_count16(half_ref, n_rows, indicator):
    w = half_ref.shape[1]
    acc = jnp.zeros((HALF_SLAB, w), I16)
    for c in range(n_rows // HALF_SLAB):
        acc = acc + indicator(half_ref[c * HALF_SLAB:(c + 1) * HALF_SLAB, :])
    return jnp.sum(acc.astype(I32), axis=0, keepdims=True)


def _kth_largest16(half_ref, n_rows, kk):
    w = half_ref.shape[1]
    one = jnp.ones((HALF_SLAB, w), I16)
    zero = jnp.zeros((HALF_SLAB, w), I16)

    def body(i, t):
        cand = t + lax.shift_left(jnp.int32(1), 15 - i)
        candb = _bcast16(cand, HALF_SLAB)
        cnt = _col_count16(half_ref, n_rows, lambda k: jnp.where(k >= candb, one, zero))
        return jnp.where(cnt >= kk, cand, t)

    return lax.fori_loop(0, 16, body, jnp.full((1, w), I16_MIN, I32))


def _select_bias_cols(key_ref, hi_ref, lo_ref, bias_ref, n_rows, kq):
    w = key_ref.shape[1]
    one = jnp.ones((HALF_SLAB, w), I16)
    zero = jnp.zeros((HALF_SLAB, w), I16)

    t_hi = _kth_largest16(hi_ref, n_rows, kq)
    t_hib = _bcast16(t_hi, HALF_SLAB)
    above = _col_count16(hi_ref, n_rows, lambda k: jnp.where(k > t_hib, one, zero))
    lowest = jnp.full((HALF_SLAB, w), I16_MIN, I16)
    for c in range(n_rows // HALF_SLAB):
        rows = slice(c * HALF_SLAB, (c + 1) * HALF_SLAB)
        lo_ref[rows, :] = jnp.where(hi_ref[rows, :] == t_hib, lo_ref[rows, :], lowest)
    t_lo = _kth_largest16(lo_ref, n_rows, kq - above)
    thr = t_hi * 65536 + (t_lo - I16_MIN)

    cnt_gt = _col_count(key_ref, n_rows, lambda k, r0: jnp.where(k > thr, 1, 0))
    cnt_ge = _col_count(key_ref, n_rows, lambda k, r0: jnp.where(k >= thr, 1, 0))
    need = kq - cnt_gt

    drop_ties = jnp.max(cnt_ge - kq) > 0

    @pl.when(jnp.logical_not(drop_ties))
    def _():
        for c in range(n_rows // COUNT_SLAB):
            rows = slice(c * COUNT_SLAB, (c + 1) * COUNT_SLAB)
            bias_ref[rows, :] = jnp.where(key_ref[rows, :] >= thr, 0.0, NEG_INF)

    @pl.when(drop_ties)
    def _():
        nbits = max(1, (n_rows - 1).bit_length())
        sub = lax.broadcasted_iota(I32, (COUNT_SLAB, w), 0)

        def lt_body(i, j):
            cand = j + lax.shift_left(jnp.int32(1), nbits - 1 - i)
            cnt = _col_count(key_ref, n_rows,
                             lambda k, r0: jnp.where(k == thr, jnp.where(sub + r0 < cand, 1, 0), 0))
            return jnp.where(cnt < need, cand, j)

        jmax = lax.fori_loop(0, nbits, lt_body, jnp.zeros((1, w), I32))
        for c in range(n_rows // COUNT_SLAB):
            rows = slice(c * COUNT_SLAB, (c + 1) * COUNT_SLAB)
            k = key_ref[rows, :]
            keep_tie = jnp.where(sub + c * COUNT_SLAB <= jmax, 0.0, NEG_INF)
            bias_ref[rows, :] = jnp.where(k > thr, 0.0, jnp.where(k == thr, keep_tie, NEG_INF))


def _for_chunks(n, body):
    def pair(p, carry):
        body([2 * p, 2 * p + 1])
        return carry

    lax.fori_loop(0, lax.shift_right_logical(n, 1), pair, 0)

    @pl.when((n & 1) == 1)
    def _():
        body([n - 1])


def _dsa_prompt_kernel(q8_ref, iq4_ref, misct_ref, ik_ref, k_ref, vt_ref, att_ref,
                       key_ref, hi_ref, lo_ref, bias_ref, logit_ref, acc_ref, stat_ref, *, k_sel):
    t = q8_ref.shape[2]
    nq = key_ref.shape[0] // t
    gw = GROUP * t
    qi = pl.program_id(1)
    n_chunks = qi + 1
    qpos = qi * t + lax.broadcasted_iota(I32, (1, t), 1)
    rows_of = lambda c: pl.ds(pl.multiple_of(c * t, t), t)

    iq = jnp.concatenate([iq4_ref[h] for h in range(IDX_HEADS)], axis=1)
    iw = misct_ref[IDX_DIM:IDX_DIM + SUBLANES, :]

    def score_chunks(cs):
        for c in cs:
            s = _dot(ik_ref[rows_of(c), :], iq)
            score = jnp.zeros((t, t), F32)
            for h in range(IDX_HEADS):
                score = score + jnp.maximum(s[:, h * t:(h + 1) * t], 0.0) * iw[h:h + 1, :]
            kpos = c * t + lax.broadcasted_iota(I32, (t, t), 0)
            key = _sortable_key(jnp.where(kpos <= qpos, score, NEG_INF))
            key_ref[rows_of(c), :] = key
            hi_ref[rows_of(c), :] = (key >> 16).astype(I16)
            lo_ref[rows_of(c), :] = ((key & 0xFFFF) + I16_MIN).astype(I16)

    _for_chunks(n_chunks, score_chunks)

    kq = jnp.minimum(k_sel, qpos + 1)
    for j in range(nq):
        @pl.when(qi == j)
        def _():
            _select_bias_cols(key_ref, hi_ref, lo_ref, bias_ref, (j + 1) * t, kq)

    qs = [jnp.concatenate([q8_ref[g * GROUP + hh] for hh in range(GROUP)], axis=1)
          for g in range(N_KV_HEADS)]
    head_cols = lambda g: slice(g * HEAD_DIM, (g + 1) * HEAD_DIM)
    lanes_of = lambda g: slice(g * gw, (g + 1) * gw)
    stat_ref[0:1, :] = jnp.full((1, N_KV_HEADS * gw), NEG_INF, F32)
    stat_ref[1:2, :] = jnp.zeros((1, N_KV_HEADS * gw), F32)
    acc_ref[...] = jnp.zeros(acc_ref.shape, F32)

    def logits_chunks(cs):
        for g in range(N_KV_HEADS):
            m = stat_ref[0:1, lanes_of(g)]
            for c in cs:
                lg = _dot(k_ref[rows_of(c), head_cols(g)], qs[g]) * (HEAD_DIM ** -0.5 * LOG2_E)
                lg = lg + jnp.concatenate([bias_ref[rows_of(c), :]] * GROUP, axis=1)
                logit_ref[rows_of(c), lanes_of(g)] = lg
                m = jnp.maximum(m, jnp.max(lg, axis=0, keepdims=True))
            stat_ref[0:1, lanes_of(g)] = m

    _for_chunks(n_chunks, logits_chunks)

    def pv_chunks(cs):
        for g in range(N_KV_HEADS):
            m = stat_ref[0:1, lanes_of(g)]
            den = stat_ref[1:2, lanes_of(g)]
            for c in cs:
                e = jnp.exp2(logit_ref[rows_of(c), lanes_of(g)] - m)
                acc_ref[g] += _dot(vt_ref[c, head_cols(g), :], e.astype(BF16))
                den = den + jnp.sum(e, axis=0, keepdims=True)
            stat_ref[1:2, lanes_of(g)] = den

    _for_chunks(n_chunks, pv_chunks)

    for g in range(N_KV_HEADS):
        o = acc_ref[g] / stat_ref[1:2, lanes_of(g)]
        for hh in range(GROUP):
            head = g * GROUP + hh
            att_ref[:, head * HEAD_DIM:(head + 1) * HEAD_DIM] = o[:, hh * t:(hh + 1) * t].T.astype(BF16)


def _dsa_prompt_call(q8, iq4, misct, ikb, kb, vt, batch, seq):
    t = ATTN_TILE
    n = batch * seq
    d = N_HEADS * HEAD_DIM
    nq = seq // t
    k_sel = min(TOPK_MAX, seq // 4)
    return pl.pallas_call(
        functools.partial(_dsa_prompt_kernel, k_sel=k_sel),
        grid=(batch, nq),
        in_specs=[
            pl.BlockSpec((N_HEADS, HEAD_DIM, t), lambda b, q: (0, 0, b * nq + q)),
            pl.BlockSpec((IDX_HEADS, IDX_DIM, t), lambda b, q: (0, 0, b * nq + q)),
            pl.BlockSpec((LANES, t), lambda b, q: (0, b * nq + q)),
            pl.BlockSpec((seq, IDX_DIM), lambda b, q: (b, 0)),
            pl.BlockSpec((seq, KV_W), lambda b, q: (b, 0)),
            pl.BlockSpec((nq, KV_W, t), lambda b, q: (b, 0, 0)),
        ],
        out_specs=pl.BlockSpec((t, d), lambda b, q: (b * nq + q, 0)),
        out_shape=jax.ShapeDtypeStruct((n, d), BF16),
        scratch_shapes=[
            pltpu.VMEM((seq, t), I32),
            pltpu.VMEM((seq, t), I16),
            pltpu.VMEM((seq, t), I16),
            pltpu.VMEM((seq, t), F32),
            pltpu.VMEM((seq, N_KV_HEADS * GROUP * t), F32),
            pltpu.VMEM((N_KV_HEADS, HEAD_DIM, GROUP * t), F32),
            pltpu.VMEM((SUBLANES, N_KV_HEADS * GROUP * t), F32),
        ],
        compiler_params=_params(grid_rank=2),
        name="dsa_prompt",
    )(q8, iq4, misct, ikb, kb, vt)


def _row_count(key_ref, indicator):
    r, l = key_ref.shape
    acc = jnp.zeros((r, LANES), I32)
    for c in range(l // LANES):
        acc = acc + indicator(key_ref[:, c * LANES:(c + 1) * LANES], c)
    return jnp.sum(acc, axis=-1, keepdims=True)


ROW_COUNT_CHAINS = 4


def _row_count16(half_ref, indicator):
    r, l = half_ref.shape
    accs = [jnp.zeros((r, LANES), I16) for _ in range(ROW_COUNT_CHAINS)]
    for c in range(l // LANES):
        accs[c % ROW_COUNT_CHAINS] = accs[c % ROW_COUNT_CHAINS] + indicator(half_ref[:, c * LANES:(c + 1) * LANES])
    acc = accs[0]
    for a in accs[1:]:
        acc = acc + a
    return jnp.sum(acc.astype(I32), axis=-1, keepdims=True)


def _kth_largest16_rows(half_ref, kk):
    r = half_ref.shape[0]
    one = jnp.ones((r, LANES), I16)
    zero = jnp.zeros((r, LANES), I16)

    def body(i, t):
        cand = t + lax.shift_left(jnp.int32(1), 15 - i)
        candb = jnp.broadcast_to(cand, (r, LANES)).astype(I16)
        cnt = _row_count16(half_ref, lambda k: jnp.where(k >= candb, one, zero))
        return jnp.where(cnt >= kk, cand, t)

    return lax.fori_loop(0, 16, body, jnp.full((r, 1), I16_MIN, I32))


def _select_bias_rows(key_ref, hi_ref, lo_ref, bias_ref, jmax_ref, kq):
    r, l = key_ref.shape
    one = jnp.ones((r, LANES), I16)
    zero = jnp.zeros((r, LANES), I16)
    chunks = [slice(c * LANES, (c + 1) * LANES) for c in range(l // LANES)]

    for ch in chunks:
        k = key_ref[:, ch]
        hi_ref[:, ch] = (k >> 16).astype(I16)
        lo_ref[:, ch] = ((k & 0xFFFF) + I16_MIN).astype(I16)
    t_hi = _kth_largest16_rows(hi_ref, kq)
    t_hib = jnp.broadcast_to(t_hi, (r, LANES)).astype(I16)
    above = _row_count16(hi_ref, lambda k: jnp.where(k > t_hib, one, zero))
    lowest = jnp.full((r, LANES), I16_MIN, I16)
    for ch in chunks:
        lo_ref[:, ch] = jnp.where(hi_ref[:, ch] == t_hib, lo_ref[:, ch], lowest)
    t_lo = _kth_largest16_rows(lo_ref, kq - above)
    thr = t_hi * 65536 + (t_lo - I16_MIN)
    thrb = jnp.broadcast_to(thr, (r, LANES))
    cnt_gt = _row_count(key_ref, lambda k, c: jnp.where(k > thrb, 1, 0))
    cnt_ge = _row_count(key_ref, lambda k, c: jnp.where(k >= thrb, 1, 0))
    need = kq - cnt_gt

    jmax_ref[...] = jnp.full(jmax_ref.shape, l, I32)
    lane = lax.broadcasted_iota(I32, (r, LANES), 1)

    @pl.when(jnp.max(cnt_ge - kq) > 0)
    def _():
        nbits = max(1, (l - 1).bit_length())

        def lt_body(i, j):
            cand = j + lax.shift_left(jnp.int32(1), nbits - 1 - i)
            candb = jnp.broadcast_to(cand, (r, LANES))
            cnt = _row_count(key_ref, lambda k, c: jnp.where(k == thrb, jnp.where(lane + c * LANES < candb, 1, 0), 0))
            return jnp.where(cnt < need, cand, j)

        jmax_ref[...] = jnp.broadcast_to(lax.fori_loop(0, nbits, lt_body, jnp.zeros((r, 1), I32)), jmax_ref.shape)

    jmaxb = jmax_ref[...]
    for c in range(l // LANES):
        k = key_ref[:, c * LANES:(c + 1) * LANES]
        keep_tie = jnp.where(lane + c * LANES <= jmaxb, 0.0, NEG_INF)
        bias_ref[:, c * LANES:(c + 1) * LANES] = jnp.where(k > thrb, 0.0, jnp.where(k == thrb, keep_tie, NEG_INF))


SELECT_SEQS = 8
ATTEND_SEQS = 2


def _sample_select_kernel(pt_ref, iq4_ref, misc_ref, cikt_ref, bias_ref, iktbuf, iknew, key_ref, hi_ref, lo_ref,
                          jmax_ref, sem, *, k_sel, layer, ts):
    page = cikt_ref.shape[3]
    n_pages = pt_ref.shape[1]
    past = n_pages * page
    l_pad = past + LANES
    seqs = iktbuf.shape[0]
    step = pl.program_id(0)

    def page_copy(i, p):
        lanes = pl.ds(pl.multiple_of(p * page, page), page)
        return pltpu.make_async_copy(cikt_ref.at[layer, pt_ref[step * seqs + i, p]], iktbuf.at[i, :, lanes], sem.at[0])

    for i in range(seqs):
        lax.fori_loop(0, n_pages, lambda p, c, i=i: (page_copy(i, p).start(), c)[1], 0)

    iknew[...] = jnp.zeros(iknew.shape, F32)
    for i in range(seqs):
        iknew[i, 0:ts, :] = misc_ref[i * ts:(i + 1) * ts, :IDX_DIM]

    for i in range(seqs):
        lax.fori_loop(0, n_pages, lambda p, c, i=i: (page_copy(i, p).wait(), c)[1], 0)

    qpos = past + lax.broadcasted_iota(I32, (ts, 1), 0)
    kpos = lax.broadcasted_iota(I32, (ts, l_pad), 1)
    for i in range(seqs):
        rows = slice(i * ts, (i + 1) * ts)
        iq = iq4_ref[:, rows, :].reshape(IDX_HEADS * ts, IDX_DIM).astype(BF16)
        s = jnp.concatenate([_dot(iq, iktbuf[i].astype(BF16)), _dot_nt(iq, iknew[i].astype(BF16))], axis=1)
        iw = misc_ref[rows, IDX_DIM:IDX_DIM + IDX_HEADS]
        score = jnp.zeros((ts, l_pad), F32)
        for h in range(IDX_HEADS):
            score = score + jnp.maximum(s[h * ts:(h + 1) * ts], 0.0) * iw[:, h:h + 1]
        key_ref[rows, :] = _sortable_key(jnp.where(kpos <= qpos, score, NEG_INF))
    kq = jnp.minimum(k_sel, jnp.concatenate([qpos] * seqs, axis=0) + 1)
    _select_bias_rows(key_ref, hi_ref, lo_ref, bias_ref, jmax_ref, kq)


def _sample_attend_kernel(pt_ref, h_ref, q8_ref, kn_ref, vn_ref, bias_ref, ck_ref, cv_ref, wo_ref, o_ref,
                          kbuf, vbuf, knew, vnew, att_ref, sems, *, layer, ts):
    page_rows = ck_ref.shape[2]
    n_pages = pt_ref.shape[1]
    past = n_pages * page_rows // N_KV_HEADS
    seqs = kbuf.shape[0]
    step = pl.program_id(0)
    n_steps = pl.num_programs(0)

    first = step * seqs

    def page_copies(seq, slot, p):
        src = pt_ref[seq, p]
        rows = pl.ds(pl.multiple_of(p * page_rows, page_rows), page_rows)
        return (pltpu.make_async_copy(ck_ref.at[layer, src], kbuf.at[slot, rows], sems.at[0, slot]),
                pltpu.make_async_copy(cv_ref.at[layer, src], vbuf.at[slot, rows], sems.at[1, slot]))

    def start_seq(seq, slot):
        def body(p, c):
            for cp in page_copies(seq, slot, p):
                cp.start()
            return c
        lax.fori_loop(0, n_pages, body, 0)

    def wait_seq(seq, slot):
        def body(p, c):
            for cp in page_copies(seq, slot, p):
                cp.wait()
            return c
        lax.fori_loop(0, n_pages, body, 0)

    @pl.when(step == 0)
    def _():
        start_seq(first, 0)

    pad = jnp.zeros((LANES - ts, KV_W), F32)
    for i in range(seqs):
        rows = slice(i * ts, (i + 1) * ts)
        if i + 1 < seqs:
            start_seq(first + i + 1, i + 1)
        else:
            @pl.when(step + 1 < n_steps)
            def _():
                start_seq(first + seqs, 0)
        wait_seq(first + i, i)
        for g in range(N_KV_HEADS):
            new_rows = pl.ds(i * ts * N_KV_HEADS + g, ts, stride=N_KV_HEADS)
            knew[0:ts, g * HEAD_DIM:(g + 1) * HEAD_DIM] = kn_ref[new_rows, :]
            vnew[0:ts, g * HEAD_DIM:(g + 1) * HEAD_DIM] = vn_ref[new_rows, :]
        knew[ts:, :] = pad
        vnew[ts:, :] = pad
        bias = jnp.concatenate([bias_ref[rows, :]] * GROUP, axis=0)
        row0 = pl.multiple_of((step * seqs + i) * ts, ts)
        for g in range(N_KV_HEADS):
            cols = slice(g * HEAD_DIM, (g + 1) * HEAD_DIM)
            k_past = kbuf[i, pl.ds(g, past, stride=N_KV_HEADS), :].astype(BF16)
            v_past = vbuf[i, pl.ds(g, past, stride=N_KV_HEADS), :].astype(BF16)
            q = q8_ref[g * GROUP:(g + 1) * GROUP, rows, :].reshape(GROUP * ts, HEAD_DIM).astype(BF16)
            logits = jnp.concatenate([_dot_nt(q, k_past), _dot_nt(q, knew[:, cols].astype(BF16))], axis=1)
            logits = logits * (HEAD_DIM ** -0.5) + bias
            m = jnp.max(logits, axis=-1, keepdims=True)
            e = jnp.exp(logits - m)
            den = jnp.sum(e, axis=-1, keepdims=True)
            eb = e.astype(BF16)
            o = (_dot(eb[:, :past], v_past) + _dot(eb[:, past:], vnew[:, cols].astype(BF16))) / den
            for hh in range(GROUP):
                head = g * GROUP + hh
                att_ref[pl.ds(row0, ts), head * HEAD_DIM:(head + 1) * HEAD_DIM] = o[hh * ts:(hh + 1) * ts]

    @pl.when(step == n_steps - 1)
    def _():
        o_ref[...] = h_ref[...] + _dot(att_ref[...].astype(BF16), wo_ref[...])


def _dsa_sample_call(h, q8, iq4, misc, k_new, v_new, cache_k, cache_v, cache_ik, layer, page_table, wo, ts):
    n, d = h.shape
    nb = n // ts
    n_layers, n_pool, page = cache_k.shape[:3]
    n_pages = page_table.shape[1]
    past = n_pages * page
    l_pad = past + LANES
    k_sel = min(TOPK_MAX, (past + ts) // 4)
    ck = cache_k.reshape(n_layers, n_pool, page * N_KV_HEADS, HEAD_DIM)
    cv = cache_v.reshape(n_layers, n_pool, page * N_KV_HEADS, HEAD_DIM)
    cikt = jnp.swapaxes(cache_ik, 2, 3)
    sel_rows = SELECT_SEQS * ts
    bias = pl.pallas_call(
        functools.partial(_sample_select_kernel, k_sel=k_sel, layer=layer, ts=ts),
        grid_spec=pltpu.PrefetchScalarGridSpec(
            num_scalar_prefetch=1,
            grid=(nb // SELECT_SEQS,),
            in_specs=[
                pl.BlockSpec((IDX_HEADS, sel_rows, IDX_DIM), lambda s, pt: (0, s, 0)),
                pl.BlockSpec((sel_rows, LANES), lambda s, pt: (s, 0)),
                pl.BlockSpec(memory_space=pl.ANY),
            ],
            out_specs=pl.BlockSpec((sel_rows, l_pad), lambda s, pt: (s, 0)),
            scratch_shapes=[
                pltpu.VMEM((SELECT_SEQS, IDX_DIM, past), F32),
                pltpu.VMEM((SELECT_SEQS, LANES, IDX_DIM), F32),
                pltpu.VMEM((sel_rows, l_pad), I32),
                pltpu.VMEM((sel_rows, l_pad), I16),
                pltpu.VMEM((sel_rows, l_pad), I16),
                pltpu.VMEM((sel_rows, LANES), I32),
                pltpu.SemaphoreType.DMA((1,)),
            ],
        ),
        out_shape=jax.ShapeDtypeStruct((n, l_pad), F32),
        compiler_params=_params(),
        name="sample_select",
    )(page_table, iq4, misc, cikt)

    att_rows = ATTEND_SEQS * ts
    return pl.pallas_call(
        functools.partial(_sample_attend_kernel, layer=layer, ts=ts),
        grid_spec=pltpu.PrefetchScalarGridSpec(
            num_scalar_prefetch=1,
            grid=(nb // ATTEND_SEQS,),
            in_specs=[
                pl.BlockSpec((n, d), lambda s, pt: (0, 0)),
                pl.BlockSpec((N_HEADS, att_rows, HEAD_DIM), lambda s, pt: (0, s, 0)),
                pl.BlockSpec((att_rows * N_KV_HEADS, HEAD_DIM), lambda s, pt: (s, 0)),
                pl.BlockSpec((att_rows * N_KV_HEADS, HEAD_DIM), lambda s, pt: (s, 0)),
                pl.BlockSpec((att_rows, l_pad), lambda s, pt: (s, 0)),
                pl.BlockSpec(memory_space=pl.ANY),
                pl.BlockSpec(memory_space=pl.ANY),
                pl.BlockSpec((d, d), lambda s, pt: (0, 0)),
            ],
            out_specs=pl.BlockSpec((n, d), lambda s, pt: (0, 0)),
            scratch_shapes=[
                pltpu.VMEM((ATTEND_SEQS, past * N_KV_HEADS, HEAD_DIM), F32),
                pltpu.VMEM((ATTEND_SEQS, past * N_KV_HEADS, HEAD_DIM), F32),
                pltpu.VMEM((LANES, KV_W), F32),
                pltpu.VMEM((LANES, KV_W), F32),
                pltpu.VMEM((n, d), F32),
                pltpu.SemaphoreType.DMA((2, ATTEND_SEQS)),
            ],
        ),
        out_shape=jax.ShapeDtypeStruct((n, d), F32),
        compiler_params=_params(),
        name="sample_attend",
    )(page_table, h, q8, k_new, v_new, bias, ck, cv, wo)


def _sgu_kernel(h_ref, g_ref, win_ref, bin_ref, gv_ref, ws_ref, bs_ref, wout_ref, o_ref, *rest,
                seg, emit_v):
    if emit_v:
        v_ref, gated_ref = rest
    else:
        (gated_ref,) = rest
    tm, d = h_ref.shape
    c_len = ws_ref.shape[1]
    d_sg = gv_ref.shape[1]
    gw = d_sg // SG_GROUPS
    x = h_ref[...]
    xn = _rms(x, g_ref[...]).astype(BF16)
    z = _dot(xn, win_ref[...]) + bin_ref[...]
    z = 0.5 * z * (1.0 + lax.erf(z * (0.5 ** 0.5)))
    u = z[:, :d_sg]
    v = _rms(z[:, d_sg:], gv_ref[...])
    if emit_v:
        v_ref[...] = v
    vb = v.astype(BF16)
    row = lax.broadcasted_iota(I32, (c_len, c_len), 0)
    col = lax.broadcasted_iota(I32, (c_len, c_len), 1)
    same_seq = (row >> _log2(seg)) == (col >> _log2(seg))
    if seg != c_len:
        src = lax.broadcasted_iota(I32, (ws_ref.shape[2], c_len), 0)
        dst = lax.broadcasted_iota(I32, (ws_ref.shape[2], c_len), 1)
        spread = jnp.where((dst & (seg - 1)) == src, 1.0, 0.0).astype(BF16)
    for g in range(SG_GROUPS):
        wsg = ws_ref[g] if seg == c_len else _dot(ws_ref[g].astype(BF16), spread)
        wg = jnp.where(col <= row, jnp.where(same_seq, wsg, 0.0), 0.0).astype(BF16)
        bg = bs_ref[:, g:g + 1]
        for ch in range(tm // c_len):
            rows = slice(ch * c_len, (ch + 1) * c_len)
            cols = slice(g * gw, (g + 1) * gw)
            mixed = _dot(wg, vb[rows, cols]) + bg
            gated_ref[rows, cols] = (u[rows, cols] * mixed).astype(BF16)
    o_ref[...] = x + _dot(gated_ref[...], wout_ref[...])


def _sgu_call(h, g, win, b_in, gv, ws, bs_t, wout, tm, seg, emit_v):
    n, d = h.shape
    d2 = win.shape[1]
    d_sg = d2 // 2
    c_len = ws.shape[1]
    out_shape = [jax.ShapeDtypeStruct((n, d), F32)]
    out_specs = [pl.BlockSpec((tm, d), lambda i: (i, 0))]
    if emit_v:
        out_shape.append(jax.ShapeDtypeStruct((n, d_sg), F32))
        out_specs.append(pl.BlockSpec((tm, d_sg), lambda i: (i, 0)))
    res = pl.pallas_call(
        functools.partial(_sgu_kernel, seg=seg, emit_v=emit_v),
        grid=(n // tm,),
        in_specs=[
            pl.BlockSpec((tm, d), lambda i: (i, 0)),
            pl.BlockSpec((1, d), lambda i: (0, 0)),
            pl.BlockSpec((d, d2), lambda i: (0, 0)),
            pl.BlockSpec((1, d2), lambda i: (0, 0)),
            pl.BlockSpec((1, d_sg), lambda i: (0, 0)),
            pl.BlockSpec(ws.shape, lambda i: (0, 0, 0)),
            pl.BlockSpec((c_len, SG_GROUPS), lambda i: (0, 0)),
            pl.BlockSpec((d_sg, d), lambda i: (0, 0)),
        ],
        out_specs=tuple(out_specs),
        out_shape=tuple(out_shape),
        scratch_shapes=[pltpu.VMEM((tm, d_sg), BF16)],
        compiler_params=_params(),
        name="sgu",
    )(h, g.reshape(1, d), win, b_in.reshape(1, d2), gv.reshape(1, d_sg), ws, bs_t, wout)
    return res


FFN_CHUNK = 256


def _ffn_body(x, p_ref, gf_ref, wup_ref, cw_ref, cb_ref, wdn_ref, gp_ref, wple_ref, wgate_ref, gfin_ref,
              o_ref, act_ref, shifted, emit_up, final_norm):
    d_ff = wdn_ref.shape[0]
    xn = _rms(x, gf_ref[...]).astype(BF16)

    def conv(cols):
        up = _dot(xn, wup_ref[:, cols])
        m1, m2 = shifted(up, cols)
        emit_up(up, cols)
        return cb_ref[:, cols] + cw_ref[0:1, cols] * m2 + cw_ref[1:2, cols] * m1 + cw_ref[2:3, cols] * up

    for c in range(d_ff // FFN_CHUNK):
        gate = conv(slice(c * FFN_CHUNK, (c + 1) * FFN_CHUNK))
        val = conv(slice(d_ff + c * FFN_CHUNK, d_ff + (c + 1) * FFN_CHUNK))
        act_ref[:, c * FFN_CHUNK:(c + 1) * FFN_CHUNK] = (gate * _sigmoid(gate) * val).astype(BF16)
    h2 = x + _dot(act_ref[...], wdn_ref[...])
    gate = _sigmoid(_dot(_rms(h2, gp_ref[...]).astype(BF16), wgate_ref[...]))
    h3 = h2 + _dot(p_ref[...].astype(BF16), wple_ref[...]) * gate
    o_ref[...] = _rms(h3, gfin_ref[...]) if final_norm else h3


def _ffn_prompt_kernel(*refs, tiles_per_seq, final_norm, with_attn):
    if with_attn:
        att_ref, wo_ref, *refs = refs
    (h_ref, p_ref, gf_ref, wup_ref, cw_ref, cb_ref, wdn_ref, gp_ref, wple_ref, wgate_ref, gfin_ref,
     o_ref, tail_ref, carry_ref, act_ref) = refs
    tm = h_ref.shape[0]
    i = pl.program_id(0)

    @pl.when(i % tiles_per_seq == 0)
    def _():
        carry_ref[...] = jnp.zeros(carry_ref.shape, F32)

    row = lax.broadcasted_iota(I32, (tm, FFN_CHUNK), 0)

    def shifted(up, cols):
        prev = carry_ref[:, cols]
        p1 = prev[SUBLANES - 1:SUBLANES]
        p2 = prev[SUBLANES - 2:SUBLANES - 1]
        m1 = jnp.where(row >= 1, pltpu.roll(up, 1, 0), p1)
        m2 = jnp.where(row >= 2, pltpu.roll(up, 2, 0), jnp.where(row == 0, p2, p1))
        return m1, m2

    def emit_up(up, cols):
        last = up[tm - SUBLANES:tm]
        carry_ref[:, cols] = last
        tail_ref[0, :, cols] = last

    x = h_ref[...]
    if with_attn:
        x = x + _dot(att_ref[...], wo_ref[...])
    _ffn_body(x, p_ref, gf_ref, wup_ref, cw_ref, cb_ref, wdn_ref, gp_ref, wple_ref, wgate_ref, gfin_ref,
              o_ref, act_ref, shifted, emit_up, final_norm)


def _ffn_sample_kernel(h_ref, p_ref, pm1_ref, pm2_ref, gf_ref, wup_ref, cw_ref, cb_ref, wdn_ref, gp_ref, wple_ref,
                       wgate_ref, gfin_ref, o_ref, up_ref, act_ref, *, seg, final_norm):
    tm = h_ref.shape[0]
    row = lax.broadcasted_iota(I32, (tm, FFN_CHUNK), 0) & ((1 << _log2(seg)) - 1)

    def shifted(up, cols):
        m1 = jnp.where(row >= 1, pltpu.roll(up, 1, 0), pm1_ref[:, cols])
        m2 = jnp.where(row >= 2, pltpu.roll(up, 2, 0), pm2_ref[:, cols])
        return m1, m2

    def emit_up(up, cols):
        up_ref[:, cols] = up

    _ffn_body(h_ref[...], p_ref, gf_ref, wup_ref, cw_ref, cb_ref, wdn_ref, gp_ref, wple_ref, wgate_ref, gfin_ref,
              o_ref, act_ref, shifted, emit_up, final_norm)


def _ffn_weight_specs(d, f2, d_ff, ple, layer):
    full = lambda shape: pl.BlockSpec(shape, lambda i: (0,) * len(shape))
    of_layer = lambda shape: pl.BlockSpec((None,) + shape, lambda i: (layer, 0, 0), pipeline_mode=pl.Buffered(1))
    return [of_layer((1, d)), of_layer((d, f2)), of_layer((CONV_W, f2)), of_layer((1, f2)), of_layer((d_ff, d)),
            of_layer((1, d)), of_layer((ple, d)), of_layer((d, d)), full((1, d))]


def _ffn_prompt_call(h, att, wo, p_all, layer, weights, tm, tiles_per_seq, final_norm):
    n, d = h.shape
    gf, wup, cw, cb, wdn, gp, wple, wgate, gfin = weights
    f2 = wup.shape[2]
    d_ff = wdn.shape[1]
    ple = p_all.shape[2]
    nt = n // tm
    with_attn = att is not None
    attn_specs = [pl.BlockSpec((tm, d), lambda i: (i, 0)), pl.BlockSpec((d, d), lambda i: (0, 0))] if with_attn else []
    attn_args = (att, wo) if with_attn else ()
    return pl.pallas_call(
        functools.partial(_ffn_prompt_kernel, tiles_per_seq=tiles_per_seq, final_norm=final_norm,
                          with_attn=with_attn),
        grid=(nt,),
        in_specs=attn_specs
        + [pl.BlockSpec((tm, d), lambda i: (i, 0)), pl.BlockSpec((None, tm, ple), lambda i: (layer, i, 0))]
        + _ffn_weight_specs(d, f2, d_ff, ple, layer),
        out_specs=(pl.BlockSpec((tm, d), lambda i: (i, 0)), pl.BlockSpec((1, SUBLANES, f2), lambda i: (i, 0, 0))),
        out_shape=(jax.ShapeDtypeStruct((n, d), F32), jax.ShapeDtypeStruct((nt, SUBLANES, f2), F32)),
        scratch_shapes=[pltpu.VMEM((SUBLANES, f2), F32), pltpu.VMEM((tm, d_ff), BF16)],
        compiler_params=_params(),
        name="ffn_prompt",
    )(*attn_args, h, p_all, gf, wup, cw, cb, wdn, gp, wple, wgate, gfin)


def _ffn_sample_call(h, p, pm1, pm2, layer, weights, seg, final_norm):
    n, d = h.shape
    gf, wup, cw, cb, wdn, gp, wple, wgate, gfin = weights
    f2 = wup.shape[2]
    d_ff = wdn.shape[1]
    ple = p.shape[2]
    full = lambda shape: pl.BlockSpec(shape, lambda i: (0,) * len(shape))
    of_layer = lambda shape: pl.BlockSpec((None,) + shape, lambda i: (layer, 0, 0))
    return pl.pallas_call(
        functools.partial(_ffn_sample_kernel, seg=seg, final_norm=final_norm),
        grid=(1,),
        in_specs=[full((n, d)), of_layer((n, ple)), of_layer((n, f2)), of_layer((n, f2))]
        + _ffn_weight_specs(d, f2, d_ff, ple, layer),
        out_specs=(full((n, d)), full((n, f2))),
        out_shape=(jax.ShapeDtypeStruct((n, d), F32), jax.ShapeDtypeStruct((n, f2), F32)),
        scratch_shapes=[pltpu.VMEM((n, d_ff), BF16)],
        compiler_params=_params(),
        name="ffn_sample",
    )(h, p, pm1, pm2, gf, wup, cw, cb, wdn, gp, wple, wgate, gfin)


PROMPT_TILE = 512


def kernel(x_prompt, x_sample, cache_k, cache_v, cache_idx_k, state_conv, page_table, p_prompt, p_sample,
           norm_mix, w_attn_in, w_attn_out, w_sg_in, b_sg_in, norm_sg_v, w_sg_spatial, b_sg_spatial, w_sg_out,
           norm_ffn, w_ffn_up, w_ffn_conv, b_ffn_conv, w_ffn_down, norm_ple, w_ple, w_ple_gate, norm_final):
    batch, seq, d = x_prompt.shape
    nb, ts, _ = x_sample.shape
    depth = norm_mix.shape[0]
    page = cache_k.shape[2]
    past = page_table.shape[1] * page
    f2 = w_ffn_up.shape[2]
    n_s = nb * ts
    tm = PROMPT_TILE

    hp = x_prompt.reshape(batch * seq, d)
    hs = x_sample.reshape(n_s, d)
    tabs_p = _rope_tables(jnp.arange(seq))
    tabs_s = tuple(jnp.tile(t, (nb, 1)) for t in _rope_tables(past + jnp.arange(ts)))
    zeros = jnp.zeros((depth, nb, ts - 1, f2), F32)
    pm1 = jnp.concatenate([state_conv[:, :, 1:2], zeros], axis=2).reshape(depth, n_s, f2)
    pm2 = jnp.concatenate([state_conv, zeros[:, :, 1:]], axis=2).reshape(depth, n_s, f2)
    ffn_weights = (norm_ffn.reshape(depth, 1, d), w_ffn_up.astype(BF16), w_ffn_conv, b_ffn_conv.reshape(depth, 1, f2),
                   w_ffn_down.astype(BF16), norm_ple.reshape(depth, 1, d), w_ple.astype(BF16),
                   w_ple_gate.astype(BF16), norm_final.reshape(1, d))

    kp_l, vp_l, ikp_l, ks_l, vs_l, iks_l, cp_l, cs_l, sgv_l = [], [], [], [], [], [], [], [], []
    for i in range(depth):
        j = i // 2
        att = w_out = None
        if i % 2 == 0:
            w_in = jnp.pad(w_attn_in[j], ((0, 0), (0, PROJ_W - w_attn_in.shape[2]))).astype(BF16)
            w_out = w_attn_out[j].astype(BF16)
            q8, k, v, kb, vt, iq4, ikt, misct, ikb = _proj_call(hp, norm_mix[i], w_in, tabs_p, tm, True)
            att = _dsa_prompt_call(q8, iq4, misct, ikb, kb, vt, batch, seq)
            kp_l.append(k.reshape(batch, seq, N_KV_HEADS, HEAD_DIM))
            vp_l.append(v.reshape(batch, seq, N_KV_HEADS, HEAD_DIM))
            ikp_l.append(jnp.swapaxes(ikt, 1, 2))
            q8, k, v, iq4, misc = _proj_call(hs, norm_mix[i], w_in, tabs_s, n_s, False)
            hs = _dsa_sample_call(hs, q8, iq4, misc, k, v, cache_k, cache_v, cache_idx_k, j,
                                  page_table, w_out, ts)
            ks_l.append(k.reshape(nb, ts, N_KV_HEADS, HEAD_DIM))
            vs_l.append(v.reshape(nb, ts, N_KV_HEADS, HEAD_DIM))
            iks_l.append(misc[:, :IDX_DIM].reshape(nb, ts, IDX_DIM))
        else:
            win = w_sg_in[j].astype(BF16)
            wout = w_sg_out[j].astype(BF16)
            (hp,) = _sgu_call(hp, norm_mix[i], win, b_sg_in[j], norm_sg_v[j], w_sg_spatial[j],
                              b_sg_spatial[j].T, wout, tm, SG_CHUNK, False)
            ws_s = jnp.tile(jnp.pad(w_sg_spatial[j][:, :ts, :ts], ((0, 0), (0, 0), (0, LANES - ts))), (1, nb, 1))
            bs_s = jnp.tile(b_sg_spatial[j].T[:ts], (nb, 1))
            hs, v_rows = _sgu_call(hs, norm_mix[i], win, b_sg_in[j], norm_sg_v[j], ws_s, bs_s, wout, n_s, ts, True)
            sgv_l.append(v_rows.reshape(nb, ts, -1))

        final = i == depth - 1
        weights = ffn_weights
        hp, tail = _ffn_prompt_call(hp, att, w_out, p_prompt.reshape(depth, batch * seq, -1), i, weights,
                                    tm, seq // tm, final)
        cp_l.append(tail.reshape(batch, seq // tm, SUBLANES, f2)[:, -1, SUBLANES - (CONV_W - 1):])
        hs, up_s = _ffn_sample_call(hs, p_sample.reshape(depth, n_s, -1), pm1, pm2, i, weights, ts, final)
        cs_l.append(up_s.reshape(nb, ts, f2)[:, ts - (CONV_W - 1):])

    return (hp.reshape(batch, seq, d), hs.reshape(nb, ts, d),
            jnp.stack(kp_l), jnp.stack(vp_l), jnp.stack(ikp_l),
            jnp.stack(ks_l), jnp.stack(vs_l), jnp.stack(iks_l),
            jnp.stack(cp_l), jnp.stack(cs_l), jnp.stack(sgv_l))
```

```python
import functools

import jax
import jax.numpy as jnp
from jax import lax
from jax.experimental import pallas as pl
from jax.experimental.pallas import tpu as pltpu

F32 = jnp.float32
BF16 = jnp.bfloat16
I32 = jnp.int32

N_HEADS = 8
N_KV_HEADS = 2
GROUP = N_HEADS // N_KV_HEADS
HEAD_DIM = 128
IDX_HEADS = 4
IDX_DIM = 64
IDX_W_SCALE = (IDX_HEADS * IDX_DIM) ** -0.5
TOPK_MAX = 256
SG_CHUNK = 128
SG_GROUPS = 8
CONV_W = 3
ROPE_THETA = 10000.0
EPS = 1e-6

LANES = 128
SUBLANES = 8
V7X_VMEM_BYTES = 64 * 1024 * 1024
VMEM_LIMIT_BYTES = V7X_VMEM_BYTES - 8 * 1024 * 1024

NEG_INF = float("-inf")
LOG2_E = 1.4426950408889634
INT_MIN = -(2 ** 31)


def _params(grid_rank=1):
    return pltpu.CompilerParams(dimension_semantics=("arbitrary",) * grid_rank,
                                vmem_limit_bytes=VMEM_LIMIT_BYTES)


def _log2(n):
    assert n > 0 and n & (n - 1) == 0, n
    return n.bit_length() - 1


def _rms(x, g):
    ms = jnp.mean(x * x, axis=-1, keepdims=True)
    return x * lax.rsqrt(ms + EPS) * g


def _sigmoid(x):
    return 1.0 / (1.0 + jnp.exp(-x))


def _dot(a, b):
    return jnp.dot(a, b, preferred_element_type=F32)


def _dot_nt(a, b):
    return lax.dot_general(a, b, (((1,), (1,)), ((), ())), preferred_element_type=F32)


def _sortable_key(score):
    b = pltpu.bitcast(score, I32)
    b = jnp.where(b == INT_MIN, 0, b)
    return b ^ ((b >> 31) & 0x7FFFFFFF)


Q_OFF = 0
K_OFF = N_HEADS * HEAD_DIM
V_OFF = K_OFF + N_KV_HEADS * HEAD_DIM
IQ_OFF = V_OFF + N_KV_HEADS * HEAD_DIM
MISC_OFF = IQ_OFF + IDX_HEADS * IDX_DIM
PROJ_W = MISC_OFF + LANES
KV_W = N_KV_HEADS * HEAD_DIM
ATTN_TILE = 256


def _project(h_ref, g_ref, w_ref, tab_refs, q8_ref, k_ref, v_ref, iq4_ref, queries_on_lanes):
    c128_ref, s128_ref, c64_ref, s64_ref, cm_ref, sm_ref = tab_refs
    tm = h_ref.shape[0]
    xn = _rms(h_ref[...], g_ref[...]).astype(BF16)
    y = _dot(xn, w_ref[...])
    c128 = c128_ref[...]
    s128 = s128_ref[...]

    def rope128(t):
        return t * c128 + pltpu.roll(t, HEAD_DIM // 2, 1) * s128

    for h in range(N_HEADS):
        qh = rope128(y[:, Q_OFF + h * HEAD_DIM:Q_OFF + (h + 1) * HEAD_DIM])
        q8_ref[h] = (qh.T if queries_on_lanes else qh).astype(q8_ref.dtype)
    ks = []
    for h in range(N_KV_HEADS):
        kh = rope128(y[:, K_OFF + h * HEAD_DIM:K_OFF + (h + 1) * HEAD_DIM])
        k_ref[pl.ds(h, tm, stride=N_KV_HEADS), :] = kh
        v_ref[pl.ds(h, tm, stride=N_KV_HEADS), :] = y[:, V_OFF + h * HEAD_DIM:V_OFF + (h + 1) * HEAD_DIM]
        ks.append(kh)
    v = y[:, V_OFF:IQ_OFF]

    lane = lax.broadcasted_iota(I32, (tm, LANES), 1)
    low_half = (lane & (IDX_DIM - 1)) < (IDX_DIM // 2)

    def rope64(t, c, s):
        partner = jnp.where(low_half, pltpu.roll(t, LANES - IDX_DIM // 2, 1), pltpu.roll(t, IDX_DIM // 2, 1))
        return t * c + partner * s

    c64 = c64_ref[...]
    s64 = s64_ref[...]
    for pair in range(IDX_HEADS // 2):
        t = rope64(y[:, IQ_OFF + pair * LANES:IQ_OFF + (pair + 1) * LANES], c64, s64)
        if queries_on_lanes:
            tt = t.T
            iq4_ref[2 * pair] = tt[:IDX_DIM, :].astype(iq4_ref.dtype)
            iq4_ref[2 * pair + 1] = tt[IDX_DIM:, :].astype(iq4_ref.dtype)
        else:
            iq4_ref[2 * pair] = t[:, :IDX_DIM].astype(iq4_ref.dtype)
            iq4_ref[2 * pair + 1] = t[:, IDX_DIM:].astype(iq4_ref.dtype)
    m = rope64(y[:, MISC_OFF:MISC_OFF + LANES], cm_ref[...], sm_ref[...])
    return ks, v, m


def _proj_prompt_kernel(h_ref, g_ref, w_ref, c128_ref, s128_ref, c64_ref, s64_ref, cm_ref, sm_ref,
                        q8_ref, k_ref, v_ref, kb_ref, vt_ref, iq4_ref, ikt_ref, misct_ref, ikb_ref):
    tm = h_ref.shape[0]
    ks, v, m = _project(h_ref, g_ref, w_ref, (c128_ref, s128_ref, c64_ref, s64_ref, cm_ref, sm_ref),
                        q8_ref, k_ref, v_ref, iq4_ref, True)
    for h in range(N_KV_HEADS):
        kb_ref[:, h * HEAD_DIM:(h + 1) * HEAD_DIM] = ks[h].astype(BF16)
    for c in range(tm // ATTN_TILE):
        vt_ref[c] = v[c * ATTN_TILE:(c + 1) * ATTN_TILE].T.astype(BF16)
    mt = m.T
    misct_ref[...] = mt
    ikt_ref[0] = mt[:IDX_DIM, :]
    ikb_ref[...] = m[:, :IDX_DIM].astype(BF16)


def _proj_sample_kernel(h_ref, g_ref, w_ref, c128_ref, s128_ref, c64_ref, s64_ref, cm_ref, sm_ref,
                        q8_ref, k_ref, v_ref, iq4_ref, misc_ref):
    _, _, m = _project(h_ref, g_ref, w_ref, (c128_ref, s128_ref, c64_ref, s64_ref, cm_ref, sm_ref),
                       q8_ref, k_ref, v_ref, iq4_ref, False)
    misc_ref[...] = m


def _proj_call(h, g, w, tabs, tm, prompt):
    n, d = h.shape
    t_tab = tabs[0].shape[0]
    n_tab = t_tab // tm
    tab_spec = pl.BlockSpec((tm, LANES), lambda i: (i % n_tab, 0))
    row = lambda w_: pl.BlockSpec((tm, w_), lambda i: (i, 0))
    if prompt:
        q8 = (jax.ShapeDtypeStruct((N_HEADS, HEAD_DIM, n), BF16), pl.BlockSpec((N_HEADS, HEAD_DIM, tm), lambda i: (0, 0, i)))
        iq4 = (jax.ShapeDtypeStruct((IDX_HEADS, IDX_DIM, n), BF16), pl.BlockSpec((IDX_HEADS, IDX_DIM, tm), lambda i: (0, 0, i)))
    else:
        q8 = (jax.ShapeDtypeStruct((N_HEADS, n, HEAD_DIM), F32), pl.BlockSpec((N_HEADS, tm, HEAD_DIM), lambda i: (0, i, 0)))
        iq4 = (jax.ShapeDtypeStruct((IDX_HEADS, n, IDX_DIM), F32), pl.BlockSpec((IDX_HEADS, tm, IDX_DIM), lambda i: (0, i, 0)))
    kf = (jax.ShapeDtypeStruct((n * N_KV_HEADS, HEAD_DIM), F32),
          pl.BlockSpec((tm * N_KV_HEADS, HEAD_DIM), lambda i: (i, 0)))
    if prompt:
        assert tm % ATTN_TILE == 0
        outs = [q8, kf, kf, (jax.ShapeDtypeStruct((n, KV_W), BF16), row(KV_W)),
                (jax.ShapeDtypeStruct((n // ATTN_TILE, KV_W, ATTN_TILE), BF16),
                 pl.BlockSpec((tm // ATTN_TILE, KV_W, ATTN_TILE), lambda i: (i, 0, 0))),
                iq4,
                (jax.ShapeDtypeStruct((n // t_tab, IDX_DIM, t_tab), F32),
                 pl.BlockSpec((1, IDX_DIM, tm), lambda i: (i // n_tab, 0, i % n_tab))),
                (jax.ShapeDtypeStruct((LANES, n), F32), pl.BlockSpec((LANES, tm), lambda i: (0, i))),
                (jax.ShapeDtypeStruct((n, IDX_DIM), BF16), row(IDX_DIM))]
        body = _proj_prompt_kernel
    else:
        outs = [q8, kf, kf, iq4, (jax.ShapeDtypeStruct((n, LANES), F32), row(LANES))]
        body = _proj_sample_kernel
    return pl.pallas_call(
        body,
        grid=(n // tm,),
        in_specs=[
            pl.BlockSpec((tm, d), lambda i: (i, 0)),
            pl.BlockSpec((1, d), lambda i: (0, 0)),
            pl.BlockSpec((d, PROJ_W), lambda i: (0, 0)),
        ] + [tab_spec] * 6,
        out_specs=tuple(o[1] for o in outs),
        out_shape=tuple(o[0] for o in outs),
        compiler_params=_params(),
        name="attn_proj_prompt" if prompt else "attn_proj_sample",
    )(h, g.reshape(1, d), w, *tabs)


def _rope_tables(pos):
    t = pos.shape[0]

    def tab(half):
        inv = ROPE_THETA ** (-jnp.arange(half, dtype=F32) / half)
        ang = pos.astype(F32)[:, None] * inv[None, :]
        return jnp.cos(ang), jnp.sin(ang)

    c64, s64 = tab(HEAD_DIM // 2)
    c32, s32 = tab(IDX_DIM // 2)
    c128 = jnp.concatenate([c64, c64], axis=1)
    s128 = jnp.concatenate([-s64, s64], axis=1)
    cq = jnp.concatenate([c32, c32, c32, c32], axis=1)
    sq = jnp.concatenate([-s32, s32, -s32, s32], axis=1)
    cm = jnp.concatenate([c32, c32, jnp.full((t, LANES - IDX_DIM), IDX_W_SCALE, F32)], axis=1)
    sm = jnp.concatenate([-s32, s32, jnp.zeros((t, LANES - IDX_DIM), F32)], axis=1)
    return (c128, s128, cq, sq, cm, sm)


COUNT_SLAB = 64


def _col_count(key_ref, n_rows, indicator):
    w = key_ref.shape[1]
    acc = jnp.zeros((COUNT_SLAB, w), I32)
    for c in range(n_rows // COUNT_SLAB):
        acc = acc + indicator(key_ref[c * COUNT_SLAB:(c + 1) * COUNT_SLAB, :], c * COUNT_SLAB)
    return jnp.sum(acc, axis=0, keepdims=True)


I16 = jnp.int16
I16_MIN = -(2 ** 15)
HALF_SLAB = 128
PACK_ROWS = 16


def _bcast16(x, rows):
    one = jnp.broadcast_to(x, (PACK_ROWS, x.shape[1])).astype(I16)
    return jnp.concatenate([one] * (rows // PACK_ROWS), axis=0)


def _col_count16(half_ref, n_rows, indicator):
    w = half_ref.shape[1]
    acc = jnp.zeros((HALF_SLAB, w), I16)
    for c in range(n_rows // HALF_SLAB):
        acc = acc + indicator(half_ref[c * HALF_SLAB:(c + 1) * HALF_SLAB, :])
    return jnp.sum(acc.astype(I32), axis=0, keepdims=True)


def _kth_largest16(half_ref, n_rows, kk):
    w = half_ref.shape[1]
    one = jnp.ones((HALF_SLAB, w), I16)
    zero = jnp.zeros((HALF_SLAB, w), I16)

    def body(i, t):
        cand = t + lax.shift_left(jnp.int32(1), 15 - i)
        candb = _bcast16(cand, HALF_SLAB)
        cnt = _col_count16(half_ref, n_rows, lambda k: jnp.where(k >= candb, one, zero))
        return jnp.where(cnt >= kk, cand, t)

    return lax.fori_loop(0, 16, body, jnp.full((1, w), I16_MIN, I32))


def _select_bias_cols(key_ref, hi_ref, lo_ref, bias_ref, n_rows, kq):
    w = key_ref.shape[1]
    one = jnp.ones((HALF_SLAB, w), I16)
    zero = jnp.zeros((HALF_SLAB, w), I16)

    t_hi = _kth_largest16(hi_ref, n_rows, kq)
    t_hib = _bcast16(t_hi, HALF_SLAB)
    above = _col_count16(hi_ref, n_rows, lambda k: jnp.where(k > t_hib, one, zero))
    lowest = jnp.full((HALF_SLAB, w), I16_MIN, I16)
    for c in range(n_rows // HALF_SLAB):
        rows = slice(c * HALF_SLAB, (c + 1) * HALF_SLAB)
        lo_ref[rows, :] = jnp.where(hi_ref[rows, :] == t_hib, lo_ref[rows, :], lowest)
    t_lo = _kth_largest16(lo_ref, n_rows, kq - above)
    thr = t_hi * 65536 + (t_lo - I16_MIN)

    cnt_gt = _col_count(key_ref, n_rows, lambda k, r0: jnp.where(k > thr, 1, 0))
    cnt_ge = _col_count(key_ref, n_rows, lambda k, r0: jnp.where(k >= thr, 1, 0))
    need = kq - cnt_gt

    drop_ties = jnp.max(cnt_ge - kq) > 0

    @pl.when(jnp.logical_not(drop_ties))
    def _():
        for c in range(n_rows // COUNT_SLAB):
            rows = slice(c * COUNT_SLAB, (c + 1) * COUNT_SLAB)
            bias_ref[rows, :] = jnp.where(key_ref[rows, :] >= thr, 0.0, NEG_INF)

    @pl.when(drop_ties)
    def _():
        nbits = max(1, (n_rows - 1).bit_length())
        sub = lax.broadcasted_iota(I32, (COUNT_SLAB, w), 0)

        def lt_body(i, j):
            cand = j + lax.shift_left(jnp.int32(1), nbits - 1 - i)
            cnt = _col_count(key_ref, n_rows,
                             lambda k, r0: jnp.where(k == thr, jnp.where(sub + r0 < cand, 1, 0), 0))
            return jnp.where(cnt < need, cand, j)

        jmax = lax.fori_loop(0, nbits, lt_body, jnp.zeros((1, w), I32))
        for c in range(n_rows // COUNT_SLAB):
            rows = slice(c * COUNT_SLAB, (c + 1) * COUNT_SLAB)
            k = key_ref[rows, :]
            keep_tie = jnp.where(sub + c * COUNT_SLAB <= jmax, 0.0, NEG_INF)
            bias_ref[rows, :] = jnp.where(k > thr, 0.0, jnp.where(k == thr, keep_tie, NEG_INF))


def _for_chunks(n, body):
    def pair(p, carry):
        body([2 * p, 2 * p + 1])
        return carry

    lax.fori_loop(0, lax.shift_right_logical(n, 1), pair, 0)

    @pl.when((n & 1) == 1)
    def _():
        body([n - 1])


def _dsa_prompt_kernel(q8_ref, iq4_ref, misct_ref, ik_ref, k_ref, vt_ref, att_ref,
                       key_ref, hi_ref, lo_ref, bias_ref, logit_ref, acc_ref, stat_ref, *, k_sel):
    t = q8_ref.shape[2]
    nq = key_ref.shape[0] // t
    gw = GROUP * t
    qi = pl.program_id(1)
    n_chunks = qi + 1
    qpos = qi * t + lax.broadcasted_iota(I32, (1, t), 1)
    rows_of = lambda c: pl.ds(pl.multiple_of(c * t, t), t)

    iq = jnp.concatenate([iq4_ref[h] for h in range(IDX_HEADS)], axis=1)
    iw = misct_ref[IDX_DIM:IDX_DIM + SUBLANES, :]

    def score_chunks(cs):
        for c in cs:
            s = _dot(ik_ref[rows_of(c), :], iq)
            score = jnp.zeros((t, t), F32)
            for h in range(IDX_HEADS):
                score = score + jnp.maximum(s[:, h * t:(h + 1) * t], 0.0) * iw[h:h + 1, :]
            kpos = c * t + lax.broadcasted_iota(I32, (t, t), 0)
            key = _sortable_key(jnp.where(kpos <= qpos, score, NEG_INF))
            key_ref[rows_of(c), :] = key
            hi_ref[rows_of(c), :] = (key >> 16).astype(I16)
            lo_ref[rows_of(c), :] = ((key & 0xFFFF) + I16_MIN).astype(I16)

    _for_chunks(n_chunks, score_chunks)

    kq = jnp.minimum(k_sel, qpos + 1)
    for j in range(nq):
        @pl.when(qi == j)
        def _():
            _select_bias_cols(key_ref, hi_ref, lo_ref, bias_ref, (j + 1) * t, kq)

    qs = [jnp.concatenate([q8_ref[g * GROUP + hh] for hh in range(GROUP)], axis=1)
          for g in range(N_KV_HEADS)]
    head_cols = lambda g: slice(g * HEAD_DIM, (g + 1) * HEAD_DIM)
    lanes_of = lambda g: slice(g * gw, (g + 1) * gw)
    stat_ref[0:1, :] = jnp.full((1, N_KV_HEADS * gw), NEG_INF, F32)
    stat_ref[1:2, :] = jnp.zeros((1, N_KV_HEADS * gw), F32)
    acc_ref[...] = jnp.zeros(acc_ref.shape, F32)

    def logits_chunks(cs):
        for g in range(N_KV_HEADS):
            m = stat_ref[0:1, lanes_of(g)]
            for c in cs:
                lg = _dot(k_ref[rows_of(c), head_cols(g)], qs[g]) * (HEAD_DIM ** -0.5 * LOG2_E)
                lg = lg + jnp.concatenate([bias_ref[rows_of(c), :]] * GROUP, axis=1)
                logit_ref[rows_of(c), lanes_of(g)] = lg
                m = jnp.maximum(m, jnp.max(lg, axis=0, keepdims=True))
            stat_ref[0:1, lanes_of(g)] = m

    _for_chunks(n_chunks, logits_chunks)

    def pv_chunks(cs):
        for g in range(N_KV_HEADS):
            m = stat_ref[0:1, lanes_of(g)]
            den = stat_ref[1:2, lanes_of(g)]
            for c in cs:
                e = jnp.exp2(logit_ref[rows_of(c), lanes_of(g)] - m)
                acc_ref[g] += _dot(vt_ref[c, head_cols(g), :], e.astype(BF16))
                den = den + jnp.sum(e, axis=0, keepdims=True)
            stat_ref[1:2, lanes_of(g)] = den

    _for_chunks(n_chunks, pv_chunks)

    for g in range(N_KV_HEADS):
        o = acc_ref[g] / stat_ref[1:2, lanes_of(g)]
        for hh in range(GROUP):
            head = g * GROUP + hh
            att_ref[:, head * HEAD_DIM:(head + 1) * HEAD_DIM] = o[:, hh * t:(hh + 1) * t].T.astype(BF16)


def _dsa_prompt_call(q8, iq4, misct, ikb, kb, vt, batch, seq):
    t = ATTN_TILE
    n = batch * seq
    d = N_HEADS * HEAD_DIM
    nq = seq // t
    k_sel = min(TOPK_MAX, seq // 4)
    return pl.pallas_call(
        functools.partial(_dsa_prompt_kernel, k_sel=k_sel),
        grid=(batch, nq),
        in_specs=[
            pl.BlockSpec((N_HEADS, HEAD_DIM, t), lambda b, q: (0, 0, b * nq + q)),
            pl.BlockSpec((IDX_HEADS, IDX_DIM, t), lambda b, q: (0, 0, b * nq + q)),
            pl.BlockSpec((LANES, t), lambda b, q: (0, b * nq + q)),
            pl.BlockSpec((seq, IDX_DIM), lambda b, q: (b, 0)),
            pl.BlockSpec((seq, KV_W), lambda b, q: (b, 0)),
            pl.BlockSpec((nq, KV_W, t), lambda b, q: (b, 0, 0)),
        ],
        out_specs=pl.BlockSpec((t, d), lambda b, q: (b * nq + q, 0)),
        out_shape=jax.ShapeDtypeStruct((n, d), BF16),
        scratch_shapes=[
            pltpu.VMEM((seq, t), I32),
            pltpu.VMEM((seq, t), I16),
            pltpu.VMEM((seq, t), I16),
            pltpu.VMEM((seq, t), F32),
            pltpu.VMEM((seq, N_KV_HEADS * GROUP * t), F32),
            pltpu.VMEM((N_KV_HEADS, HEAD_DIM, GROUP * t), F32),
            pltpu.VMEM((SUBLANES, N_KV_HEADS * GROUP * t), F32),
        ],
        compiler_params=_params(grid_rank=2),
        name="dsa_prompt",
    )(q8, iq4, misct, ikb, kb, vt)


def _row_count(key_ref, indicator):
    r, l = key_ref.shape
    acc = jnp.zeros((r, LANES), I32)
    for c in range(l // LANES):
        acc = acc + indicator(key_ref[:, c * LANES:(c + 1) * LANES], c)
    return jnp.sum(acc, axis=-1, keepdims=True)


def _select_bias_rows(key_ref, bias_ref, jmax_ref, kq):
    r, l = key_ref.shape

    def ge_body(i, t):
        cand = t + lax.shift_left(jnp.int32(1), 31 - i)
        candb = jnp.broadcast_to(cand, (r, LANES))
        cnt = _row_count(key_ref, lambda k, c: jnp.where(k >= candb, 1, 0))
        return jnp.where(cnt >= kq, cand, t)

    thr = lax.fori_loop(0, 32, ge_body, jnp.full((r, 1), INT_MIN, I32))
    thrb = jnp.broadcast_to(thr, (r, LANES))
    cnt_gt = _row_count(key_ref, lambda k, c: jnp.where(k > thrb, 1, 0))
    cnt_ge = _row_count(key_ref, lambda k, c: jnp.where(k >= thrb, 1, 0))
    need = kq - cnt_gt

    jmax_ref[...] = jnp.full(jmax_ref.shape, l, I32)
    lane = lax.broadcasted_iota(I32, (r, LANES), 1)

    @pl.when(jnp.max(cnt_ge - kq) > 0)
    def _():
        nbits = max(1, (l - 1).bit_length())

        def lt_body(i, j):
            cand = j + lax.shift_left(jnp.int32(1), nbits - 1 - i)
            candb = jnp.broadcast_to(cand, (r, LANES))
            cnt = _row_count(key_ref, lambda k, c: jnp.where(k == thrb, jnp.where(lane + c * LANES < candb, 1, 0), 0))
            return jnp.where(cnt < need, cand, j)

        jmax_ref[...] = jnp.broadcast_to(lax.fori_loop(0, nbits, lt_body, jnp.zeros((r, 1), I32)), jmax_ref.shape)

    jmaxb = jmax_ref[...]
    for c in range(l // LANES):
        k = key_ref[:, c * LANES:(c + 1) * LANES]
        keep_tie = jnp.where(lane + c * LANES <= jmaxb, 0.0, NEG_INF)
        bias_ref[:, c * LANES:(c + 1) * LANES] = jnp.where(k > thrb, 0.0, jnp.where(k == thrb, keep_tie, NEG_INF))


SELECT_SEQS = 8
ATTEND_SEQS = 2


def _sample_select_kernel(pt_ref, iq4_ref, misc_ref, cikt_ref, bias_ref, iktbuf, iknew, key_ref, jmax_ref, sem,
                          *, k_sel, layer, ts):
    page = cikt_ref.shape[3]
    n_pages = pt_ref.shape[1]
    past = n_pages * page
    l_pad = past + LANES
    seqs = iktbuf.shape[0]
    step = pl.program_id(0)

    def page_copy(i, p):
        lanes = pl.ds(pl.multiple_of(p * page, page), page)
        return pltpu.make_async_copy(cikt_ref.at[layer, pt_ref[step * seqs + i, p]], iktbuf.at[i, :, lanes], sem.at[0])

    for i in range(seqs):
        lax.fori_loop(0, n_pages, lambda p, c, i=i: (page_copy(i, p).start(priority=i % 2), c)[1], 0)

    iknew[...] = jnp.zeros(iknew.shape, F32)
    for i in range(seqs):
        iknew[i, 0:ts, :] = misc_ref[i * ts:(i + 1) * ts, :IDX_DIM]

    for i in range(seqs):
        lax.fori_loop(0, n_pages, lambda p, c, i=i: (page_copy(i, p).wait(), c)[1], 0)

    qpos = past + lax.broadcasted_iota(I32, (ts, 1), 0)
    kpos = lax.broadcasted_iota(I32, (ts, l_pad), 1)
    for i in range(seqs):
        rows = slice(i * ts, (i + 1) * ts)
        iq = iq4_ref[:, rows, :].reshape(IDX_HEADS * ts, IDX_DIM).astype(BF16)
        s = jnp.concatenate([_dot(iq, iktbuf[i].astype(BF16)), _dot_nt(iq, iknew[i].astype(BF16))], axis=1)
        iw = misc_ref[rows, IDX_DIM:IDX_DIM + IDX_HEADS]
        score = jnp.zeros((ts, l_pad), F32)
        for h in range(IDX_HEADS):
            score = score + jnp.maximum(s[h * ts:(h + 1) * ts], 0.0) * iw[:, h:h + 1]
        key_ref[rows, :] = _sortable_key(jnp.where(kpos <= qpos, score, NEG_INF))
    kq = jnp.minimum(k_sel, jnp.concatenate([qpos] * seqs, axis=0) + 1)
    _select_bias_rows(key_ref, bias_ref, jmax_ref, kq)


def _sample_attend_kernel(pt_ref, h_ref, q8_ref, kn_ref, vn_ref, bias_ref, ck_ref, cv_ref, wo_ref, o_ref,
                          kbuf, vbuf, knew, vnew, att_ref, sems, *, layer, ts):
    page_rows = ck_ref.shape[2]
    n_pages = pt_ref.shape[1]
    past = n_pages * page_rows // N_KV_HEADS
    seqs = kbuf.shape[0]
    step = pl.program_id(0)
    n_steps = pl.num_programs(0)

    first = step * seqs

    def page_copies(seq, slot, p):
        src = pt_ref[seq, p]
        rows = pl.ds(pl.multiple_of(p * page_rows, page_rows), page_rows)
        return (pltpu.make_async_copy(ck_ref.at[layer, src], kbuf.at[slot, rows], sems.at[0, slot]),
                pltpu.make_async_copy(cv_ref.at[layer, src], vbuf.at[slot, rows], sems.at[1, slot]))

    def start_seq(seq, slot):
        def body(p, c):
            for cp in page_copies(seq, slot, p):
                cp.start()
            return c
        lax.fori_loop(0, n_pages, body, 0)

    def wait_seq(seq, slot):
        def body(p, c):
            for cp in page_copies(seq, slot, p):
                cp.wait()
            return c
        lax.fori_loop(0, n_pages, body, 0)

    @pl.when(step == 0)
    def _():
        start_seq(first, 0)

    pad = jnp.zeros((LANES - ts, KV_W), F32)
    for i in range(seqs):
        rows = slice(i * ts, (i + 1) * ts)
        if i + 1 < seqs:
            start_seq(first + i + 1, i + 1)
        else:
            @pl.when(step + 1 < n_steps)
            def _():
                start_seq(first + seqs, 0)
        wait_seq(first + i, i)
        for g in range(N_KV_HEADS):
            new_rows = pl.ds(i * ts * N_KV_HEADS + g, ts, stride=N_KV_HEADS)
            knew[0:ts, g * HEAD_DIM:(g + 1) * HEAD_DIM] = kn_ref[new_rows, :]
            vnew[0:ts, g * HEAD_DIM:(g + 1) * HEAD_DIM] = vn_ref[new_rows, :]
        knew[ts:, :] = pad
        vnew[ts:, :] = pad
        bias = jnp.concatenate([bias_ref[rows, :]] * GROUP, axis=0)
        row0 = pl.multiple_of((step * seqs + i) * ts, ts)
        for g in range(N_KV_HEADS):
            cols = slice(g * HEAD_DIM, (g + 1) * HEAD_DIM)
            k_past = kbuf[i, pl.ds(g, past, stride=N_KV_HEADS), :].astype(BF16)
            v_past = vbuf[i, pl.ds(g, past, stride=N_KV_HEADS), :].astype(BF16)
            q = q8_ref[g * GROUP:(g + 1) * GROUP, rows, :].reshape(GROUP * ts, HEAD_DIM).astype(BF16)
            logits = jnp.concatenate([_dot_nt(q, k_past), _dot_nt(q, knew[:, cols].astype(BF16))], axis=1)
            logits = logits * (HEAD_DIM ** -0.5) + bias
            m = jnp.max(logits, axis=-1, keepdims=True)
            e = jnp.exp(logits - m)
            den = jnp.sum(e, axis=-1, keepdims=True)
            eb = e.astype(BF16)
            o = (_dot(eb[:, :past], v_past) + _dot(eb[:, past:], vnew[:, cols].astype(BF16))) / den
            for hh in range(GROUP):
                head = g * GROUP + hh
                att_ref[pl.ds(row0, ts), head * HEAD_DIM:(head + 1) * HEAD_DIM] = o[hh * ts:(hh + 1) * ts]

    @pl.when(step == n_steps - 1)
    def _():
        o_ref[...] = h_ref[...] + _dot(att_ref[...].astype(BF16), wo_ref[...])


def _dsa_sample_call(h, q8, iq4, misc, k_new, v_new, cache_k, cache_v, cache_ik, layer, page_table, wo, ts):
    n, d = h.shape
    nb = n // ts
    n_layers, n_pool, page = cache_k.shape[:3]
    n_pages = page_table.shape[1]
    past = n_pages * page
    l_pad = past + LANES
    k_sel = min(TOPK_MAX, (past + ts) // 4)
    ck = cache_k.reshape(n_layers, n_pool, page * N_KV_HEADS, HEAD_DIM)
    cv = cache_v.reshape(n_layers, n_pool, page * N_KV_HEADS, HEAD_DIM)
    cikt = jnp.swapaxes(cache_ik, 2, 3)
    sel_rows = SELECT_SEQS * ts
    bias = pl.pallas_call(
        functools.partial(_sample_select_kernel, k_sel=k_sel, layer=layer, ts=ts),
        grid_spec=pltpu.PrefetchScalarGridSpec(
            num_scalar_prefetch=1,
            grid=(nb // SELECT_SEQS,),
            in_specs=[
                pl.BlockSpec((IDX_HEADS, sel_rows, IDX_DIM), lambda s, pt: (0, s, 0)),
                pl.BlockSpec((sel_rows, LANES), lambda s, pt: (s, 0)),
                pl.BlockSpec(memory_space=pl.ANY),
            ],
            out_specs=pl.BlockSpec((sel_rows, l_pad), lambda s, pt: (s, 0)),
            scratch_shapes=[
                pltpu.VMEM((SELECT_SEQS, IDX_DIM, past), F32),
                pltpu.VMEM((SELECT_SEQS, LANES, IDX_DIM), F32),
                pltpu.VMEM((sel_rows, l_pad), I32),
                pltpu.VMEM((sel_rows, LANES), I32),
                pltpu.SemaphoreType.DMA((1,)),
            ],
        ),
        out_shape=jax.ShapeDtypeStruct((n, l_pad), F32),
        compiler_params=_params(),
        name="sample_select",
    )(page_table, iq4, misc, cikt)

    att_rows = ATTEND_SEQS * ts
    return pl.pallas_call(
        functools.partial(_sample_attend_kernel, layer=layer, ts=ts),
        grid_spec=pltpu.PrefetchScalarGridSpec(
            num_scalar_prefetch=1,
            grid=(nb // ATTEND_SEQS,),
            in_specs=[
                pl.BlockSpec((n, d), lambda s, pt: (0, 0)),
                pl.BlockSpec((N_HEADS, att_rows, HEAD_DIM), lambda s, pt: (0, s, 0)),
                pl.BlockSpec((att_rows * N_KV_HEADS, HEAD_DIM), lambda s, pt: (s, 0)),
                pl.BlockSpec((att_rows * N_KV_HEADS, HEAD_DIM), lambda s, pt: (s, 0)),
                pl.BlockSpec((att_rows, l_pad), lambda s, pt: (s, 0)),
                pl.BlockSpec(memory_space=pl.ANY),
                pl.BlockSpec(memory_space=pl.ANY),
                pl.BlockSpec((d, d), lambda s, pt: (0, 0)),
            ],
            out_specs=pl.BlockSpec((n, d), lambda s, pt: (0, 0)),
            scratch_shapes=[
                pltpu.VMEM((ATTEND_SEQS, past * N_KV_HEADS, HEAD_DIM), F32),
                pltpu.VMEM((ATTEND_SEQS, past * N_KV_HEADS, HEAD_DIM), F32),
                pltpu.VMEM((LANES, KV_W), F32),
                pltpu.VMEM((LANES, KV_W), F32),
                pltpu.VMEM((n, d), F32),
                pltpu.SemaphoreType.DMA((2, ATTEND_SEQS)),
            ],
        ),
        out_shape=jax.ShapeDtypeStruct((n, d), F32),
        compiler_params=_params(),
        name="sample_attend",
    )(page_table, h, q8, k_new, v_new, bias, ck, cv, wo)


def _sgu_kernel(h_ref, g_ref, win_ref, bin_ref, gv_ref, ws_ref, bs_ref, wout_ref, o_ref, *rest,
                seg, emit_v):
    if emit_v:
        v_ref, gated_ref = rest
    else:
        (gated_ref,) = rest
    tm, d = h_ref.shape
    c_len = ws_ref.shape[1]
    d_sg = gv_ref.shape[1]
    gw = d_sg // SG_GROUPS
    x = h_ref[...]
    xn = _rms(x, g_ref[...]).astype(BF16)
    z = _dot(xn, win_ref[...]) + bin_ref[...]
    z = 0.5 * z * (1.0 + lax.erf(z * (0.5 ** 0.5)))
    u = z[:, :d_sg]
    v = _rms(z[:, d_sg:], gv_ref[...])
    if emit_v:
        v_ref[...] = v
    vb = v.astype(BF16)
    row = lax.broadcasted_iota(I32, (c_len, c_len), 0)
    col = lax.broadcasted_iota(I32, (c_len, c_len), 1)
    same_seq = (row >> _log2(seg)) == (col >> _log2(seg))
    if seg != c_len:
        src = lax.broadcasted_iota(I32, (ws_ref.shape[2], c_len), 0)
        dst = lax.broadcasted_iota(I32, (ws_ref.shape[2], c_len), 1)
        spread = jnp.where((dst & (seg - 1)) == src, 1.0, 0.0).astype(BF16)
    for g in range(SG_GROUPS):
        wsg = ws_ref[g] if seg == c_len else _dot(ws_ref[g].astype(BF16), spread)
        wg = jnp.where(col <= row, jnp.where(same_seq, wsg, 0.0), 0.0).astype(BF16)
        bg = bs_ref[:, g:g + 1]
        for ch in range(tm // c_len):
            rows = slice(ch * c_len, (ch + 1) * c_len)
            cols = slice(g * gw, (g + 1) * gw)
            mixed = _dot(wg, vb[rows, cols]) + bg
            gated_ref[rows, cols] = (u[rows, cols] * mixed).astype(BF16)
    o_ref[...] = x + _dot(gated_ref[...], wout_ref[...])


def _sgu_call(h, g, win, b_in, gv, ws, bs_t, wout, tm, seg, emit_v):
    n, d = h.shape
    d2 = win.shape[1]
    d_sg = d2 // 2
    c_len = ws.shape[1]
    out_shape = [jax.ShapeDtypeStruct((n, d), F32)]
    out_specs = [pl.BlockSpec((tm, d), lambda i: (i, 0))]
    if emit_v:
        out_shape.append(jax.ShapeDtypeStruct((n, d_sg), F32))
        out_specs.append(pl.BlockSpec((tm, d_sg), lambda i: (i, 0)))
    res = pl.pallas_call(
        functools.partial(_sgu_kernel, seg=seg, emit_v=emit_v),
        grid=(n // tm,),
        in_specs=[
            pl.BlockSpec((tm, d), lambda i: (i, 0)),
            pl.BlockSpec((1, d), lambda i: (0, 0)),
            pl.BlockSpec((d, d2), lambda i: (0, 0)),
            pl.BlockSpec((1, d2), lambda i: (0, 0)),
            pl.BlockSpec((1, d_sg), lambda i: (0, 0)),
            pl.BlockSpec(ws.shape, lambda i: (0, 0, 0)),
            pl.BlockSpec((c_len, SG_GROUPS), lambda i: (0, 0)),
            pl.BlockSpec((d_sg, d), lambda i: (0, 0)),
        ],
        out_specs=tuple(out_specs),
        out_shape=tuple(out_shape),
        scratch_shapes=[pltpu.VMEM((tm, d_sg), BF16)],
        compiler_params=_params(),
        name="sgu",
    )(h, g.reshape(1, d), win, b_in.reshape(1, d2), gv.reshape(1, d_sg), ws, bs_t, wout)
    return res


FFN_CHUNK = 256


def _ffn_body(x, p_ref, gf_ref, wup_ref, cw_ref, cb_ref, wdn_ref, gp_ref, wple_ref, wgate_ref, gfin_ref,
              o_ref, act_ref, shifted, emit_up, final_norm):
    d_ff = wdn_ref.shape[0]
    xn = _rms(x, gf_ref[...]).astype(BF16)

    def conv(cols):
        up = _dot(xn, wup_ref[:, cols])
        m1, m2 = shifted(up, cols)
        emit_up(up, cols)
        return cb_ref[:, cols] + cw_ref[0:1, cols] * m2 + cw_ref[1:2, cols] * m1 + cw_ref[2:3, cols] * up

    for c in range(d_ff // FFN_CHUNK):
        gate = conv(slice(c * FFN_CHUNK, (c + 1) * FFN_CHUNK))
        val = conv(slice(d_ff + c * FFN_CHUNK, d_ff + (c + 1) * FFN_CHUNK))
        act_ref[:, c * FFN_CHUNK:(c + 1) * FFN_CHUNK] = (gate * _sigmoid(gate) * val).astype(BF16)
    h2 = x + _dot(act_ref[...], wdn_ref[...])
    gate = _sigmoid(_dot(_rms(h2, gp_ref[...]).astype(BF16), wgate_ref[...]))
    h3 = h2 + _dot(p_ref[...].astype(BF16), wple_ref[...]) * gate
    o_ref[...] = _rms(h3, gfin_ref[...]) if final_norm else h3


def _ffn_prompt_kernel(*refs, tiles_per_seq, final_norm, with_attn):
    if with_attn:
        att_ref, wo_ref, *refs = refs
    (h_ref, p_ref, gf_ref, wup_ref, cw_ref, cb_ref, wdn_ref, gp_ref, wple_ref, wgate_ref, gfin_ref,
     o_ref, tail_ref, carry_ref, act_ref) = refs
    tm = h_ref.shape[0]
    i = pl.program_id(0)

    @pl.when(i % tiles_per_seq == 0)
    def _():
        carry_ref[...] = jnp.zeros(carry_ref.shape, F32)

    row = lax.broadcasted_iota(I32, (tm, FFN_CHUNK), 0)

    def shifted(up, cols):
        prev = carry_ref[:, cols]
        p1 = prev[SUBLANES - 1:SUBLANES]
        p2 = prev[SUBLANES - 2:SUBLANES - 1]
        m1 = jnp.where(row >= 1, pltpu.roll(up, 1, 0), p1)
        m2 = jnp.where(row >= 2, pltpu.roll(up, 2, 0), jnp.where(row == 0, p2, p1))
        return m1, m2

    def emit_up(up, cols):
        last = up[tm - SUBLANES:tm]
        carry_ref[:, cols] = last
        tail_ref[0, :, cols] = last

    x = h_ref[...]
    if with_attn:
        x = x + _dot(att_ref[...], wo_ref[...])
    _ffn_body(x, p_ref, gf_ref, wup_ref, cw_ref, cb_ref, wdn_ref, gp_ref, wple_ref, wgate_ref, gfin_ref,
              o_ref, act_ref, shifted, emit_up, final_norm)


def _ffn_sample_kernel(h_ref, p_ref, pm1_ref, pm2_ref, gf_ref, wup_ref, cw_ref, cb_ref, wdn_ref, gp_ref, wple_ref,
                       wgate_ref, gfin_ref, o_ref, up_ref, act_ref, *, seg, final_norm):
    tm = h_ref.shape[0]
    row = lax.broadcasted_iota(I32, (tm, FFN_CHUNK), 0) & ((1 << _log2(seg)) - 1)

    def shifted(up, cols):
        m1 = jnp.where(row >= 1, pltpu.roll(up, 1, 0), pm1_ref[:, cols])
        m2 = jnp.where(row >= 2, pltpu.roll(up, 2, 0), pm2_ref[:, cols])
        return m1, m2

    def emit_up(up, cols):
        up_ref[:, cols] = up

    _ffn_body(h_ref[...], p_ref, gf_ref, wup_ref, cw_ref, cb_ref, wdn_ref, gp_ref, wple_ref, wgate_ref, gfin_ref,
              o_ref, act_ref, shifted, emit_up, final_norm)


def _ffn_weight_specs(d, f2, d_ff, ple, layer):
    full = lambda shape: pl.BlockSpec(shape, lambda i: (0,) * len(shape))
    of_layer = lambda shape: pl.BlockSpec((None,) + shape, lambda i: (layer, 0, 0), pipeline_mode=pl.Buffered(1))
    return [of_layer((1, d)), of_layer((d, f2)), of_layer((CONV_W, f2)), of_layer((1, f2)), of_layer((d_ff, d)),
            of_layer((1, d)), of_layer((ple, d)), of_layer((d, d)), full((1, d))]


def _ffn_prompt_call(h, att, wo, p_all, layer, weights, tm, tiles_per_seq, final_norm):
    n, d = h.shape
    gf, wup, cw, cb, wdn, gp, wple, wgate, gfin = weights
    f2 = wup.shape[2]
    d_ff = wdn.shape[1]
    ple = p_all.shape[2]
    nt = n // tm
    with_attn = att is not None
    attn_specs = [pl.BlockSpec((tm, d), lambda i: (i, 0)), pl.BlockSpec((d, d), lambda i: (0, 0))] if with_attn else []
    attn_args = (att, wo) if with_attn else ()
    return pl.pallas_call(
        functools.partial(_ffn_prompt_kernel, tiles_per_seq=tiles_per_seq, final_norm=final_norm,
                          with_attn=with_attn),
        grid=(nt,),
        in_specs=attn_specs
        + [pl.BlockSpec((tm, d), lambda i: (i, 0)), pl.BlockSpec((None, tm, ple), lambda i: (layer, i, 0))]
        + _ffn_weight_specs(d, f2, d_ff, ple, layer),
        out_specs=(pl.BlockSpec((tm, d), lambda i: (i, 0)), pl.BlockSpec((1, SUBLANES, f2), lambda i: (i, 0, 0))),
        out_shape=(jax.ShapeDtypeStruct((n, d), F32), jax.ShapeDtypeStruct((nt, SUBLANES, f2), F32)),
        scratch_shapes=[pltpu.VMEM((SUBLANES, f2), F32), pltpu.VMEM((tm, d_ff), BF16)],
        compiler_params=_params(),
        name="ffn_prompt",
    )(*attn_args, h, p_all, gf, wup, cw, cb, wdn, gp, wple, wgate, gfin)


def _ffn_sample_call(h, p, pm1, pm2, layer, weights, seg, final_norm):
    n, d = h.shape
    gf, wup, cw, cb, wdn, gp, wple, wgate, gfin = weights
    f2 = wup.shape[2]
    d_ff = wdn.shape[1]
    ple = p.shape[2]
    full = lambda shape: pl.BlockSpec(shape, lambda i: (0,) * len(shape))
    of_layer = lambda shape: pl.BlockSpec((None,) + shape, lambda i: (layer, 0, 0))
    return pl.pallas_call(
        functools.partial(_ffn_sample_kernel, seg=seg, final_norm=final_norm),
        grid=(1,),
        in_specs=[full((n, d)), of_layer((n, ple)), of_layer((n, f2)), of_layer((n, f2))]
        + _ffn_weight_specs(d, f2, d_ff, ple, layer),
        out_specs=(full((n, d)), full((n, f2))),
        out_shape=(jax.ShapeDtypeStruct((n, d), F32), jax.ShapeDtypeStruct((n, f2), F32)),
        scratch_shapes=[pltpu.VMEM((n, d_ff), BF16)],
        compiler_params=_params(),
        name="ffn_sample",
    )(h, p, pm1, pm2, gf, wup, cw, cb, wdn, gp, wple, wgate, gfin)


PROMPT_TILE = 512


def kernel(x_prompt, x_sample, cache_k, cache_v, cache_idx_k, state_conv, page_table, p_prompt, p_sample,
           norm_mix, w_attn_in, w_attn_out, w_sg_in, b_sg_in, norm_sg_v, w_sg_spatial, b_sg_spatial, w_sg_out,
           norm_ffn, w_ffn_up, w_ffn_conv, b_ffn_conv, w_ffn_down, norm_ple, w_ple, w_ple_gate, norm_final):
    batch, seq, d = x_prompt.shape
    nb, ts, _ = x_sample.shape
    depth = norm_mix.shape[0]
    page = cache_k.shape[2]
    past = page_table.shape[1] * page
    f2 = w_ffn_up.shape[2]
    n_s = nb * ts
    tm = PROMPT_TILE

    hp = x_prompt.reshape(batch * seq, d)
    hs = x_sample.reshape(n_s, d)
    tabs_p = _rope_tables(jnp.arange(seq))
    tabs_s = tuple(jnp.tile(t, (nb, 1)) for t in _rope_tables(past + jnp.arange(ts)))
    zeros = jnp.zeros((depth, nb, ts - 1, f2), F32)
    pm1 = jnp.concatenate([state_conv[:, :, 1:2], zeros], axis=2).reshape(depth, n_s, f2)
    pm2 = jnp.concatenate([state_conv, zeros[:, :, 1:]], axis=2).reshape(depth, n_s, f2)
    ffn_weights = (norm_ffn.reshape(depth, 1, d), w_ffn_up.astype(BF16), w_ffn_conv, b_ffn_conv.reshape(depth, 1, f2),
                   w_ffn_down.astype(BF16), norm_ple.reshape(depth, 1, d), w_ple.astype(BF16),
                   w_ple_gate.astype(BF16), norm_final.reshape(1, d))

    kp_l, vp_l, ikp_l, ks_l, vs_l, iks_l, cp_l, cs_l, sgv_l = [], [], [], [], [], [], [], [], []
    for i in range(depth):
        j = i // 2
        att = w_out = None
        if i % 2 == 0:
            w_in = jnp.pad(w_attn_in[j], ((0, 0), (0, PROJ_W - w_attn_in.shape[2]))).astype(BF16)
            w_out = w_attn_out[j].astype(BF16)
            q8, k, v, kb, vt, iq4, ikt, misct, ikb = _proj_call(hp, norm_mix[i], w_in, tabs_p, tm, True)
            att = _dsa_prompt_call(q8, iq4, misct, ikb, kb, vt, batch, seq)
            kp_l.append(k.reshape(batch, seq, N_KV_HEADS, HEAD_DIM))
            vp_l.append(v.reshape(batch, seq, N_KV_HEADS, HEAD_DIM))
            ikp_l.append(jnp.swapaxes(ikt, 1, 2))
            q8, k, v, iq4, misc = _proj_call(hs, norm_mix[i], w_in, tabs_s, n_s, False)
            hs = _dsa_sample_call(hs, q8, iq4, misc, k, v, cache_k, cache_v, cache_idx_k, j,
                                  page_table, w_out, ts)
            ks_l.append(k.reshape(nb, ts, N_KV_HEADS, HEAD_DIM))
            vs_l.append(v.reshape(nb, ts, N_KV_HEADS, HEAD_DIM))
            iks_l.append(misc[:, :IDX_DIM].reshape(nb, ts, IDX_DIM))
        else:
            win = w_sg_in[j].astype(BF16)
            wout = w_sg_out[j].astype(BF16)
            (hp,) = _sgu_call(hp, norm_mix[i], win, b_sg_in[j], norm_sg_v[j], w_sg_spatial[j],
                              b_sg_spatial[j].T, wout, tm, SG_CHUNK, False)
            ws_s = jnp.tile(jnp.pad(w_sg_spatial[j][:, :ts, :ts], ((0, 0), (0, 0), (0, LANES - ts))), (1, nb, 1))
            bs_s = jnp.tile(b_sg_spatial[j].T[:ts], (nb, 1))
            hs, v_rows = _sgu_call(hs, norm_mix[i], win, b_sg_in[j], norm_sg_v[j], ws_s, bs_s, wout, n_s, ts, True)
            sgv_l.append(v_rows.reshape(nb, ts, -1))

        final = i == depth - 1
        weights = ffn_weights
        hp, tail = _ffn_prompt_call(hp, att, w_out, p_prompt.reshape(depth, batch * seq, -1), i, weights,
                                    tm, seq // tm, final)
        cp_l.append(tail.reshape(batch, seq // tm, SUBLANES, f2)[:, -1, SUBLANES - (CONV_W - 1):])
        hs, up_s = _ffn_sample_call(hs, p_sample.reshape(depth, n_s, -1), pm1, pm2, i, weights, ts, final)
        cs_l.append(up_s.reshape(nb, ts, f2)[:, ts - (CONV_W - 1):])

    return (hp.reshape(batch, seq, d), hs.reshape(nb, ts, d),
            jnp.stack(kp_l), jnp.stack(vp_l), jnp.stack(ikp_l),
            jnp.stack(ks_l), jnp.stack(vs_l), jnp.stack(iks_l),
            jnp.stack(cp_l), jnp.stack(cs_l), jnp.stack(sgv_l))
```
